```python
import math
import jax, jax.numpy as jnp
from jax import lax
import numpy as np

D_MODEL = 2048
BATCH = 4
SEQ = 2048
DEPTH = 2
DEC_BATCH = 128
DEC_SEQ = 8
PAST_LEN = 16384
PAGE_SIZE = 128

N_META = 16
D_RNN = D_MODEL // 2
RG_BLOCKS = 8
RG_BS = D_RNN // RG_BLOCKS
CONV_W = 4
C_RG = 8.0
HG_HEADS = 8
HG_DK = (D_MODEL // 2) // HG_HEADS
HG_DV = (D_MODEL // 2) // HG_HEADS
HG_W = HG_HEADS * HG_DK
HG_VW = HG_HEADS * HG_DV
CHUNK = 16
D_FF = ((8 * D_MODEL // 3 + 127) // 128) * 128
ALPHA = (2.0 * DEPTH) ** 0.25
BETA = (8.0 * DEPTH) ** -0.25
LN_EPS = 1e-5
RMS_EPS = 1e-6
IN_SIZES = (D_RNN, D_RNN, HG_W, HG_W, HG_VW, HG_VW, D_MODEL, D_MODEL)
IN_COLS = sum(IN_SIZES)
IN_SPLITS = tuple(int(s) for s in np.cumsum(IN_SIZES)[:-1])

kernel_name = "hawk_hgrn2_macaron_deepnorm_step"


def layer_norm(x, g, b):
    xf = x.astype(jnp.float32)
    mu = jnp.mean(xf, axis=-1, keepdims=True)
    var = jnp.mean(jnp.square(xf - mu), axis=-1, keepdims=True)
    y = (xf - mu) * lax.rsqrt(var + LN_EPS)
    return (y * g.astype(jnp.float32) + b.astype(jnp.float32)).astype(x.dtype)


def swiglu(x, w_in, w_out):
    gate, up = jnp.split(x @ w_in, 2, axis=-1)
    return (jax.nn.silu(gate) * up) @ w_out


def causal_conv(x, buf, w, b):
    L = x.shape[1]
    xp = jnp.concatenate([buf.astype(x.dtype), x], axis=1)
    y = b + sum(xp[:, j:j + L] * w[j] for j in range(CONV_W))
    return y, xp[:, -(CONV_W - 1):]


def rglru(xc, wa, ba, wi, bi, lam, h0):
    B, L, _ = xc.shape
    xb = xc.reshape(B, L, RG_BLOCKS, RG_BS)
    r = jax.nn.sigmoid(jnp.einsum('blnk,nkj->blnj', xb, wa).reshape(B, L, D_RNN) + ba)
    i = jax.nn.sigmoid(jnp.einsum('blnk,nkj->blnj', xb, wi).reshape(B, L, D_RNN) + bi)
    log_a = -C_RG * jax.nn.softplus(-lam) * r
    a = jnp.exp(log_a)
    bterm = jnp.sqrt(-jnp.expm1(2.0 * log_a)) * (i * xc)
    bterm = bterm.at[:, 0].add(a[:, 0] * h0.astype(xc.dtype))

    def combine(left, right):
        a1, b1 = left
        a2, b2 = right
        return a1 * a2, a2 * b1 + b2

    _, h = lax.associative_scan(combine, (a, bterm), axis=1)
    return h, h[:, -1]


def hgrn2_chunked(q, k, v, logf, S0):
    B, L, H, _ = q.shape
    n = -(-L // CHUNK)
    pad = n * CHUNK - L
    padf = lambda t: jnp.pad(t, ((0, 0), (0, pad), (0, 0), (0, 0)))
    to_chunks = lambda t: padf(t).reshape(B, n, CHUNK, H, t.shape[-1]).transpose(1, 0, 3, 2, 4)
    mask = jnp.tril(jnp.ones((CHUNK, CHUNK), dtype=bool))[:, :, None]

    def step(S, inp):
        qc, kc, vc, gc = inp
        bcum = jnp.cumsum(gc, axis=2)
        inter = jnp.einsum('bhtd,bhdv->bhtv', qc * jnp.exp(bcum), S)
        diff = bcum[:, :, :, None, :] - bcum[:, :, None, :, :]
        decay = jnp.exp(jnp.where(mask, diff, -jnp.inf))
        A = jnp.einsum('bhtd,bhsd,bhtsd->bhts', qc, kc, decay)
        o = inter + jnp.einsum('bhts,bhsv->bhtv', A, vc)
        b_last = bcum[:, :, -1:, :]
        S_new = (jnp.exp(b_last[:, :, 0, :])[..., None] * S
                 + jnp.einsum('bhsd,bhsv->bhdv', kc * jnp.exp(b_last - bcum), vc))
        return S_new.astype(S.dtype), o

    S_fin, o = lax.scan(step, S0, (to_chunks(q), to_chunks(k), to_chunks(v), to_chunks(logf)))
    o = o.transpose(1, 0, 3, 2, 4).reshape(B, n * CHUNK, H, v.shape[-1])[:, :L]
    return o, S_fin


def mixer(x, h0, conv0, S0, lb, w_in, conv_w, conv_b, rg_wa, rg_ba, rg_wi, rg_bi,
          rg_lambda, hg_norm_g, w_pa, w_pb, w_o):
    B, L, _ = x.shape
    u = x @ w_in
    xr, gr, q, fz, iv, og, gA, gB = jnp.split(u, IN_SPLITS, axis=-1)
    xc, conv_new = causal_conv(xr, conv0, conv_w, conv_b)
    h, h_last = rglru(xc, rg_wa, rg_ba, rg_wi, rg_bi, rg_lambda, h0)
    yA = h * jax.nn.gelu(gr)
    f = lb + (1.0 - lb) * jax.nn.sigmoid(fz)
    logf = jnp.log(f)
    k = 1.0 - f
    shp = lambda t, d: t.reshape(B, L, HG_HEADS, d)
    o, S_new = hgrn2_chunked(shp(q, HG_DK), shp(k, HG_DK), shp(iv, HG_DV), shp(logf, HG_DK), S0)
    of = o.astype(jnp.float32)
    of = of * lax.rsqrt(jnp.mean(jnp.square(of), axis=-1, keepdims=True) + RMS_EPS)
    yB = of.reshape(B, L, HG_VW).astype(x.dtype) * hg_norm_g * jax.nn.silu(og)
    merged = jax.nn.sigmoid(gA) * (yA @ w_pa) + jax.nn.sigmoid(gB) * (yB @ w_pb)
    return merged @ w_o, h_last, conv_new, S_new


def trunk(x, h0s, conv0s, S0s, lbs, ln_g, ln_b, ffn_w_in, ffn_w_out, w_in, conv_w, conv_b,
          rg_wa, rg_ba, rg_wi, rg_bi, rg_lambda, hg_norm_g, w_pa, w_pb, w_o):
    hs, convs, Ss = [], [], []
    for l in range(DEPTH):
        x = layer_norm(ALPHA * x + 0.5 * swiglu(x, ffn_w_in[l, 0], ffn_w_out[l, 0]), ln_g[l, 0], ln_b[l, 0])
        m, h, c, S = mixer(x, h0s[l], conv0s[l], S0s[l], lbs[l], w_in[l], conv_w[l], conv_b[l],
                           rg_wa[l], rg_ba[l], rg_wi[l], rg_bi[l], rg_lambda[l], hg_norm_g[l],
                           w_pa[l], w_pb[l], w_o[l])
        x = layer_norm(ALPHA * x + m, ln_g[l, 1], ln_b[l, 1])
        x = layer_norm(ALPHA * x + 0.5 * swiglu(x, ffn_w_in[l, 1], ffn_w_out[l, 1]), ln_g[l, 2], ln_b[l, 2])
        hs.append(h); convs.append(c); Ss.append(S)
    return x, jnp.stack(hs), jnp.stack(convs), jnp.stack(Ss)


def setup_inputs(seed: int = 0) -> dict:
    key = jax.random.key(seed)
    ks = jax.random.split(key, 24)
    nrm = lambda k, shape, s: jax.random.normal(k, shape, jnp.float32) * s
    a_init = jax.random.uniform(ks[16], (DEPTH, D_RNN), jnp.float32, 0.9, 0.999)
    return {
        "x_prompt": nrm(ks[0], (BATCH, SEQ, D_MODEL), 1.0),
        "x_sample": nrm(ks[1], (DEC_BATCH, DEC_SEQ, D_MODEL), 1.0),
        "state_rglru_h": nrm(ks[2], (DEPTH, DEC_BATCH, D_RNN), 0.5),
        "state_rglru_conv": nrm(ks[3], (DEPTH, DEC_BATCH, CONV_W - 1, D_RNN), 1.0),
        "state_hgrn": nrm(ks[4], (DEPTH, DEC_BATCH, HG_HEADS, HG_DK, HG_DV), 0.3),
        "meta_tokens": nrm(ks[5], (N_META, D_MODEL), 1.0),
        "ln_g": 1.0 + nrm(ks[6], (DEPTH, 3, D_MODEL), 0.02),
        "ln_b": nrm(ks[7], (DEPTH, 3, D_MODEL), 0.02),
        "ffn_w_in": nrm(ks[8], (DEPTH, 2, D_MODEL, 2 * D_FF), BETA * D_MODEL ** -0.5),
        "ffn_w_out": nrm(ks[9], (DEPTH, 2, D_FF, D_MODEL), BETA * D_FF ** -0.5),
        "w_in": nrm(ks[10], (DEPTH, D_MODEL, IN_COLS), D_MODEL ** -0.5),
        "conv_w": nrm(ks[11], (DEPTH, CONV_W, D_RNN), CONV_W ** -0.5),
        "conv_b": nrm(ks[12], (DEPTH, D_RNN), 0.02),
        "rg_wa": nrm(ks[13], (DEPTH, RG_BLOCKS, RG_BS, RG_BS), RG_BS ** -0.5),
        "rg_ba": nrm(ks[14], (DEPTH, D_RNN), 0.02),
        "rg_wi": nrm(ks[15], (DEPTH, RG_BLOCKS, RG_BS, RG_BS), RG_BS ** -0.5),
        "rg_bi": nrm(ks[17], (DEPTH, D_RNN), 0.02),
        "rg_lambda": jnp.log(a_init) - jnp.log1p(-a_init),
        "hg_lb": nrm(ks[18], (DEPTH, HG_W), 1.0),
        "hg_norm_g": 1.0 + nrm(ks[19], (DEPTH, HG_VW), 0.02),
        "w_pa": nrm(ks[20], (DEPTH, D_RNN, D_MODEL), BETA * D_RNN ** -0.5),
        "w_pb": nrm(ks[21], (DEPTH, HG_VW, D_MODEL), BETA * HG_VW ** -0.5),
        "w_o": nrm(ks[22], (DEPTH, D_MODEL, D_MODEL), BETA * D_MODEL ** -0.5),
    }


def reference(x_prompt, x_sample, state_rglru_h, state_rglru_conv, state_hgrn, meta_tokens,
              ln_g, ln_b, ffn_w_in, ffn_w_out, w_in, conv_w, conv_b, rg_wa, rg_ba, rg_wi, rg_bi,
              rg_lambda, hg_lb, hg_norm_g, w_pa, w_pb, w_o):
    sm = jax.nn.softmax(hg_lb.astype(jnp.float32), axis=0)
    lbs = (jnp.cumsum(sm, axis=0) - sm[0]).astype(x_prompt.dtype)
    weights = (ln_g, ln_b, ffn_w_in, ffn_w_out, w_in, conv_w, conv_b, rg_wa, rg_ba, rg_wi,
               rg_bi, rg_lambda, hg_norm_g, w_pa, w_pb, w_o)
    dt = x_prompt.dtype
    meta = jnp.broadcast_to(meta_tokens.astype(dt)[None], (BATCH, N_META, D_MODEL))
    xp = jnp.concatenate([meta, x_prompt], axis=1)
    h0 = jnp.zeros((DEPTH, BATCH, D_RNN), dt)
    c0 = jnp.zeros((DEPTH, BATCH, CONV_W - 1, D_RNN), dt)
    S0 = jnp.zeros((DEPTH, BATCH, HG_HEADS, HG_DK, HG_DV), dt)
    yp, p_h, p_conv, p_S = trunk(xp, h0, c0, S0, lbs, *weights)
    y_prompt = yp[:, N_META:]
    y_sample, s_h, s_conv, s_S = trunk(x_sample, state_rglru_h, state_rglru_conv, state_hgrn,
                                       lbs, *weights)
    return (y_prompt, y_sample, p_h, p_conv, p_S, s_h, s_conv, s_S)
```

```python
import functools

import jax
import jax.numpy as jnp
from jax import lax
from jax.experimental import pallas as pl
from jax.experimental.pallas import tpu as pltpu

F32 = jnp.float32
BF16 = jnp.bfloat16

CONV_W = 4
C_RG = 8.0
RG_BLOCKS = 8
HG_HEADS = 8
CHUNK = 16
LN_EPS = 1e-5
RMS_EPS = 1e-6

LANES = 128
SUBLANES = 8
VMEM_LIMIT_BYTES = 56 * 1024 * 1024

TM_FFN = 464
TF_FFN = 512
TM_PROJ = 464
TN_PROJ = 1024
TM_MERGE = 232
PROMPT_TILE_ROWS = 688
SAMPLE_SEQS_PER_STEP = 8

_NT = (((1,), (1,)), ((), ()))
_TN = (((0,), (0,)), ((), ()))


def _params(semantics):
    return pltpu.CompilerParams(dimension_semantics=semantics, vmem_limit_bytes=VMEM_LIMIT_BYTES)


def _layer_norm(y, g, b):
    mu = jnp.mean(y, axis=-1, keepdims=True)
    d = y - mu
    var = jnp.mean(d * d, axis=-1, keepdims=True)
    return d * lax.rsqrt(var + LN_EPS) * g + b


def _ffn_body(x_ref, wi_ref, wo_ref, g_ref, b_ref, o_ref, xb_ref, acc_ref, *, tf, n_f, alpha):
    j = pl.program_id(1)

    @pl.when(j == 0)
    def _():
        xb_ref[...] = x_ref[...].astype(BF16)
        acc_ref[...] = jnp.zeros_like(acc_ref)

    gu = jnp.dot(xb_ref[...], wi_ref[...], preferred_element_type=F32)
    gate = gu[:, :tf]
    up = gu[:, tf:]
    h = (gate * jax.nn.sigmoid(gate) * up).astype(BF16)
    acc_ref[...] += jnp.dot(h, wo_ref[...], preferred_element_type=F32)

    @pl.when(j == n_f - 1)
    def _():
        y = alpha * x_ref[...] + 0.5 * acc_ref[...]
        o_ref[...] = _layer_norm(y, g_ref[...], b_ref[...])


def _ffn(x, wi_p, wo_p, ln_g, ln_b, l, k, ln_idx, alpha):
    m, d = x.shape
    tf = TF_FFN
    n_f = wo_p.shape[2] // tf
    tm = TM_FFN
    assert m % tm == 0
    body = functools.partial(_ffn_body, tf=tf, n_f=n_f, alpha=alpha)
    return pl.pallas_call(
        body,
        grid=(m // tm, n_f),
        in_specs=[
            pl.BlockSpec((tm, d), lambda i, j: (i, 0)),
            pl.BlockSpec((None, None, d, 2 * tf), lambda i, j: (l, k, 0, j)),
            pl.BlockSpec((None, None, tf, d), lambda i, j: (l, k, j, 0)),
            pl.BlockSpec((None, 1, d), lambda i, j: (ln_idx, 0, 0)),
            pl.BlockSpec((None, 1, d), lambda i, j: (ln_idx, 0, 0)),
        ],
        out_specs=pl.BlockSpec((tm, d), lambda i, j: (i, 0)),
        out_shape=jax.ShapeDtypeStruct((m, d), F32),
        scratch_shapes=[pltpu.VMEM((tm, d), BF16), pltpu.VMEM((tm, d), F32)],
        compiler_params=_params(("parallel", "arbitrary")),
        name=f"ffn_l{l}_{k}",
    )(x, wi_p, wo_p, ln_g, ln_b)


def _proj_body(x_ref, w_ref, o_ref, xb_ref):
    @pl.when(pl.program_id(1) == 0)
    def _():
        xb_ref[...] = x_ref[...].astype(BF16)

    o_ref[...] = jnp.dot(xb_ref[...], w_ref[...], preferred_element_type=F32)


def _proj(x, w_b, l):
    m, d = x.shape
    n = w_b.shape[-1]
    tm, tn = TM_PROJ, TN_PROJ
    assert m % tm == 0 and n % tn == 0
    return pl.pallas_call(
        _proj_body,
        grid=(m // tm, n // tn),
        in_specs=[
            pl.BlockSpec((tm, d), lambda i, j: (i, 0)),
            pl.BlockSpec((None, d, tn), lambda i, j: (l, 0, j)),
        ],
        out_specs=pl.BlockSpec((tm, tn), lambda i, j: (i, j)),
        out_shape=jax.ShapeDtypeStruct((m, n), F32),
        scratch_shapes=[pltpu.VMEM((tm, d), BF16)],
        compiler_params=_params(("parallel", "arbitrary")),
        name=f"proj_l{l}",
    )(x, w_b)


def _rglru_body(xr_ref, gr_ref, c0_ref, h0_ref, cw_ref, cb_ref, wa_ref, ba_ref, wi_ref, bi_ref, lam_ref,
                ya_ref, hl_ref, cn_ref, xbuf, a_scr, b_scr, hcar, *, n_seq, t_rows, n_t):
    t = pl.program_id(1)
    c = a_scr.shape[1]
    hist = CONV_W - 1
    base = SUBLANES - hist

    @pl.when(t == 0)
    def _():
        for g in range(n_seq):
            xbuf[g, base:SUBLANES, :] = c0_ref[g]
            hcar[g] = h0_ref[g]

    for g in range(n_seq):
        xbuf[g, SUBLANES:SUBLANES + t_rows, :] = xr_ref[g * t_rows:(g + 1) * t_rows, :]

    log_a_scale = -C_RG * jax.nn.softplus(-lam_ref[...])
    bs = c // RG_BLOCKS
    for n in range(RG_BLOCKS):
        ls = slice(n * bs, (n + 1) * bs)
        for g in range(n_seq):
            xc = cb_ref[:, ls] + sum(
                xbuf[g, base + j:base + j + t_rows, ls] * cw_ref[j:j + 1, ls] for j in range(CONV_W))
            a_scr[g * t_rows:(g + 1) * t_rows, ls] = xc
        xc = a_scr[:, ls]
        xcb = xc.astype(BF16)
        r = jax.nn.sigmoid(jnp.dot(xcb, wa_ref[n], preferred_element_type=F32) + ba_ref[:, ls])
        i = jax.nn.sigmoid(jnp.dot(xcb, wi_ref[n], preferred_element_type=F32) + bi_ref[:, ls])
        log_a = log_a_scale[:, ls] * r
        a = jnp.exp(log_a)
        mult = jnp.sqrt(-jnp.tanh(log_a) * (a * a + 1.0))
        a_scr[:, ls] = a
        b_scr[:, ls] = mult * (i * xc)

    for g in range(n_seq):
        xbuf[g, base:SUBLANES, :] = xbuf[g, base + t_rows:SUBLANES + t_rows, :]

    row = lax.broadcasted_iota(jnp.int32, (SUBLANES, c), 0)
    for g in range(n_seq):
        def group(j, h_in, g=g):
            r0 = pl.multiple_of(g * t_rows + j * SUBLANES, SUBLANES)
            a = a_scr[pl.ds(r0, SUBLANES), :]
            b = b_scr[pl.ds(r0, SUBLANES), :]
            for s in (1, 2, 4):
                keep = row >= s
                b = jnp.where(keep, a * pltpu.roll(b, s, 0) + b, b)
                a = jnp.where(keep, a * pltpu.roll(a, s, 0), a)
            h = a * h_in + b
            a_scr[pl.ds(r0, SUBLANES), :] = h
            return h[SUBLANES - 1:SUBLANES, :]

        hcar[g] = lax.fori_loop(0, t_rows // SUBLANES, group, hcar[g])

    for n in range(RG_BLOCKS):
        ls = slice(n * bs, (n + 1) * bs)
        ya_ref[:, ls] = a_scr[:, ls] * jax.nn.gelu(gr_ref[:, ls])

    @pl.when(t == n_t - 1)
    def _():
        for g in range(n_seq):
            hl_ref[g] = hcar[g]
            cn_ref[g] = xbuf[g, base:SUBLANES, :]


def _rglru(u, conv0, h0, conv_w, conv_b, wa_b, ba, wi_b, bi, lam, l, *, row0, n_batch, seq_len, n_seq, t_rows, tag):
    c = conv_w.shape[-1]
    n_t = seq_len // t_rows
    r = n_seq * t_rows
    assert seq_len % t_rows == 0 and n_batch % n_seq == 0 and row0 % r == 0 and t_rows % SUBLANES == 0
    assert n_seq == 1 or n_t == 1
    blk0 = row0 // r
    hist = CONV_W - 1
    body = functools.partial(_rglru_body, n_seq=n_seq, t_rows=t_rows, n_t=n_t)
    vec = pl.BlockSpec((None, 1, c), lambda s, t: (l, 0, 0))
    return pl.pallas_call(
        body,
        grid=(n_batch // n_seq, n_t),
        in_specs=[
            pl.BlockSpec((r, c), lambda s, t: (blk0 + s * n_t + t, 0)),
            pl.BlockSpec((r, c), lambda s, t: (blk0 + s * n_t + t, 1)),
            pl.BlockSpec((n_seq, hist, c), lambda s, t: (s, 0, 0)),
            pl.BlockSpec((n_seq, 1, c), lambda s, t: (s, 0, 0)),
            pl.BlockSpec((None, CONV_W, c), lambda s, t: (l, 0, 0)),
            vec,
            pl.BlockSpec((None, RG_BLOCKS, c // RG_BLOCKS, c // RG_BLOCKS), lambda s, t: (l, 0, 0, 0)),
            vec,
            pl.BlockSpec((None, RG_BLOCKS, c // RG_BLOCKS, c // RG_BLOCKS), lambda s, t: (l, 0, 0, 0)),
            vec,
            vec,
        ],
        out_specs=[
            pl.BlockSpec((r, c), lambda s, t: (s * n_t + t, 0)),
            pl.BlockSpec((n_seq, 1, c), lambda s, t: (s, 0, 0)),
            pl.BlockSpec((n_seq, hist, c), lambda s, t: (s, 0, 0)),
        ],
        out_shape=[
            jax.ShapeDtypeStruct((n_batch * seq_len, c), F32),
            jax.ShapeDtypeStruct((n_batch, 1, c), F32),
            jax.ShapeDtypeStruct((n_batch, hist, c), F32),
        ],
        scratch_shapes=[
            pltpu.VMEM((n_seq, SUBLANES + t_rows, c), F32),
            pltpu.VMEM((r, c), F32),
            pltpu.VMEM((r, c), F32),
            pltpu.VMEM((n_seq, 1, c), F32),
        ],
        compiler_params=_params(("parallel", "arbitrary")),
        name=f"rglru_{tag}_l{l}",
    )(u, u, conv0, h0, conv_w, conv_b, wa_b, ba, wi_b, bi, lam)


def _hgrn_body(q_ref, fz_ref, v_ref, og_ref, s0_ref, lb_ref, gn_ref, yb_ref, so_ref, st_scr,
               *, n_seq, t_rows, n_t, chunk, layer):
    t = pl.program_id(1)
    c = q_ref.shape[1]
    dk = c // HG_HEADS

    @pl.when(t == 0)
    def _():
        for g in range(n_seq):
            for h in range(HG_HEADS):
                st_scr[g, h] = s0_ref[g, h].T

    hl = lb_ref[...]
    e = jnp.exp(hl - jnp.max(hl, axis=0, keepdims=True))
    sm = e / jnp.sum(e, axis=0, keepdims=True)
    cum = sm[0:1, :]
    for m in range(1, layer + 1):
        cum = cum + sm[m:m + 1, :]
    lb = cum - sm[0:1, :]
    gn = gn_ref[...]

    row_c = lax.broadcasted_iota(jnp.int32, (chunk, c), 0)
    row_h = lax.broadcasted_iota(jnp.int32, (chunk, dk), 0)
    shifts = [s for s in (1, 2, 4, 8, 16) if s < chunk]

    for g in range(n_seq):
        def step(ci, carry, g=g):
            r0 = pl.multiple_of(g * t_rows + ci * chunk, chunk)
            rows = pl.ds(r0, chunk)
            f = lb + (1.0 - lb) * jax.nn.sigmoid(fz_ref[rows, :])
            k = 1.0 - f
            b = jnp.log(f)
            for s in shifts:
                b = b + jnp.where(row_c >= s, pltpu.roll(b, s, 0), 0.0)
            q = q_ref[rows, :]
            v = v_ref[rows, :]
            og = og_ref[rows, :]
            b_last = b[chunk - 1:chunk, :]
            qe = (q * jnp.exp(b)).astype(BF16)
            kd = (k * jnp.exp(b_last - b)).astype(BF16)
            e_last = jnp.exp(b_last)
            vb = v.astype(BF16)
            outs = []
            for h in range(HG_HEADS):
                ls = slice(h * dk, (h + 1) * dk)
                st = st_scr[g, h]
                inter = lax.dot_general(qe[:, ls], st.astype(BF16), _NT, preferred_element_type=F32)
                bc, qc, kc, vc = b[:, ls], q[:, ls], k[:, ls], v[:, ls]
                intra = jnp.zeros((chunk, dk), F32)
                for s in range(chunk):
                    dec = jnp.exp(jnp.where(row_h >= s, bc - bc[s:s + 1, :], -1e30))
                    a_col = jnp.sum(qc * kc[s:s + 1, :] * dec, axis=-1, keepdims=True)
                    intra = intra + a_col * vc[s:s + 1, :]
                o = inter + intra
                outs.append(o * lax.rsqrt(jnp.mean(o * o, axis=-1, keepdims=True) + RMS_EPS))
                st_scr[g, h] = st * e_last[:, ls] + lax.dot_general(
                    vb[:, ls], kd[:, ls], _TN, preferred_element_type=F32)
            on = jnp.concatenate(outs, axis=-1)
            yb_ref[rows, :] = on * gn * (og * jax.nn.sigmoid(og))
            return carry

        lax.fori_loop(0, t_rows // chunk, step, 0)

    @pl.when(t == n_t - 1)
    def _():
        for g in range(n_seq):
            for h in range(HG_HEADS):
                so_ref[g, h] = st_scr[g, h].T


def _hgrn(u, s0, hg_lb, gn, l, *, row0, n_batch, seq_len, n_seq, t_rows, chunk, tag):
    c = gn.shape[-1]
    dk = c // HG_HEADS
    n_t = seq_len // t_rows
    r = n_seq * t_rows
    assert seq_len % t_rows == 0 and n_batch % n_seq == 0 and row0 % r == 0 and t_rows % chunk == 0
    assert n_seq == 1 or n_t == 1
    blk0 = row0 // r
    body = functools.partial(_hgrn_body, n_seq=n_seq, t_rows=t_rows, n_t=n_t, chunk=chunk, layer=l)

    def col(cb):
        return pl.BlockSpec((r, c), lambda s, t: (blk0 + s * n_t + t, cb))

    return pl.pallas_call(
        body,
        grid=(n_batch // n_seq, n_t),
        in_specs=[
            col(2), col(3), col(4), col(5),
            pl.BlockSpec((n_seq, HG_HEADS, dk, dk), lambda s, t: (s, 0, 0, 0)),
            pl.BlockSpec(hg_lb.shape, lambda s, t: (0, 0)),
            pl.BlockSpec((None, 1, c), lambda s, t: (l, 0, 0)),
        ],
        out_specs=[
            pl.BlockSpec((r, c), lambda s, t: (s * n_t + t, 0)),
            pl.BlockSpec((n_seq, HG_HEADS, dk, dk), lambda s, t: (s, 0, 0, 0)),
        ],
        out_shape=[
            jax.ShapeDtypeStruct((n_batch * seq_len, c), F32),
            jax.ShapeDtypeStruct((n_batch, HG_HEADS, dk, dk), F32),
        ],
        scratch_shapes=[pltpu.VMEM((n_seq, HG_HEADS, dk, dk), F32)],
        compiler_params=_params(("parallel", "arbitrary")),
        name=f"hgrn_{tag}_l{l}",
    )(u, u, u, u, s0, hg_lb, gn)


def _merge_body(x_ref, ya_ref, yb_ref, ga_ref, gb_ref, wpa_ref, wpb_ref, wo_ref, g_ref, b_ref, o_ref, *, alpha):
    pa = jnp.dot(ya_ref[...].astype(BF16), wpa_ref[...], preferred_element_type=F32)
    pb = jnp.dot(yb_ref[...].astype(BF16), wpb_ref[...], preferred_element_type=F32)
    merged = jax.nn.sigmoid(ga_ref[...]) * pa + jax.nn.sigmoid(gb_ref[...]) * pb
    m = jnp.dot(merged.astype(BF16), wo_ref[...], preferred_element_type=F32)
    o_ref[...] = _layer_norm(alpha * x_ref[...] + m, g_ref[...], b_ref[...])


def _merge(x, ya, yb, u, wpa_b, wpb_b, wo_b, ln_g, ln_b, l, ln_idx, alpha):
    m, d = x.shape
    c = ya.shape[1]
    tm = TM_MERGE
    assert m % tm == 0
    gate_blk0 = (u.shape[1] - 2 * d) // d
    body = functools.partial(_merge_body, alpha=alpha)
    return pl.pallas_call(
        body,
        grid=(m // tm,),
        in_specs=[
            pl.BlockSpec((tm, d), lambda i: (i, 0)),
            pl.BlockSpec((tm, c), lambda i: (i, 0)),
            pl.BlockSpec((tm, c), lambda i: (i, 0)),
            pl.BlockSpec((tm, d), lambda i: (i, gate_blk0)),
            pl.BlockSpec((tm, d), lambda i: (i, gate_blk0 + 1)),
            pl.BlockSpec((None, c, d), lambda i: (l, 0, 0)),
            pl.BlockSpec((None, c, d), lambda i: (l, 0, 0)),
            pl.BlockSpec((None, d, d), lambda i: (l, 0, 0)),
            pl.BlockSpec((None, 1, d), lambda i: (ln_idx, 0, 0)),
            pl.BlockSpec((None, 1, d), lambda i: (ln_idx, 0, 0)),
        ],
        out_specs=pl.BlockSpec((tm, d), lambda i: (i, 0)),
        out_shape=jax.ShapeDtypeStruct((m, d), F32),
        compiler_params=_params(("parallel",)),
        name=f"merge_l{l}",
    )(x, ya, yb, u, u, wpa_b, wpb_b, wo_b, ln_g, ln_b)


def _prep_ffn_weights(w_in, w_out, tf):
    d, f2 = w_in.shape[-2:]
    f = f2 // 2
    n_f = -(-f // tf)
    pad = n_f * tf - f
    lead = w_in.shape[:-2]
    wb = w_in.astype(BF16)
    cpad = [(0, 0)] * (wb.ndim - 1) + [(0, pad)]
    gate = jnp.pad(wb[..., :f], cpad).reshape(*lead, d, n_f, 1, tf)
    up = jnp.pad(wb[..., f:], cpad).reshape(*lead, d, n_f, 1, tf)
    wi_p = jnp.concatenate([gate, up], axis=-2).reshape(*lead, d, n_f * 2 * tf)
    rpad = [(0, 0)] * (w_out.ndim - 2) + [(0, pad), (0, 0)]
    wo_p = jnp.pad(w_out.astype(BF16), rpad)
    return wi_p, wo_p


def kernel(x_prompt, x_sample, state_rglru_h, state_rglru_conv, state_hgrn, meta_tokens, ln_g, ln_b, ffn_w_in, ffn_w_out, w_in, conv_w, conv_b, rg_wa, rg_ba, rg_wi, rg_bi, rg_lambda, hg_lb, hg_norm_g, w_pa, w_pb, w_o):
    batch, seq, d_model = x_prompt.shape
    dec_batch, dec_seq, _ = x_sample.shape
    depth = w_in.shape[0]
    n_meta = meta_tokens.shape[0]
    d_rnn = conv_w.shape[-1]
    dk = d_rnn // HG_HEADS
    alpha = (2.0 * depth) ** 0.25
    dt = x_prompt.dtype

    p_len = n_meta + seq
    p_rows = batch * p_len
    meta = jnp.broadcast_to(meta_tokens.astype(dt)[None], (batch, n_meta, d_model))
    x = jnp.concatenate(
        [jnp.concatenate([meta, x_prompt], axis=1).reshape(p_rows, d_model),
         x_sample.reshape(dec_batch * dec_seq, d_model)], axis=0)

    wi_p, wo_p = _prep_ffn_weights(ffn_w_in, ffn_w_out, TF_FFN)
    w_in_b = w_in.astype(BF16)
    wpa_b, wpb_b, wo_b = w_pa.astype(BF16), w_pb.astype(BF16), w_o.astype(BF16)
    wa_b, wi_b = rg_wa.astype(BF16), rg_wi.astype(BF16)
    ln_g3 = ln_g.reshape(depth * 3, 1, d_model)
    ln_b3 = ln_b.reshape(depth * 3, 1, d_model)
    vec = lambda a: a.reshape(depth, 1, a.shape[-1])
    conv_b3, ba3, bi3, lam3, gn3 = vec(conv_b), vec(rg_ba), vec(rg_bi), vec(rg_lambda), vec(hg_norm_g)

    zeros_h = jnp.zeros((batch, 1, d_rnn), dt)
    zeros_c = jnp.zeros((batch, CONV_W - 1, d_rnn), dt)
    zeros_s = jnp.zeros((batch, HG_HEADS, dk, dk), dt)

    prompt = dict(row0=0, n_batch=batch, seq_len=p_len, n_seq=1, t_rows=PROMPT_TILE_ROWS, tag="p")
    sample = dict(row0=p_rows, n_batch=dec_batch, seq_len=dec_seq, n_seq=SAMPLE_SEQS_PER_STEP, t_rows=dec_seq, tag="s")

    outs = {k: [] for k in ("ph", "pc", "ps", "sh", "sc", "ss")}
    for l in range(depth):
        x = _ffn(x, wi_p, wo_p, ln_g3, ln_b3, l, 0, 3 * l, alpha)
        u = _proj(x, w_in_b, l)
        rg = (conv_w, conv_b3, wa_b, ba3, wi_b, bi3, lam3, l)
        ya_p, ph, pc = _rglru(u, zeros_c, zeros_h, *rg, **prompt)
        ya_s, sh, sc = _rglru(u, state_rglru_conv[l], state_rglru_h[l][:, None, :], *rg, **sample)
        yb_p, ps = _hgrn(u, zeros_s, hg_lb, gn3, l, chunk=CHUNK, **prompt)
        yb_s, ss = _hgrn(u, state_hgrn[l], hg_lb, gn3, l, chunk=dec_seq, **sample)
        ya = jnp.concatenate([ya_p, ya_s], axis=0)
        yb = jnp.concatenate([yb_p, yb_s], axis=0)
        x = _merge(x, ya, yb, u, wpa_b, wpb_b, wo_b, ln_g3, ln_b3, l, 3 * l + 1, alpha)
        x = _ffn(x, wi_p, wo_p, ln_g3, ln_b3, l, 1, 3 * l + 2, alpha)
        for key, val in zip(("ph", "pc", "ps", "sh", "sc", "ss"), (ph[:, 0], pc, ps, sh[:, 0], sc, ss)):
            outs[key].append(val)

    y_prompt = x[:p_rows].reshape(batch, p_len, d_model)[:, n_meta:]
    y_sample = x[p_rows:].reshape(dec_batch, dec_seq, d_model)
    st = lambda key: jnp.stack(outs[key])
    return (y_prompt, y_sample, st("ph"), st("pc"), st("ps"), st("sh"), st("sc"), st("ss"))
```

```python
import functools

import jax
import jax.numpy as jnp
from jax import lax
from jax.experimental import pallas as pl
from jax.experimental.pallas import tpu as pltpu

F32 = jnp.float32
BF16 = jnp.bfloat16

CONV_W = 4
C_RG = 8.0
RG_BLOCKS = 8
HG_HEADS = 8
CHUNK = 16
LN_EPS = 1e-5
RMS_EPS = 1e-6

LANES = 128
SUBLANES = 8
MXU_DIM = 256
VMEM_LIMIT_BYTES = 56 * 1024 * 1024

TM_FFN = 928
TF_FFN = 256
LN_ROWS = 232
TM_PROJ = 928
TN_PROJ = 1024
TM_MERGE = 232
PROMPT_TILE_ROWS = 688
SAMPLE_SEQS_PER_STEP = 8

_NT = (((1,), (1,)), ((), ()))
_TN = (((0,), (0,)), ((), ()))
_MASKED = -1e30


def _params(semantics):
    return pltpu.CompilerParams(dimension_semantics=semantics, vmem_limit_bytes=VMEM_LIMIT_BYTES)


def _layer_norm(y, g, b):
    mu = jnp.mean(y, axis=-1, keepdims=True)
    d = y - mu
    var = jnp.mean(d * d, axis=-1, keepdims=True)
    return d * lax.rsqrt(var + LN_EPS) * g + b


_ANY = pl.BlockSpec(memory_space=pl.ANY)


def _ffn_body(x_ref, wg_ref, wu_ref, wo_ref, g_ref, b_ref, o_ref, xb_ref, *, tf, n_f, f_dim, alpha):
    j = pl.program_id(1)

    @pl.when(j == 0)
    def _():
        xb_ref[...] = x_ref[...].astype(BF16)
        o_ref[...] = alpha * x_ref[...]

    xb = xb_ref[...]
    gate = jnp.dot(xb, wg_ref[...].astype(BF16), preferred_element_type=F32)
    up = jnp.dot(xb, wu_ref[...].astype(BF16), preferred_element_type=F32)
    h = (0.5 * gate) * jax.nn.sigmoid(gate) * up
    overlap = n_f * tf - f_dim
    if overlap:
        col = lax.broadcasted_iota(jnp.int32, (1, tf), 1)
        h = jnp.where((j < n_f - 1) | (col >= overlap), h, 0.0)
    o_ref[...] += jnp.dot(h.astype(BF16), wo_ref[...].astype(BF16), preferred_element_type=F32)

    @pl.when(j == n_f - 1)
    def _():
        def rows(ci, carry):
            r = pl.ds(pl.multiple_of(ci * LN_ROWS, SUBLANES), LN_ROWS)
            o_ref[r, :] = _layer_norm(o_ref[r, :], g_ref[...], b_ref[...])
            return carry

        lax.fori_loop(0, o_ref.shape[0] // LN_ROWS, rows, 0)


def _ffn(x, w_in2, w_out2, ln_g, ln_b, wsel, ln_idx, alpha):
    m, d = x.shape
    f_dim = w_in2.shape[1] // 2
    tf, tm = TF_FFN, TM_FFN
    n_f = -(-f_dim // tf)
    assert m % tm == 0 and tm % LN_ROWS == 0 and f_dim % LANES == 0 and tf % LANES == 0 and f_dim >= tf
    assert (n_f * tf - f_dim) < tf and d % SUBLANES == 0 and f_dim % SUBLANES == 0
    last = (f_dim - tf) // LANES
    step = tf // LANES

    def chunk(j):
        return jnp.minimum(j * step, last)

    body = functools.partial(_ffn_body, tf=tf, n_f=n_f, f_dim=f_dim, alpha=alpha)
    return pl.pallas_call(
        body,
        grid=(m // tm, n_f),
        in_specs=[
            pl.BlockSpec((tm, d), lambda i, j: (i, 0)),
            pl.BlockSpec((pl.Element(d), pl.Element(tf)), lambda i, j: (wsel * d, chunk(j) * LANES)),
            pl.BlockSpec((pl.Element(d), pl.Element(tf)),
                         lambda i, j: (wsel * d, (f_dim // LANES + chunk(j)) * LANES)),
            pl.BlockSpec((pl.Element(tf), pl.Element(d)),
                         lambda i, j: ((wsel * (f_dim // SUBLANES) + chunk(j) * (LANES // SUBLANES)) * SUBLANES, 0)),
            pl.BlockSpec((None, 1, d), lambda i, j: (ln_idx, 0, 0)),
            pl.BlockSpec((None, 1, d), lambda i, j: (ln_idx, 0, 0)),
        ],
        out_specs=pl.BlockSpec((tm, d), lambda i, j: (i, 0)),
        out_shape=jax.ShapeDtypeStruct((m, d), F32),
        scratch_shapes=[pltpu.VMEM((tm, d), BF16)],
        compiler_params=_params(("parallel", "arbitrary")),
        name=f"ffn_{wsel}",
    )(x, w_in2, w_in2, w_out2, ln_g, ln_b)


def _proj_body(x_ref, w_ref, o_ref, xb_ref):
    @pl.when(pl.program_id(1) == 0)
    def _():
        xb_ref[...] = x_ref[...].astype(BF16)

    o_ref[...] = jnp.dot(xb_ref[...], w_ref[...].astype(BF16), preferred_element_type=F32)


def _proj(x, w, l):
    m, d = x.shape
    n = w.shape[-1]
    tm, tn = TM_PROJ, TN_PROJ
    assert m % tm == 0 and n % tn == 0
    return pl.pallas_call(
        _proj_body,
        grid=(m // tm, n // tn),
        in_specs=[
            pl.BlockSpec((tm, d), lambda i, j: (i, 0)),
            pl.BlockSpec((None, d, tn), lambda i, j: (l, 0, j)),
        ],
        out_specs=pl.BlockSpec((tm, tn), lambda i, j: (i, j)),
        out_shape=jax.ShapeDtypeStruct((m, n), F32),
        scratch_shapes=[pltpu.VMEM((tm, d), BF16)],
        compiler_params=_params(("parallel", "arbitrary")),
        name=f"proj_l{l}",
    )(x, w)


def _rglru_body(*refs, n_seq, t_rows, n_t, n_alias):
    xr_ref, gr_ref, c0_ref, h0_ref, cw_ref, cb_ref, wa_ref, ba_ref, wi_ref, bi_ref, lam_ref = refs[:11]
    ya_ref, hl_ref, cn_ref, xbuf, a_scr, b_scr, hcar = refs[11 + n_alias:]
    t = pl.program_id(1)
    c = a_scr.shape[1]
    hist = CONV_W - 1
    base = SUBLANES - hist

    @pl.when(t == 0)
    def _():
        for g in range(n_seq):
            xbuf[g, base:SUBLANES, :] = c0_ref[g]
            hcar[g] = h0_ref[g]

    for g in range(n_seq):
        xbuf[g, SUBLANES:SUBLANES + t_rows, :] = xr_ref[g * t_rows:(g + 1) * t_rows, :]

    log_a_scale = -C_RG * jax.nn.softplus(-lam_ref[...])
    bs = c // RG_BLOCKS
    for n in range(RG_BLOCKS):
        ls = slice(n * bs, (n + 1) * bs)
        for g in range(n_seq):
            xc = cb_ref[:, ls] + sum(
                xbuf[g, base + j:base + j + t_rows, ls] * cw_ref[j:j + 1, ls] for j in range(CONV_W))
            a_scr[g * t_rows:(g + 1) * t_rows, ls] = xc
        xc = a_scr[:, ls]
        xcb = xc.astype(BF16)
        r = jax.nn.sigmoid(jnp.dot(xcb, wa_ref[n], preferred_element_type=F32) + ba_ref[:, ls])
        i = jax.nn.sigmoid(jnp.dot(xcb, wi_ref[n], preferred_element_type=F32) + bi_ref[:, ls])
        log_a = log_a_scale[:, ls] * r
        a = jnp.exp(log_a)
        mult = jnp.sqrt(-jnp.tanh(log_a) * (a * a + 1.0))
        a_scr[:, ls] = a
        b_scr[:, ls] = mult * (i * xc)

    for g in range(n_seq):
        xbuf[g, base:SUBLANES, :] = xbuf[g, base + t_rows:SUBLANES + t_rows, :]

    row = lax.broadcasted_iota(jnp.int32, (SUBLANES, c), 0)
    for g in range(n_seq):
        def group(j, h_in, g=g):
            r0 = pl.multiple_of(g * t_rows + j * SUBLANES, SUBLANES)
            a = a_scr[pl.ds(r0, SUBLANES), :]
            b = b_scr[pl.ds(r0, SUBLANES), :]
            for s in (1, 2, 4):
                keep = row >= s
                b = jnp.where(keep, a * pltpu.roll(b, s, 0) + b, b)
                a = jnp.where(keep, a * pltpu.roll(a, s, 0), a)
            h = a * h_in + b
            a_scr[pl.ds(r0, SUBLANES), :] = h
            return h[SUBLANES - 1:SUBLANES, :]

        hcar[g] = lax.fori_loop(0, t_rows // SUBLANES, group, hcar[g])

    for n in range(RG_BLOCKS):
        ls = slice(n * bs, (n + 1) * bs)
        ya_ref[:, ls] = a_scr[:, ls] * jax.nn.gelu(gr_ref[:, ls])

    @pl.when(t == n_t - 1)
    def _():
        for g in range(n_seq):
            hl_ref[g] = hcar[g]
            cn_ref[g] = xbuf[g, base:SUBLANES, :]


def _with_aliases(n_in, prev):
    specs, args, amap = [], [], {}
    for out_idx, p in enumerate(prev):
        if p is not None:
            amap[n_in + len(args)] = out_idx
            specs.append(_ANY)
            args.append(p)
    return specs, args, amap


def _rglru(u, conv0, h0, conv_w, conv_b, wa_b, ba, wi_b, bi, lam, l, prev, *,
           depth, total_rows, row0, n_batch, seq_len, n_seq, t_rows, tag):
    c = conv_w.shape[-1]
    n_t = seq_len // t_rows
    r = n_seq * t_rows
    assert seq_len % t_rows == 0 and n_batch % n_seq == 0 and row0 % r == 0 and t_rows % SUBLANES == 0
    assert n_seq == 1 or n_t == 1
    blk0 = row0 // r
    hist = CONV_W - 1
    a_specs, a_args, amap = _with_aliases(11, prev)
    body = functools.partial(_rglru_body, n_seq=n_seq, t_rows=t_rows, n_t=n_t, n_alias=len(a_args))
    vec = pl.BlockSpec((None, 1, c), lambda s, t: (l, 0, 0))
    gate_w = pl.BlockSpec((None, RG_BLOCKS, c // RG_BLOCKS, c // RG_BLOCKS), lambda s, t: (l, 0, 0, 0))
    return pl.pallas_call(
        body,
        grid=(n_batch // n_seq, n_t),
        in_specs=[
            pl.BlockSpec((r, c), lambda s, t: (blk0 + s * n_t + t, 0)),
            pl.BlockSpec((r, c), lambda s, t: (blk0 + s * n_t + t, 1)),
            pl.BlockSpec((n_seq, hist, c), lambda s, t: (s, 0, 0)),
            pl.BlockSpec((n_seq, 1, c), lambda s, t: (s, 0, 0)),
            pl.BlockSpec((None, CONV_W, c), lambda s, t: (l, 0, 0)),
            vec, gate_w, vec, gate_w, vec, vec,
        ] + a_specs,
        out_specs=[
            pl.BlockSpec((r, c), lambda s, t: (blk0 + s * n_t + t, 0)),
            pl.BlockSpec((None, n_seq, 1, c), lambda s, t: (l, s, 0, 0)),
            pl.BlockSpec((None, n_seq, hist, c), lambda s, t: (l, s, 0, 0)),
        ],
        out_shape=[
            jax.ShapeDtypeStruct((total_rows, c), F32),
            jax.ShapeDtypeStruct((depth, n_batch, 1, c), F32),
            jax.ShapeDtypeStruct((depth, n_batch, hist, c), F32),
        ],
        scratch_shapes=[
            pltpu.VMEM((n_seq, SUBLANES + t_rows, c), F32),
            pltpu.VMEM((r, c), F32),
            pltpu.VMEM((r, c), F32),
            pltpu.VMEM((n_seq, 1, c), F32),
        ],
        input_output_aliases=amap,
        compiler_params=_params(("arbitrary", "arbitrary")),
        name=f"rglru_{tag}_l{l}",
    )(u, u, conv0, h0, conv_w, conv_b, wa_b, ba, wi_b, bi, lam, *a_args)


def _hgrn_body(*refs, n_seq, t_rows, n_t, chunk, layer, n_alias):
    q_ref, fz_ref, v_ref, og_ref, s0_ref, lb_ref, gn_ref = refs[:7]
    yb_ref, so_ref, st_scr, p_scr, r_scr = refs[7 + n_alias:]
    t = pl.program_id(1)
    c = q_ref.shape[1]
    dk = c // HG_HEADS
    n_lane_blk = c // MXU_DIM
    n_rb = chunk // SUBLANES
    n_units = sum(n_rb - s // SUBLANES for s in range(chunk))
    assert n_units % 2 == 0
    n_tiles = n_units // 2

    @pl.when(t == 0)
    def _():
        for g in range(n_seq):
            for h in range(HG_HEADS):
                st_scr[g, h] = s0_ref[g, h].T

    hl = lb_ref[...]
    e = jnp.exp(hl - jnp.max(hl, axis=0, keepdims=True))
    sm = e / jnp.sum(e, axis=0, keepdims=True)
    cum = sm[0:1, :]
    for m in range(1, layer + 1):
        cum = cum + sm[m:m + 1, :]
    lb = cum - sm[0:1, :]
    gn = gn_ref[...]

    row_c = lax.broadcasted_iota(jnp.int32, (chunk, c), 0)
    row_8 = lax.broadcasted_iota(jnp.int32, (SUBLANES, c), 0)
    shifts = [s for s in (1, 2, 4, 8, 16) if s < chunk]
    ri = lax.broadcasted_iota(jnp.int32, (MXU_DIM, MXU_DIM), 0)
    ci_ = lax.broadcasted_iota(jnp.int32, (MXU_DIM, MXU_DIM), 1)
    head_ones = ((ri // dk) == (ci_ // dk)).astype(BF16)

    for g in range(n_seq):
        def step(ci, carry, g=g):
            r0 = pl.multiple_of(g * t_rows + ci * chunk, chunk)
            rows = pl.ds(r0, chunk)
            f = lb + (1.0 - lb) * jax.nn.sigmoid(fz_ref[rows, :])
            k = 1.0 - f
            b = jnp.log(f)
            for s in shifts:
                b = b + jnp.where(row_c >= s, pltpu.roll(b, s, 0), 0.0)
            q = q_ref[rows, :]
            v = v_ref[rows, :]

            bb = [b[i * SUBLANES:(i + 1) * SUBLANES, :] for i in range(n_rb)]
            qq = [q[i * SUBLANES:(i + 1) * SUBLANES, :] for i in range(n_rb)]
            units = [(s, tb) for s in range(chunk) for tb in range(s // SUBLANES, n_rb)]

            def pair_prod(s, tb):
                d = bb[tb] - b[s:s + 1, :]
                if tb == s // SUBLANES:
                    d = jnp.where(row_8 >= (s % SUBLANES), d, _MASKED)
                return qq[tb] * k[s:s + 1, :] * jnp.exp(d)

            for ti in range(n_tiles):
                pb = jnp.concatenate([pair_prod(*units[2 * ti]), pair_prod(*units[2 * ti + 1])], axis=0).astype(BF16)
                for j in range(n_lane_blk):
                    p_scr[(j * n_tiles + ti) * 16:(j * n_tiles + ti + 1) * 16, :] = pb[:, j * MXU_DIM:(j + 1) * MXU_DIM]
            r_scr[...] = jnp.dot(p_scr[...], head_ones, preferred_element_type=F32)
            intra = [jnp.zeros((SUBLANES, c), F32) for _ in range(n_rb)]
            for un, (s, tb) in enumerate(units):
                ti, hf = divmod(un, 2)
                a_col = jnp.concatenate(
                    [r_scr[(j * n_tiles + ti) * 16 + hf * SUBLANES:(j * n_tiles + ti) * 16 + (hf + 1) * SUBLANES, :]
                     for j in range(n_lane_blk)], axis=1)
                intra[tb] = intra[tb] + a_col * v[s:s + 1, :]
            intra = jnp.concatenate(intra, axis=0) if n_rb > 1 else intra[0]

            b_last = b[chunk - 1:chunk, :]
            qe = (q * jnp.exp(b)).astype(BF16)
            kd = (k * jnp.exp(b_last - b)).astype(BF16)
            e_last = jnp.exp(b_last)
            vb = v.astype(BF16)
            outs = []
            for h in range(HG_HEADS):
                ls = slice(h * dk, (h + 1) * dk)
                st = st_scr[g, h]
                o = intra[:, ls] + lax.dot_general(qe[:, ls], st.astype(BF16), _NT, preferred_element_type=F32)
                outs.append(o * lax.rsqrt(jnp.mean(o * o, axis=-1, keepdims=True) + RMS_EPS))
                st_scr[g, h] = st * e_last[:, ls] + lax.dot_general(
                    vb[:, ls], kd[:, ls], _TN, preferred_element_type=F32)
            og = og_ref[rows, :]
            yb_ref[rows, :] = jnp.concatenate(outs, axis=-1) * gn * (og * jax.nn.sigmoid(og))
            return carry

        lax.fori_loop(0, t_rows // chunk, step, 0)

    @pl.when(t == n_t - 1)
    def _():
        for g in range(n_seq):
            for h in range(HG_HEADS):
                so_ref[g, h] = st_scr[g, h].T


def _hgrn(u, s0, s0_layer, hg_lb, gn, l, prev, *, depth, total_rows, row0, n_batch, seq_len, n_seq, t_rows, chunk, tag):
    c = gn.shape[-1]
    dk = c // HG_HEADS
    n_t = seq_len // t_rows
    r = n_seq * t_rows
    assert seq_len % t_rows == 0 and n_batch % n_seq == 0 and row0 % r == 0 and t_rows % chunk == 0
    assert n_seq == 1 or n_t == 1
    assert chunk % SUBLANES == 0 and c % MXU_DIM == 0 and MXU_DIM % dk == 0
    blk0 = row0 // r
    n_tiles = sum(chunk // SUBLANES - s // SUBLANES for s in range(chunk)) // 2
    p_rows = (c // MXU_DIM) * n_tiles * 16
    a_specs, a_args, amap = _with_aliases(7, prev)
    body = functools.partial(_hgrn_body, n_seq=n_seq, t_rows=t_rows, n_t=n_t, chunk=chunk, layer=l,
                             n_alias=len(a_args))

    def col(cb):
        return pl.BlockSpec((r, c), lambda s, t: (blk0 + s * n_t + t, cb))

    return pl.pallas_call(
        body,
        grid=(n_batch // n_seq, n_t),
        in_specs=[
            col(2), col(3), col(4), col(5),
            pl.BlockSpec((None, n_seq, HG_HEADS, dk, dk), lambda s, t: (s0_layer, s, 0, 0, 0)),
            pl.BlockSpec(hg_lb.shape, lambda s, t: (0, 0)),
            pl.BlockSpec((None, 1, c), lambda s, t: (l, 0, 0)),
        ] + a_specs,
        out_specs=[
            pl.BlockSpec((r, c), lambda s, t: (blk0 + s * n_t + t, 0)),
            pl.BlockSpec((None, n_seq, HG_HEADS, dk, dk), lambda s, t: (l, s, 0, 0, 0)),
        ],
        out_shape=[
            jax.ShapeDtypeStruct((total_rows, c), F32),
            jax.ShapeDtypeStruct((depth, n_batch, HG_HEADS, dk, dk), F32),
        ],
        scratch_shapes=[
            pltpu.VMEM((n_seq, HG_HEADS, dk, dk), F32),
            pltpu.VMEM((p_rows, MXU_DIM), BF16),
            pltpu.VMEM((p_rows, MXU_DIM), F32),
        ],
        input_output_aliases=amap,
        compiler_params=_params(("arbitrary", "arbitrary")),
        name=f"hgrn_{tag}_l{l}",
    )(u, u, u, u, s0, hg_lb, gn, *a_args)


def _merge_body(x_ref, ya_ref, yb_ref, ga_ref, gb_ref, wpa_ref, wpb_ref, wo_ref, g_ref, b_ref, o_ref, *, alpha):
    pa = jnp.dot(ya_ref[...].astype(BF16), wpa_ref[...], preferred_element_type=F32)
    pb = jnp.dot(yb_ref[...].astype(BF16), wpb_ref[...], preferred_element_type=F32)
    merged = jax.nn.sigmoid(ga_ref[...]) * pa + jax.nn.sigmoid(gb_ref[...]) * pb
    m = jnp.dot(merged.astype(BF16), wo_ref[...], preferred_element_type=F32)
    o_ref[...] = _layer_norm(alpha * x_ref[...] + m, g_ref[...], b_ref[...])


def _merge(x, ya, yb, u, wpa_b, wpb_b, wo_b, ln_g, ln_b, l, ln_idx, alpha):
    m, d = x.shape
    c = ya.shape[1]
    tm = TM_MERGE
    assert m % tm == 0
    gate_blk0 = (u.shape[1] - 2 * d) // d
    body = functools.partial(_merge_body, alpha=alpha)
    return pl.pallas_call(
        body,
        grid=(m // tm,),
        in_specs=[
            pl.BlockSpec((tm, d), lambda i: (i, 0)),
            pl.BlockSpec((tm, c), lambda i: (i, 0)),
            pl.BlockSpec((tm, c), lambda i: (i, 0)),
            pl.BlockSpec((tm, d), lambda i: (i, gate_blk0)),
            pl.BlockSpec((tm, d), lambda i: (i, gate_blk0 + 1)),
            pl.BlockSpec((None, c, d), lambda i: (l, 0, 0)),
            pl.BlockSpec((None, c, d), lambda i: (l, 0, 0)),
            pl.BlockSpec((None, d, d), lambda i: (l, 0, 0)),
            pl.BlockSpec((None, 1, d), lambda i: (ln_idx, 0, 0)),
            pl.BlockSpec((None, 1, d), lambda i: (ln_idx, 0, 0)),
        ],
        out_specs=pl.BlockSpec((tm, d), lambda i: (i, 0)),
        out_shape=jax.ShapeDtypeStruct((m, d), F32),
        compiler_params=_params(("parallel",)),
        name=f"merge_l{l}",
    )(x, ya, yb, u, u, wpa_b, wpb_b, wo_b, ln_g, ln_b)


def kernel(x_prompt, x_sample, state_rglru_h, state_rglru_conv, state_hgrn, meta_tokens, ln_g, ln_b, ffn_w_in, ffn_w_out, w_in, conv_w, conv_b, rg_wa, rg_ba, rg_wi, rg_bi, rg_lambda, hg_lb, hg_norm_g, w_pa, w_pb, w_o):
    batch, seq, d_model = x_prompt.shape
    dec_batch, dec_seq, _ = x_sample.shape
    depth = w_in.shape[0]
    n_meta = meta_tokens.shape[0]
    d_rnn = conv_w.shape[-1]
    dk = d_rnn // HG_HEADS
    d_ff = ffn_w_out.shape[-2]
    alpha = (2.0 * depth) ** 0.25
    dt = x_prompt.dtype

    p_len = n_meta + seq
    p_rows = batch * p_len
    total_rows = p_rows + dec_batch * dec_seq
    meta = jnp.broadcast_to(meta_tokens.astype(dt)[None], (batch, n_meta, d_model))
    x = jnp.concatenate(
        [jnp.concatenate([meta, x_prompt], axis=1).reshape(p_rows, d_model),
         x_sample.reshape(dec_batch * dec_seq, d_model)], axis=0)

    w_in2 = ffn_w_in.reshape(depth * 2 * d_model, 2 * d_ff)
    w_out2 = ffn_w_out.reshape(depth * 2 * d_ff, d_model)
    wpa_b, wpb_b, wo_b = w_pa.astype(BF16), w_pb.astype(BF16), w_o.astype(BF16)
    wa_b, wi_b = rg_wa.astype(BF16), rg_wi.astype(BF16)
    ln_g3 = ln_g.reshape(depth * 3, 1, d_model)
    ln_b3 = ln_b.reshape(depth * 3, 1, d_model)
    vec = lambda a: a.reshape(depth, 1, a.shape[-1])
    conv_b3, ba3, bi3, lam3, gn3 = vec(conv_b), vec(rg_ba), vec(rg_bi), vec(rg_lambda), vec(hg_norm_g)

    zeros_h = jnp.zeros((batch, 1, d_rnn), dt)
    zeros_c = jnp.zeros((batch, CONV_W - 1, d_rnn), dt)
    zeros_s = jnp.zeros((1, batch, HG_HEADS, dk, dk), dt)

    common = dict(depth=depth, total_rows=total_rows)
    prompt = dict(row0=0, n_batch=batch, seq_len=p_len, n_seq=1, t_rows=PROMPT_TILE_ROWS, tag="p", **common)
    sample = dict(row0=p_rows, n_batch=dec_batch, seq_len=dec_seq, n_seq=SAMPLE_SEQS_PER_STEP, t_rows=dec_seq,
                  tag="s", **common)

    ph = pc = ps = sh = sc = ss = None
    for l in range(depth):
        x = _ffn(x, w_in2, w_out2, ln_g3, ln_b3, 2 * l, 3 * l, alpha)
        u = _proj(x, w_in, l)
        rg = (conv_w, conv_b3, wa_b, ba3, wi_b, bi3, lam3, l)
        ya, ph, pc = _rglru(u, zeros_c, zeros_h, *rg, (None, ph, pc), **prompt)
        ya, sh, sc = _rglru(u, state_rglru_conv[l], state_rglru_h[l][:, None, :], *rg, (ya, sh, sc), **sample)
        yb, ps = _hgrn(u, zeros_s, 0, hg_lb, gn3, l, (None, ps), chunk=CHUNK, **prompt)
        yb, ss = _hgrn(u, state_hgrn, l, hg_lb, gn3, l, (yb, ss), chunk=dec_seq, **sample)
        x = _merge(x, ya, yb, u, wpa_b, wpb_b, wo_b, ln_g3, ln_b3, l, 3 * l + 1, alpha)
        x = _ffn(x, w_in2, w_out2, ln_g3, ln_b3, 2 * l + 1, 3 * l + 2, alpha)

    y_prompt = x[:p_rows].reshape(batch, p_len, d_model)[:, n_meta:]
    y_sample = x[p_rows:].reshape(dec_batch, dec_seq, d_model)
    return (y_prompt, y_sample, ph[:, :, 0], pc, ps, sh[:, :, 0], sc, ss)
```

```python
import functools

import jax
import jax.numpy as jnp
from jax import lax
from jax.experimental import pallas as pl
from jax.experimental.pallas import tpu as pltpu

F32 = jnp.float32
BF16 = jnp.bfloat16

CONV_W = 4
C_RG = 8.0
RG_BLOCKS = 8
HG_HEADS = 8
CHUNK = 16
LN_EPS = 1e-5
RMS_EPS = 1e-6

LANES = 128
SUBLANES = 8
MXU_DIM = 256
VMEM_LIMIT_BYTES = 60 * 1024 * 1024

TM_FFN = 928
TF_FFN = 256
LN_ROWS = 232
TM_PROJ = 928
TN_PROJ = 1024
TM_MERGE = 464
PROMPT_TILE_ROWS = 688
SAMPLE_SEQS_PER_STEP = 8

_NT = (((1,), (1,)), ((), ()))
_TN = (((0,), (0,)), ((), ()))
_MASKED = -1e30


def _params(semantics):
    return pltpu.CompilerParams(dimension_semantics=semantics, vmem_limit_bytes=VMEM_LIMIT_BYTES)


def _layer_norm(y, g, b):
    mu = jnp.mean(y, axis=-1, keepdims=True)
    d = y - mu
    var = jnp.mean(d * d, axis=-1, keepdims=True)
    return d * lax.rsqrt(var + LN_EPS) * g + b


_ANY = pl.BlockSpec(memory_space=pl.ANY)


def _ffn_body(x_ref, wg_ref, wu_ref, wo_ref, g_ref, b_ref, o_ref, xb_ref, *, tf, n_f, f_dim, alpha):
    j = pl.program_id(1)

    @pl.when(j == 0)
    def _():
        xb_ref[...] = x_ref[...].astype(BF16)
        o_ref[...] = alpha * x_ref[...]

    xb = xb_ref[...]
    gate = jnp.dot(xb, wg_ref[...].astype(BF16), preferred_element_type=F32)
    up = jnp.dot(xb, wu_ref[...].astype(BF16), preferred_element_type=F32)
    h = (0.5 * gate) * jax.nn.sigmoid(gate) * up
    overlap = n_f * tf - f_dim
    if overlap:
        col = lax.broadcasted_iota(jnp.int32, (1, tf), 1)
        h = jnp.where((j < n_f - 1) | (col >= overlap), h, 0.0)
    o_ref[...] += jnp.dot(h.astype(BF16), wo_ref[...].astype(BF16), preferred_element_type=F32)

    @pl.when(j == n_f - 1)
    def _():
        def rows(ci, carry):
            r = pl.ds(pl.multiple_of(ci * LN_ROWS, SUBLANES), LN_ROWS)
            o_ref[r, :] = _layer_norm(o_ref[r, :], g_ref[...], b_ref[...])
            return carry

        lax.fori_loop(0, o_ref.shape[0] // LN_ROWS, rows, 0)


def _ffn(x, w_in2, w_out2, ln_g, ln_b, wsel, ln_idx, alpha):
    m, d = x.shape
    f_dim = w_in2.shape[1] // 2
    tf, tm = TF_FFN, TM_FFN
    n_f = -(-f_dim // tf)
    assert m % tm == 0 and tm % LN_ROWS == 0 and f_dim % LANES == 0 and tf % LANES == 0 and f_dim >= tf
    assert (n_f * tf - f_dim) < tf and d % SUBLANES == 0 and f_dim % SUBLANES == 0
    last = (f_dim - tf) // LANES
    step = tf // LANES

    def chunk(j):
        return jnp.minimum(j * step, last)

    body = functools.partial(_ffn_body, tf=tf, n_f=n_f, f_dim=f_dim, alpha=alpha)
    return pl.pallas_call(
        body,
        grid=(m // tm, n_f),
        in_specs=[
            pl.BlockSpec((tm, d), lambda i, j: (i, 0)),
            pl.BlockSpec((pl.Element(d), pl.Element(tf)), lambda i, j: (wsel * d, chunk(j) * LANES)),
            pl.BlockSpec((pl.Element(d), pl.Element(tf)),
                         lambda i, j: (wsel * d, (f_dim // LANES + chunk(j)) * LANES)),
            pl.BlockSpec((pl.Element(tf), pl.Element(d)),
                         lambda i, j: ((wsel * (f_dim // SUBLANES) + chunk(j) * (LANES // SUBLANES)) * SUBLANES, 0)),
            pl.BlockSpec((None, 1, d), lambda i, j: (ln_idx, 0, 0)),
            pl.BlockSpec((None, 1, d), lambda i, j: (ln_idx, 0, 0)),
        ],
        out_specs=pl.BlockSpec((tm, d), lambda i, j: (i, 0)),
        out_shape=jax.ShapeDtypeStruct((m, d), F32),
        scratch_shapes=[pltpu.VMEM((tm, d), BF16)],
        compiler_params=_params(("parallel", "arbitrary")),
        name=f"ffn_{wsel}",
    )(x, w_in2, w_in2, w_out2, ln_g, ln_b)


PROJ_ORDER = (6, 7, 8, 9, 3, 0, 1, 2, 4, 5)
UF_LOGF, UF_XR = 0, 1
UB_SGA, UB_SGB, UB_K, UB_GR, UB_Q, UB_V, UB_OG = 0, 1, 4, 5, 6, 7, 8
PROJ_ROWS = 232


def _forget_lower_bound(hg_lb, layer):
    e = jnp.exp(hg_lb - jnp.max(hg_lb, axis=0, keepdims=True))
    sm = e / jnp.sum(e, axis=0, keepdims=True)
    cum = sm[0:1, :]
    for m in range(1, layer + 1):
        cum = cum + sm[m:m + 1, :]
    return cum - sm[0:1, :]


def _proj_body(x_ref, w_ref, lb_ref, uf_ref, ub_ref, xb_ref, acc_ref, *, layer):
    j = pl.program_id(1)

    @pl.when(j == 0)
    def _():
        xb_ref[...] = x_ref[...].astype(BF16)

    acc_ref[...] = jnp.dot(xb_ref[...], w_ref[...].astype(BF16), preferred_element_type=F32)

    def by_rows(fn):
        def rows(ci, carry):
            r = pl.ds(pl.multiple_of(ci * PROJ_ROWS, SUBLANES), PROJ_ROWS)
            fn(r, acc_ref[r, :])
            return carry

        lax.fori_loop(0, acc_ref.shape[0] // PROJ_ROWS, rows, 0)

    def store_b(fn):
        def wr(r, a):
            ub_ref[r, :] = fn(a).astype(BF16)

        by_rows(wr)

    col = [PROJ_ORDER.index(cb) for cb in range(len(PROJ_ORDER))]

    @pl.when(j <= col[9])
    def _():
        store_b(jax.nn.sigmoid)

    @pl.when(j == col[3])
    def _():
        lb = _forget_lower_bound(lb_ref[...], layer)

        def wr(r, a):
            f = lb + (1.0 - lb) * jax.nn.sigmoid(a)
            uf_ref[r, :] = jnp.log(f)
            ub_ref[r, :] = (1.0 - f).astype(BF16)

        by_rows(wr)

    @pl.when(j == col[0])
    def _():
        def wr(r, a):
            uf_ref[r, :] = a

        by_rows(wr)

    @pl.when(j == col[1])
    def _():
        store_b(jax.nn.gelu)

    @pl.when((j == col[2]) | (j == col[4]))
    def _():
        store_b(lambda a: a)

    @pl.when(j == col[5])
    def _():
        store_b(lambda a: a * jax.nn.sigmoid(a))


def _proj(x, w, hg_lb, l):
    m, d = x.shape
    tm, tn = TM_PROJ, TN_PROJ
    n_blk = w.shape[-1] // tn
    assert m % tm == 0 and w.shape[-1] % tn == 0 and n_blk == len(PROJ_ORDER) and tm % PROJ_ROWS == 0
    assert PROJ_ORDER[:4] == (6, 7, 8, 9) and PROJ_ORDER[4:6] == (3, 0)

    def wcol(j):
        cb = jnp.int32(PROJ_ORDER[-1])
        for step in range(n_blk - 1):
            cb = jnp.where(j == step, PROJ_ORDER[step], cb)
        return cb

    body = functools.partial(_proj_body, layer=l)
    return pl.pallas_call(
        body,
        grid=(m // tm, n_blk),
        in_specs=[
            pl.BlockSpec((tm, d), lambda i, j: (i, 0)),
            pl.BlockSpec((None, d, tn), lambda i, j: (l, 0, wcol(j))),
            pl.BlockSpec(hg_lb.shape, lambda i, j: (0, 0)),
        ],
        out_specs=[
            pl.BlockSpec((tm, tn), lambda i, j: (i, jnp.where(j >= 5, 1, 0))),
            pl.BlockSpec((tm, tn), lambda i, j: (i, jnp.where(j >= 5, j - 1, j))),
        ],
        out_shape=[jax.ShapeDtypeStruct((m, 2 * tn), F32), jax.ShapeDtypeStruct((m, (n_blk - 1) * tn), BF16)],
        scratch_shapes=[pltpu.VMEM((tm, d), BF16), pltpu.VMEM((tm, tn), F32)],
        compiler_params=_params(("parallel", "arbitrary")),
        name=f"proj_l{l}",
    )(x, w, hg_lb)


def _rglru_body(*refs, n_seq, t_rows, n_t, n_alias):
    xr_ref, gr_ref, c0_ref, h0_ref, cw_ref, cb_ref, wa_ref, ba_ref, wi_ref, bi_ref, lam_ref = refs[:11]
    ya_ref, hl_ref, cn_ref, xbuf, a_scr, b_scr, hcar = refs[11 + n_alias:]
    t = pl.program_id(1)
    c = a_scr.shape[1]
    hist = CONV_W - 1
    base = SUBLANES - hist

    @pl.when(t == 0)
    def _():
        for g in range(n_seq):
            xbuf[g, base:SUBLANES, :] = c0_ref[g]
            hcar[g] = h0_ref[g]

    for g in range(n_seq):
        xbuf[g, SUBLANES:SUBLANES + t_rows, :] = xr_ref[g * t_rows:(g + 1) * t_rows, :]

    log_a_scale = -C_RG * jax.nn.softplus(-lam_ref[...])
    bs = c // RG_BLOCKS
    for n in range(RG_BLOCKS):
        ls = slice(n * bs, (n + 1) * bs)
        for g in range(n_seq):
            xc = cb_ref[:, ls] + sum(
                xbuf[g, base + j:base + j + t_rows, ls] * cw_ref[j:j + 1, ls] for j in range(CONV_W))
            a_scr[g * t_rows:(g + 1) * t_rows, ls] = xc
        xc = a_scr[:, ls]
        xcb = xc.astype(BF16)
        r = jax.nn.sigmoid(jnp.dot(xcb, wa_ref[n], preferred_element_type=F32) + ba_ref[:, ls])
        i = jax.nn.sigmoid(jnp.dot(xcb, wi_ref[n], preferred_element_type=F32) + bi_ref[:, ls])
        log_a = log_a_scale[:, ls] * r
        a = jnp.exp(log_a)
        mult = jnp.sqrt(-jnp.tanh(log_a) * (a * a + 1.0))
        a_scr[:, ls] = a
        b_scr[:, ls] = mult * (i * xc)

    for g in range(n_seq):
        xbuf[g, base:SUBLANES, :] = xbuf[g, base + t_rows:SUBLANES + t_rows, :]

    row = lax.broadcasted_iota(jnp.int32, (SUBLANES, c), 0)
    for g in range(n_seq):
        def group(j, h_in, g=g):
            r0 = pl.multiple_of(g * t_rows + j * SUBLANES, SUBLANES)
            a = a_scr[pl.ds(r0, SUBLANES), :]
            b = b_scr[pl.ds(r0, SUBLANES), :]
            for s in (1, 2, 4):
                keep = row >= s
                b = jnp.where(keep, a * pltpu.roll(b, s, 0) + b, b)
                a = jnp.where(keep, a * pltpu.roll(a, s, 0), a)
            h = a * h_in + b
            a_scr[pl.ds(r0, SUBLANES), :] = h
            return h[SUBLANES - 1:SUBLANES, :]

        hcar[g] = lax.fori_loop(0, t_rows // SUBLANES, group, hcar[g])

    for n in range(RG_BLOCKS):
        ls = slice(n * bs, (n + 1) * bs)
        ya_ref[:, ls] = (a_scr[:, ls] * gr_ref[:, ls].astype(F32)).astype(BF16)

    @pl.when(t == n_t - 1)
    def _():
        for g in range(n_seq):
            hl_ref[g] = hcar[g]
            cn_ref[g] = xbuf[g, base:SUBLANES, :]


def _with_aliases(n_in, prev):
    specs, args, amap = [], [], {}
    for out_idx, p in enumerate(prev):
        if p is not None:
            amap[n_in + len(args)] = out_idx
            specs.append(_ANY)
            args.append(p)
    return specs, args, amap


def _rglru(uf, ub, conv0, h0, conv_w, conv_b, wa_b, ba, wi_b, bi, lam, l, prev, *,
           depth, total_rows, row0, n_batch, seq_len, n_seq, t_rows, tag):
    c = conv_w.shape[-1]
    n_t = seq_len // t_rows
    r = n_seq * t_rows
    assert seq_len % t_rows == 0 and n_batch % n_seq == 0 and row0 % r == 0 and t_rows % SUBLANES == 0
    assert n_seq == 1 or n_t == 1
    blk0 = row0 // r
    hist = CONV_W - 1
    a_specs, a_args, amap = _with_aliases(11, prev)
    body = functools.partial(_rglru_body, n_seq=n_seq, t_rows=t_rows, n_t=n_t, n_alias=len(a_args))
    vec = pl.BlockSpec((None, 1, c), lambda s, t: (l, 0, 0))
    gate_w = pl.BlockSpec((None, RG_BLOCKS, c // RG_BLOCKS, c // RG_BLOCKS), lambda s, t: (l, 0, 0, 0))
    return pl.pallas_call(
        body,
        grid=(n_batch // n_seq, n_t),
        in_specs=[
            pl.BlockSpec((r, c), lambda s, t: (blk0 + s * n_t + t, UF_XR)),
            pl.BlockSpec((r, c), lambda s, t: (blk0 + s * n_t + t, UB_GR)),
            pl.BlockSpec((n_seq, hist, c), lambda s, t: (s, 0, 0)),
            pl.BlockSpec((n_seq, 1, c), lambda s, t: (s, 0, 0)),
            pl.BlockSpec((None, CONV_W, c), lambda s, t: (l, 0, 0)),
            vec, gate_w, vec, gate_w, vec, vec,
        ] + a_specs,
        out_specs=[
            pl.BlockSpec((r, c), lambda s, t: (blk0 + s * n_t + t, 0)),
            pl.BlockSpec((None, n_seq, 1, c), lambda s, t: (l, s, 0, 0)),
            pl.BlockSpec((None, n_seq, hist, c), lambda s, t: (l, s, 0, 0)),
        ],
        out_shape=[
            jax.ShapeDtypeStruct((total_rows, c), BF16),
            jax.ShapeDtypeStruct((depth, n_batch, 1, c), F32),
            jax.ShapeDtypeStruct((depth, n_batch, hist, c), F32),
        ],
        scratch_shapes=[
            pltpu.VMEM((n_seq, SUBLANES + t_rows, c), F32),
            pltpu.VMEM((r, c), F32),
            pltpu.VMEM((r, c), F32),
            pltpu.VMEM((n_seq, 1, c), F32),
        ],
        input_output_aliases=amap,
        compiler_params=_params(("arbitrary", "arbitrary")),
        name=f"rglru_{tag}_l{l}",
    )(uf, ub, conv0, h0, conv_w, conv_b, wa_b, ba, wi_b, bi, lam, *a_args)


def _hgrn_body(*refs, n_seq, t_rows, n_t, chunk, layer, n_alias):
    lf_ref, k_ref, q_ref, v_ref, og_ref, s0_ref, gn_ref = refs[:7]
    yb_ref, so_ref, st_scr, p_scr, r_scr, yo_scr = refs[7 + n_alias:]
    t = pl.program_id(1)
    c = q_ref.shape[1]
    dk = c // HG_HEADS
    n_lane_blk = c // MXU_DIM
    n_rb = chunk // SUBLANES
    n_units = sum(n_rb - s // SUBLANES for s in range(chunk))
    assert n_units % 2 == 0
    n_tiles = n_units // 2

    @pl.when(t == 0)
    def _():
        for g in range(n_seq):
            for h in range(HG_HEADS):
                st_scr[g, h] = s0_ref[g, h].T

    gn = gn_ref[...]

    row_c = lax.broadcasted_iota(jnp.int32, (chunk, c), 0)
    row_8 = lax.broadcasted_iota(jnp.int32, (SUBLANES, c), 0)
    shifts = [s for s in (1, 2, 4, 8, 16) if s < chunk]
    ri = lax.broadcasted_iota(jnp.int32, (MXU_DIM, MXU_DIM), 0)
    ci_ = lax.broadcasted_iota(jnp.int32, (MXU_DIM, MXU_DIM), 1)
    head_ones = ((ri // dk) == (ci_ // dk)).astype(BF16)

    packed_rows = 2 * SUBLANES

    for g in range(n_seq):
        def step(ci, carry, g=g):
            if t_rows == chunk:
                r0 = g * t_rows
            else:
                r0 = pl.multiple_of(g * t_rows + ci * chunk, chunk)
            rows = pl.ds(r0, chunk)

            def load(ref):
                if chunk % packed_rows == 0:
                    return ref[rows, :].astype(F32)
                tile0 = (r0 // packed_rows) * packed_rows
                return ref[tile0:tile0 + packed_rows, :].astype(F32)[r0 - tile0:r0 - tile0 + chunk, :]

            b = lf_ref[rows, :]
            for s in shifts:
                b = b + jnp.where(row_c >= s, pltpu.roll(b, s, 0), 0.0)
            k = load(k_ref)
            q = load(q_ref)
            v = load(v_ref)

            bb = [b[i * SUBLANES:(i + 1) * SUBLANES, :] for i in range(n_rb)]
            qq = [q[i * SUBLANES:(i + 1) * SUBLANES, :] for i in range(n_rb)]
            units = [(s, tb) for s in range(chunk) for tb in range(s // SUBLANES, n_rb)]

            def pair_prod(s, tb):
                d = bb[tb] - b[s:s + 1, :]
                if tb == s // SUBLANES:
                    d = jnp.where(row_8 >= (s % SUBLANES), d, _MASKED)
                return qq[tb] * k[s:s + 1, :] * jnp.exp(d)

            for ti in range(n_tiles):
                pb = jnp.concatenate([pair_prod(*units[2 * ti]), pair_prod(*units[2 * ti + 1])], axis=0).astype(BF16)
                for j in range(n_lane_blk):
                    p_scr[(j * n_tiles + ti) * 16:(j * n_tiles + ti + 1) * 16, :] = pb[:, j * MXU_DIM:(j + 1) * MXU_DIM]
            r_scr[...] = jnp.dot(p_scr[...], head_ones, preferred_element_type=F32)
            intra = [jnp.zeros((SUBLANES, c), F32) for _ in range(n_rb)]
            for un, (s, tb) in enumerate(units):
                ti, hf = divmod(un, 2)
                a_col = jnp.concatenate(
                    [r_scr[(j * n_tiles + ti) * 16 + hf * SUBLANES:(j * n_tiles + ti) * 16 + (hf + 1) * SUBLANES, :]
                     for j in range(n_lane_blk)], axis=1)
                intra[tb] = intra[tb] + a_col * v[s:s + 1, :]
            intra = jnp.concatenate(intra, axis=0) if n_rb > 1 else intra[0]

            b_last = b[chunk - 1:chunk, :]
            qe = (q * jnp.exp(b)).astype(BF16)
            kd = (k * jnp.exp(b_last - b)).astype(BF16)
            e_last = jnp.exp(b_last)
            vb = v.astype(BF16)
            outs = []
            for h in range(HG_HEADS):
                ls = slice(h * dk, (h + 1) * dk)
                st = st_scr[g, h]
                o = intra[:, ls] + lax.dot_general(qe[:, ls], st.astype(BF16), _NT, preferred_element_type=F32)
                outs.append(o * lax.rsqrt(jnp.mean(o * o, axis=-1, keepdims=True) + RMS_EPS))
                st_scr[g, h] = st * e_last[:, ls] + lax.dot_general(
                    vb[:, ls], kd[:, ls], _TN, preferred_element_type=F32)
            yo_scr[rows, :] = jnp.concatenate(outs, axis=-1) * gn * load(og_ref)
            return carry

        if t_rows == chunk:
            step(0, 0)
        else:
            lax.fori_loop(0, t_rows // chunk, step, 0)

    yb_ref[...] = yo_scr[...].astype(BF16)

    @pl.when(t == n_t - 1)
    def _():
        for g in range(n_seq):
            for h in range(HG_HEADS):
                so_ref[g, h] = st_scr[g, h].T


def _hgrn(uf, ub, s0, s0_layer, gn, l, prev, *, depth, total_rows, row0, n_batch, seq_len, n_seq, t_rows, chunk, tag):
    c = gn.shape[-1]
    dk = c // HG_HEADS
    n_t = seq_len // t_rows
    r = n_seq * t_rows
    assert seq_len % t_rows == 0 and n_batch % n_seq == 0 and row0 % r == 0 and t_rows % chunk == 0
    assert n_seq == 1 or n_t == 1
    assert chunk % SUBLANES == 0 and c % MXU_DIM == 0 and MXU_DIM % dk == 0
    blk0 = row0 // r
    n_tiles = sum(chunk // SUBLANES - s // SUBLANES for s in range(chunk)) // 2
    p_rows = (c // MXU_DIM) * n_tiles * 16
    a_specs, a_args, amap = _with_aliases(7, prev)
    body = functools.partial(_hgrn_body, n_seq=n_seq, t_rows=t_rows, n_t=n_t, chunk=chunk, layer=l,
                             n_alias=len(a_args))

    def col(cb):
        return pl.BlockSpec((r, c), lambda s, t: (blk0 + s * n_t + t, cb))

    return pl.pallas_call(
        body,
        grid=(n_batch // n_seq, n_t),
        in_specs=[
            col(UF_LOGF), col(UB_K), col(UB_Q), col(UB_V), col(UB_OG),
            pl.BlockSpec((None, n_seq, HG_HEADS, dk, dk), lambda s, t: (s0_layer, s, 0, 0, 0)),
            pl.BlockSpec((None, 1, c), lambda s, t: (l, 0, 0)),
        ] + a_specs,
        out_specs=[
            pl.BlockSpec((r, c), lambda s, t: (blk0 + s * n_t + t, 0)),
            pl.BlockSpec((None, n_seq, HG_HEADS, dk, dk), lambda s, t: (l, s, 0, 0, 0)),
        ],
        out_shape=[
            jax.ShapeDtypeStruct((total_rows, c), BF16),
            jax.ShapeDtypeStruct((depth, n_batch, HG_HEADS, dk, dk), F32),
        ],
        scratch_shapes=[
            pltpu.VMEM((n_seq, HG_HEADS, dk, dk), F32),
            pltpu.VMEM((p_rows, MXU_DIM), BF16),
            pltpu.VMEM((p_rows, MXU_DIM), F32),
            pltpu.VMEM((r, c), F32),
        ],
        input_output_aliases=amap,
        compiler_params=_params(("arbitrary", "arbitrary")),
        name=f"hgrn_{tag}_l{l}",
    )(uf, ub, ub, ub, ub, s0, gn, *a_args)


def _merge_body(x_ref, ya_ref, yb_ref, ga_ref, gb_ref, wpa_ref, wpb_ref, wo_ref, g_ref, b_ref, o_ref, *, alpha):
    pa = jnp.dot(ya_ref[...], wpa_ref[...], preferred_element_type=F32)
    pb = jnp.dot(yb_ref[...], wpb_ref[...], preferred_element_type=F32)
    merged = ga_ref[...].astype(F32) * pa + gb_ref[...].astype(F32) * pb
    m = jnp.dot(merged.astype(BF16), wo_ref[...], preferred_element_type=F32)
    o_ref[...] = _layer_norm(alpha * x_ref[...] + m, g_ref[...], b_ref[...])


def _merge(x, ya, yb, ub, wpa_b, wpb_b, wo_b, ln_g, ln_b, l, ln_idx, alpha):
    m, d = x.shape
    c = ya.shape[1]
    tm = TM_MERGE
    assert m % tm == 0
    body = functools.partial(_merge_body, alpha=alpha)
    once = pl.Buffered(1)
    return pl.pallas_call(
        body,
        grid=(m // tm,),
        in_specs=[
            pl.BlockSpec((tm, d), lambda i: (i, 0)),
            pl.BlockSpec((tm, c), lambda i: (i, 0)),
            pl.BlockSpec((tm, c), lambda i: (i, 0)),
            pl.BlockSpec((tm, d), lambda i: (i, UB_SGA)),
            pl.BlockSpec((tm, d), lambda i: (i, UB_SGB)),
            pl.BlockSpec((None, c, d), lambda i: (l, 0, 0), pipeline_mode=once),
            pl.BlockSpec((None, c, d), lambda i: (l, 0, 0), pipeline_mode=once),
            pl.BlockSpec((None, d, d), lambda i: (l, 0, 0), pipeline_mode=once),
            pl.BlockSpec((None, 1, d), lambda i: (ln_idx, 0, 0)),
            pl.BlockSpec((None, 1, d), lambda i: (ln_idx, 0, 0)),
        ],
        out_specs=pl.BlockSpec((tm, d), lambda i: (i, 0)),
        out_shape=jax.ShapeDtypeStruct((m, d), F32),
        compiler_params=_params(("parallel",)),
        name=f"merge_l{l}",
    )(x, ya, yb, ub, ub, wpa_b, wpb_b, wo_b, ln_g, ln_b)


def kernel(x_prompt, x_sample, state_rglru_h, state_rglru_conv, state_hgrn, meta_tokens, ln_g, ln_b, ffn_w_in, ffn_w_out, w_in, conv_w, conv_b, rg_wa, rg_ba, rg_wi, rg_bi, rg_lambda, hg_lb, hg_norm_g, w_pa, w_pb, w_o):
    batch, seq, d_model = x_prompt.shape
    dec_batch, dec_seq, _ = x_sample.shape
    depth = w_in.shape[0]
    n_meta = meta_tokens.shape[0]
    d_rnn = conv_w.shape[-1]
    dk = d_rnn // HG_HEADS
    d_ff = ffn_w_out.shape[-2]
    alpha = (2.0 * depth) ** 0.25
    dt = x_prompt.dtype

    p_len = n_meta + seq
    p_rows = batch * p_len
    total_rows = p_rows + dec_batch * dec_seq
    meta = jnp.broadcast_to(meta_tokens.astype(dt)[None], (batch, n_meta, d_model))
    x = jnp.concatenate(
        [jnp.concatenate([meta, x_prompt], axis=1).reshape(p_rows, d_model),
         x_sample.reshape(dec_batch * dec_seq, d_model)], axis=0)

    w_in2 = ffn_w_in.reshape(depth * 2 * d_model, 2 * d_ff)
    w_out2 = ffn_w_out.reshape(depth * 2 * d_ff, d_model)
    wpa_b, wpb_b, wo_b = w_pa.astype(BF16), w_pb.astype(BF16), w_o.astype(BF16)
    wa_b, wi_b = rg_wa.astype(BF16), rg_wi.astype(BF16)
    ln_g3 = ln_g.reshape(depth * 3, 1, d_model)
    ln_b3 = ln_b.reshape(depth * 3, 1, d_model)
    vec = lambda a: a.reshape(depth, 1, a.shape[-1])
    conv_b3, ba3, bi3, lam3, gn3 = vec(conv_b), vec(rg_ba), vec(rg_bi), vec(rg_lambda), vec(hg_norm_g)

    zeros_h = jnp.zeros((batch, 1, d_rnn), dt)
    zeros_c = jnp.zeros((batch, CONV_W - 1, d_rnn), dt)
    zeros_s = jnp.zeros((1, batch, HG_HEADS, dk, dk), dt)

    common = dict(depth=depth, total_rows=total_rows)
    prompt = dict(row0=0, n_batch=batch, seq_len=p_len, n_seq=1, t_rows=PROMPT_TILE_ROWS, tag="p", **common)
    sample = dict(row0=p_rows, n_batch=dec_batch, seq_len=dec_seq, n_seq=SAMPLE_SEQS_PER_STEP, t_rows=dec_seq,
                  tag="s", **common)

    ph = pc = ps = sh = sc = ss = None
    for l in range(depth):
        x = _ffn(x, w_in2, w_out2, ln_g3, ln_b3, 2 * l, 3 * l, alpha)
        uf, ub = _proj(x, w_in, hg_lb, l)
        rg = (conv_w, conv_b3, wa_b, ba3, wi_b, bi3, lam3, l)
        ya, ph, pc = _rglru(uf, ub, zeros_c, zeros_h, *rg, (None, ph, pc), **prompt)
        ya, sh, sc = _rglru(uf, ub, state_rglru_conv[l], state_rglru_h[l][:, None, :], *rg, (ya, sh, sc), **sample)
        yb, ps = _hgrn(uf, ub, zeros_s, 0, gn3, l, (None, ps), chunk=CHUNK, **prompt)
        yb, ss = _hgrn(uf, ub, state_hgrn, l, gn3, l, (yb, ss), chunk=dec_seq, **sample)
        x = _merge(x, ya, yb, ub, wpa_b, wpb_b, wo_b, ln_g3, ln_b3, l, 3 * l + 1, alpha)
        x = _ffn(x, w_in2, w_out2, ln_g3, ln_b3, 2 * l + 1, 3 * l + 2, alpha)

    y_prompt = x[:p_rows].reshape(batch, p_len, d_model)[:, n_meta:]
    y_sample = x[p_rows:].reshape(dec_batch, dec_seq, d_model)
    return (y_prompt, y_sample, ph[:, :, 0], pc, ps, sh[:, :, 0], sc, ss)
```

```python
import functools

import jax
import jax.numpy as jnp
from jax import lax
from jax.experimental import pallas as pl
from jax.experimental.pallas import tpu as pltpu

F32 = jnp.float32
BF16 = jnp.bfloat16

CONV_W = 4
C_RG = 8.0
RG_BLOCKS = 8
HG_HEADS = 8
CHUNK = 16
LN_EPS = 1e-5
RMS_EPS = 1e-6

LANES = 128
SUBLANES = 8
MXU_DIM = 256
VMEM_LIMIT_BYTES = 60 * 1024 * 1024

TM_FFN = 928
TF_FFN = 256
LN_ROWS = 232
TM_PROJ = 928
TN_PROJ = 1024
TM_MERGE = 464
PROMPT_TILE_ROWS = 688
SAMPLE_SEQS_PER_STEP = 8

_NT = (((1,), (1,)), ((), ()))
_TN = (((0,), (0,)), ((), ()))
_MASKED = -1e30
LOG2_E = 1.4426950408889634


def _params(semantics):
    return pltpu.CompilerParams(dimension_semantics=semantics, vmem_limit_bytes=VMEM_LIMIT_BYTES)


def _layer_norm(y, g, b):
    mu = jnp.mean(y, axis=-1, keepdims=True)
    d = y - mu
    var = jnp.mean(d * d, axis=-1, keepdims=True)
    return d * lax.rsqrt(var + LN_EPS) * g + b


_ANY = pl.BlockSpec(memory_space=pl.ANY)


def _ffn_body(x_ref, wg_ref, wu_ref, wo_ref, g_ref, b_ref, o_ref, xb_ref, *, tf, n_f, f_dim, alpha):
    j = pl.program_id(1)

    @pl.when(j == 0)
    def _():
        xb_ref[...] = x_ref[...].astype(BF16)
        o_ref[...] = alpha * x_ref[...]

    xb = xb_ref[...]
    gate = jnp.dot(xb, wg_ref[...].astype(BF16), preferred_element_type=F32)
    up = jnp.dot(xb, wu_ref[...].astype(BF16), preferred_element_type=F32)
    h = (0.5 * gate) * jax.nn.sigmoid(gate) * up
    overlap = n_f * tf - f_dim
    if overlap:
        col = lax.broadcasted_iota(jnp.int32, (1, tf), 1)
        h = jnp.where((j < n_f - 1) | (col >= overlap), h, 0.0)
    o_ref[...] += jnp.dot(h.astype(BF16), wo_ref[...].astype(BF16), preferred_element_type=F32)

    @pl.when(j == n_f - 1)
    def _():
        def rows(ci, carry):
            r = pl.ds(pl.multiple_of(ci * LN_ROWS, SUBLANES), LN_ROWS)
            o_ref[r, :] = _layer_norm(o_ref[r, :], g_ref[...], b_ref[...])
            return carry

        lax.fori_loop(0, o_ref.shape[0] // LN_ROWS, rows, 0)


def _ffn(x, w_in2, w_out2, ln_g, ln_b, wsel, ln_idx, alpha):
    m, d = x.shape
    f_dim = w_in2.shape[1] // 2
    tf, tm = TF_FFN, TM_FFN
    n_f = -(-f_dim // tf)
    assert m % tm == 0 and tm % LN_ROWS == 0 and f_dim % LANES == 0 and tf % LANES == 0 and f_dim >= tf
    assert (n_f * tf - f_dim) < tf and d % SUBLANES == 0 and f_dim % SUBLANES == 0
    last = (f_dim - tf) // LANES
    step = tf // LANES

    def chunk(j):
        return jnp.minimum(j * step, last)

    body = functools.partial(_ffn_body, tf=tf, n_f=n_f, f_dim=f_dim, alpha=alpha)
    return pl.pallas_call(
        body,
        grid=(m // tm, n_f),
        in_specs=[
            pl.BlockSpec((tm, d), lambda i, j: (i, 0)),
            pl.BlockSpec((pl.Element(d), pl.Element(tf)), lambda i, j: (wsel * d, chunk(j) * LANES)),
            pl.BlockSpec((pl.Element(d), pl.Element(tf)),
                         lambda i, j: (wsel * d, (f_dim // LANES + chunk(j)) * LANES)),
            pl.BlockSpec((pl.Element(tf), pl.Element(d)),
                         lambda i, j: ((wsel * (f_dim // SUBLANES) + chunk(j) * (LANES // SUBLANES)) * SUBLANES, 0)),
            pl.BlockSpec((None, 1, d), lambda i, j: (ln_idx, 0, 0)),
            pl.BlockSpec((None, 1, d), lambda i, j: (ln_idx, 0, 0)),
        ],
        out_specs=pl.BlockSpec((tm, d), lambda i, j: (i, 0)),
        out_shape=jax.ShapeDtypeStruct((m, d), F32),
        scratch_shapes=[pltpu.VMEM((tm, d), BF16)],
        compiler_params=_params(("parallel", "arbitrary")),
        name=f"ffn_{wsel}",
    )(x, w_in2, w_in2, w_out2, ln_g, ln_b)


PROJ_ORDER = (6, 7, 8, 9, 3, 0, 1, 2, 4, 5)
UF_LOGF, UF_XR = 0, 1
UB_SGA, UB_SGB, UB_K, UB_GR, UB_Q, UB_V, UB_OG = 0, 1, 4, 5, 6, 7, 8


def _forget_lower_bound(hg_lb, layer):
    e = jnp.exp(hg_lb - jnp.max(hg_lb, axis=0, keepdims=True))
    sm = e / jnp.sum(e, axis=0, keepdims=True)
    cum = sm[0:1, :]
    for m in range(1, layer + 1):
        cum = cum + sm[m:m + 1, :]
    return cum - sm[0:1, :]


def _proj_body(x_ref, w_ref, lb_ref, uf_ref, ub_ref, xb_ref, *, layer):
    j = pl.program_id(1)

    @pl.when(j == 0)
    def _():
        xb_ref[...] = x_ref[...].astype(BF16)

    def acc():
        return jnp.dot(xb_ref[...], w_ref[...].astype(BF16), preferred_element_type=F32)

    col = [PROJ_ORDER.index(cb) for cb in range(len(PROJ_ORDER))]

    @pl.when(j <= col[9])
    def _():
        ub_ref[...] = jax.nn.sigmoid(acc()).astype(BF16)

    @pl.when(j == col[3])
    def _():
        lb = _forget_lower_bound(lb_ref[...], layer)
        f = lb + (1.0 - lb) * jax.nn.sigmoid(acc())
        uf_ref[...] = jnp.log(f)
        ub_ref[...] = (1.0 - f).astype(BF16)

    @pl.when(j == col[0])
    def _():
        uf_ref[...] = acc()

    @pl.when(j == col[1])
    def _():
        ub_ref[...] = jax.nn.gelu(acc()).astype(BF16)

    @pl.when((j == col[2]) | (j == col[4]))
    def _():
        ub_ref[...] = acc().astype(BF16)

    @pl.when(j == col[5])
    def _():
        a = acc()
        ub_ref[...] = (a * jax.nn.sigmoid(a)).astype(BF16)


def _proj(x, w, hg_lb, l):
    m, d = x.shape
    tm, tn = TM_PROJ, TN_PROJ
    n_blk = w.shape[-1] // tn
    assert m % tm == 0 and w.shape[-1] % tn == 0 and n_blk == len(PROJ_ORDER)
    assert PROJ_ORDER[:4] == (6, 7, 8, 9) and PROJ_ORDER[4:6] == (3, 0)

    def wcol(j):
        cb = jnp.int32(PROJ_ORDER[-1])
        for step in range(n_blk - 1):
            cb = jnp.where(j == step, PROJ_ORDER[step], cb)
        return cb

    body = functools.partial(_proj_body, layer=l)
    return pl.pallas_call(
        body,
        grid=(m // tm, n_blk),
        in_specs=[
            pl.BlockSpec((tm, d), lambda i, j: (i, 0)),
            pl.BlockSpec((None, d, tn), lambda i, j: (l, 0, wcol(j))),
            pl.BlockSpec(hg_lb.shape, lambda i, j: (0, 0)),
        ],
        out_specs=[
            pl.BlockSpec((tm, tn), lambda i, j: (i, jnp.where(j >= 5, 1, 0))),
            pl.BlockSpec((tm, tn), lambda i, j: (i, jnp.where(j >= 5, j - 1, j))),
        ],
        out_shape=[jax.ShapeDtypeStruct((m, 2 * tn), F32), jax.ShapeDtypeStruct((m, (n_blk - 1) * tn), BF16)],
        scratch_shapes=[pltpu.VMEM((tm, d), BF16)],
        compiler_params=_params(("parallel", "arbitrary")),
        name=f"proj_l{l}",
    )(x, w, hg_lb)


def _rglru_body(*refs, n_seq, t_rows, n_t, n_alias):
    xr_ref, gr_ref, c0_ref, h0_ref, cw_ref, cb_ref, wa_ref, ba_ref, wi_ref, bi_ref, lam_ref = refs[:11]
    ya_ref, hl_ref, cn_ref, xbuf, a_scr, b_scr, hcar = refs[11 + n_alias:]
    t = pl.program_id(1)
    c = a_scr.shape[1]
    hist = CONV_W - 1
    base = SUBLANES - hist

    @pl.when(t == 0)
    def _():
        for g in range(n_seq):
            xbuf[g, base:SUBLANES, :] = c0_ref[g]
            hcar[g] = h0_ref[g]

    for g in range(n_seq):
        xbuf[g, SUBLANES:SUBLANES + t_rows, :] = xr_ref[g * t_rows:(g + 1) * t_rows, :]

    log_a_scale = -C_RG * jax.nn.softplus(-lam_ref[...])
    bs = c // RG_BLOCKS
    for n in range(RG_BLOCKS):
        ls = slice(n * bs, (n + 1) * bs)
        for g in range(n_seq):
            xc = cb_ref[:, ls] + sum(
                xbuf[g, base + j:base + j + t_rows, ls] * cw_ref[j:j + 1, ls] for j in range(CONV_W))
            a_scr[g * t_rows:(g + 1) * t_rows, ls] = xc
        xc = a_scr[:, ls]
        xcb = xc.astype(BF16)
        r = jax.nn.sigmoid(jnp.dot(xcb, wa_ref[n], preferred_element_type=F32) + ba_ref[:, ls])
        i = jax.nn.sigmoid(jnp.dot(xcb, wi_ref[n], preferred_element_type=F32) + bi_ref[:, ls])
        log_a = log_a_scale[:, ls] * r
        a = jnp.exp(log_a)
        mult = jnp.sqrt(-jnp.tanh(log_a) * (a * a + 1.0))
        a_scr[:, ls] = a
        b_scr[:, ls] = mult * (i * xc)

    for g in range(n_seq):
        xbuf[g, base:SUBLANES, :] = xbuf[g, base + t_rows:SUBLANES + t_rows, :]

    row = lax.broadcasted_iota(jnp.int32, (SUBLANES, c), 0)
    for g in range(n_seq):
        def group(j, h_in, g=g):
            r0 = pl.multiple_of(g * t_rows + j * SUBLANES, SUBLANES)
            a = a_scr[pl.ds(r0, SUBLANES), :]
            b = b_scr[pl.ds(r0, SUBLANES), :]
            for s in (1, 2, 4):
                keep = row >= s
                b = jnp.where(keep, a * pltpu.roll(b, s, 0) + b, b)
                a = jnp.where(keep, a * pltpu.roll(a, s, 0), a)
            h = a * h_in + b
            a_scr[pl.ds(r0, SUBLANES), :] = h
            return h[SUBLANES - 1:SUBLANES, :]

        hcar[g] = lax.fori_loop(0, t_rows // SUBLANES, group, hcar[g])

    for n in range(RG_BLOCKS):
        ls = slice(n * bs, (n + 1) * bs)
        ya_ref[:, ls] = (a_scr[:, ls] * gr_ref[:, ls].astype(F32)).astype(BF16)

    @pl.when(t == n_t - 1)
    def _():
        for g in range(n_seq):
            hl_ref[g] = hcar[g]
            cn_ref[g] = xbuf[g, base:SUBLANES, :]


def _with_aliases(n_in, prev):
    specs, args, amap = [], [], {}
    for out_idx, p in enumerate(prev):
        if p is not None:
            amap[n_in + len(args)] = out_idx
            specs.append(_ANY)
            args.append(p)
    return specs, args, amap


def _rglru(uf, ub, conv0, h0, conv_w, conv_b, wa_b, ba, wi_b, bi, lam, l, prev, *,
           depth, total_rows, row0, n_batch, seq_len, n_seq, t_rows, tag):
    c = conv_w.shape[-1]
    n_t = seq_len // t_rows
    r = n_seq * t_rows
    assert seq_len % t_rows == 0 and n_batch % n_seq == 0 and row0 % r == 0 and t_rows % SUBLANES == 0
    assert n_seq == 1 or n_t == 1
    blk0 = row0 // r
    hist = CONV_W - 1
    a_specs, a_args, amap = _with_aliases(11, prev)
    body = functools.partial(_rglru_body, n_seq=n_seq, t_rows=t_rows, n_t=n_t, n_alias=len(a_args))
    vec = pl.BlockSpec((None, 1, c), lambda s, t: (l, 0, 0))
    gate_w = pl.BlockSpec((None, RG_BLOCKS, c // RG_BLOCKS, c // RG_BLOCKS), lambda s, t: (l, 0, 0, 0))
    return pl.pallas_call(
        body,
        grid=(n_batch // n_seq, n_t),
        in_specs=[
            pl.BlockSpec((r, c), lambda s, t: (blk0 + s * n_t + t, UF_XR)),
            pl.BlockSpec((r, c), lambda s, t: (blk0 + s * n_t + t, UB_GR)),
            pl.BlockSpec((n_seq, hist, c), lambda s, t: (s, 0, 0)),
            pl.BlockSpec((n_seq, 1, c), lambda s, t: (s, 0, 0)),
            pl.BlockSpec((None, CONV_W, c), lambda s, t: (l, 0, 0)),
            vec, gate_w, vec, gate_w, vec, vec,
        ] + a_specs,
        out_specs=[
            pl.BlockSpec((r, c), lambda s, t: (blk0 + s * n_t + t, 0)),
            pl.BlockSpec((None, n_seq, 1, c), lambda s, t: (l, s, 0, 0)),
            pl.BlockSpec((None, n_seq, hist, c), lambda s, t: (l, s, 0, 0)),
        ],
        out_shape=[
            jax.ShapeDtypeStruct((total_rows, c), BF16),
            jax.ShapeDtypeStruct((depth, n_batch, 1, c), F32),
            jax.ShapeDtypeStruct((depth, n_batch, hist, c), F32),
        ],
        scratch_shapes=[
            pltpu.VMEM((n_seq, SUBLANES + t_rows, c), F32),
            pltpu.VMEM((r, c), F32),
            pltpu.VMEM((r, c), F32),
            pltpu.VMEM((n_seq, 1, c), F32),
        ],
        input_output_aliases=amap,
        compiler_params=_params(("arbitrary", "arbitrary")),
        name=f"rglru_{tag}_l{l}",
    )(uf, ub, conv0, h0, conv_w, conv_b, wa_b, ba, wi_b, bi, lam, *a_args)


def _hgrn_body(*refs, n_seq, t_rows, n_t, chunk, layer, n_alias):
    lf_ref, k_ref, q_ref, v_ref, og_ref, s0_ref, gn_ref = refs[:7]
    yb_ref, so_ref, st_scr, p_scr, r_scr, yo_scr = refs[7 + n_alias:]
    t = pl.program_id(1)
    c = q_ref.shape[1]
    dk = c // HG_HEADS
    n_lane_blk = c // MXU_DIM
    n_rb = chunk // SUBLANES
    n_units = sum(n_rb - s // SUBLANES for s in range(chunk))
    assert n_units % 2 == 0
    n_tiles = n_units // 2

    @pl.when(t == 0)
    def _():
        for g in range(n_seq):
            for h in range(HG_HEADS):
                st_scr[g, h] = s0_ref[g, h].T

    gn = gn_ref[...]

    row_c = lax.broadcasted_iota(jnp.int32, (chunk, c), 0)
    row_8 = lax.broadcasted_iota(jnp.int32, (SUBLANES, c), 0)
    shifts = [s for s in (1, 2, 4, 8, 16) if s < chunk]
    ri = lax.broadcasted_iota(jnp.int32, (MXU_DIM, MXU_DIM), 0)
    ci_ = lax.broadcasted_iota(jnp.int32, (MXU_DIM, MXU_DIM), 1)
    head_ones = ((ri // dk) == (ci_ // dk)).astype(BF16)

    packed_rows = 2 * SUBLANES

    for g in range(n_seq):
        def step(ci, carry, g=g):
            if t_rows == chunk:
                r0 = g * t_rows
            else:
                r0 = pl.multiple_of(g * t_rows + ci * chunk, chunk)
            rows = pl.ds(r0, chunk)

            def load(ref):
                if chunk % packed_rows == 0:
                    return ref[rows, :].astype(F32)
                tile0 = (r0 // packed_rows) * packed_rows
                return ref[tile0:tile0 + packed_rows, :].astype(F32)[r0 - tile0:r0 - tile0 + chunk, :]

            b = lf_ref[rows, :] * LOG2_E
            for s in shifts:
                b = b + jnp.where(row_c >= s, pltpu.roll(b, s, 0), 0.0)
            k = load(k_ref)
            q = load(q_ref)
            v = load(v_ref)

            bb = [b[i * SUBLANES:(i + 1) * SUBLANES, :] for i in range(n_rb)]
            qq = [q[i * SUBLANES:(i + 1) * SUBLANES, :] for i in range(n_rb)]
            units = [(s, tb) for s in range(chunk) for tb in range(s // SUBLANES, n_rb)]

            cs = b - jnp.log2(k)
            prods = []
            for s in range(chunk):
                cs_s = jnp.broadcast_to(cs[s:s + 1, :], (SUBLANES, c))
                for tb in range(s // SUBLANES, n_rb):
                    d = bb[tb] - cs_s
                    if tb == s // SUBLANES:
                        d = jnp.where(row_8 >= (s % SUBLANES), d, _MASKED)
                    prods.append(qq[tb] * jnp.exp2(d))
                    if len(prods) == 2:
                        ti = units.index((s, tb)) // 2
                        pb = jnp.concatenate(prods, axis=0).astype(BF16)
                        for j in range(n_lane_blk):
                            p_scr[(j * n_tiles + ti) * 16:(j * n_tiles + ti + 1) * 16, :] = (
                                pb[:, j * MXU_DIM:(j + 1) * MXU_DIM])
                        prods = []
            r_scr[...] = jnp.dot(p_scr[...], head_ones, preferred_element_type=F32)
            intra = [jnp.zeros((SUBLANES, c), F32) for _ in range(n_rb)]
            for s in range(chunk):
                v_s = jnp.broadcast_to(v[s:s + 1, :], (SUBLANES, c))
                for tb in range(s // SUBLANES, n_rb):
                    ti, hf = divmod(units.index((s, tb)), 2)
                    a_col = jnp.concatenate(
                        [r_scr[(j * n_tiles + ti) * 16 + hf * SUBLANES:(j * n_tiles + ti) * 16 + (hf + 1) * SUBLANES, :]
                         for j in range(n_lane_blk)], axis=1)
                    intra[tb] = intra[tb] + a_col * v_s
            intra = jnp.concatenate(intra, axis=0) if n_rb > 1 else intra[0]

            b_last = b[chunk - 1:chunk, :]
            qe = (q * jnp.exp2(b)).astype(BF16)
            kd = (k * jnp.exp2(b_last - b)).astype(BF16)
            e_last = jnp.exp2(b_last)
            vb = v.astype(BF16)
            outs = []
            for h in range(HG_HEADS):
                ls = slice(h * dk, (h + 1) * dk)
                st = st_scr[g, h]
                o = intra[:, ls] + lax.dot_general(qe[:, ls], st.astype(BF16), _NT, preferred_element_type=F32)
                outs.append(o * lax.rsqrt(jnp.mean(o * o, axis=-1, keepdims=True) + RMS_EPS))
                upd = lax.dot_general(vb[:, ls], kd[:, ls], _TN, preferred_element_type=F32)
                decay = jnp.broadcast_to(e_last[:, ls], (SUBLANES, dk))
                new = st.reshape(dk // SUBLANES, SUBLANES, dk) * decay[None] + upd.reshape(dk // SUBLANES, SUBLANES, dk)
                st_scr[g, h] = new.reshape(dk, dk)
            yo_scr[rows, :] = jnp.concatenate(outs, axis=-1) * gn * load(og_ref)
            return carry

        if t_rows == chunk:
            step(0, 0)
        else:
            lax.fori_loop(0, t_rows // chunk, step, 0)

    yb_ref[...] = yo_scr[...].astype(BF16)

    @pl.when(t == n_t - 1)
    def _():
        for g in range(n_seq):
            for h in range(HG_HEADS):
                so_ref[g, h] = st_scr[g, h].T


def _hgrn(uf, ub, s0, s0_layer, gn, l, prev, *, depth, total_rows, row0, n_batch, seq_len, n_seq, t_rows, chunk, tag):
    c = gn.shape[-1]
    dk = c // HG_HEADS
    n_t = seq_len // t_rows
    r = n_seq * t_rows
    assert seq_len % t_rows == 0 and n_batch % n_seq == 0 and row0 % r == 0 and t_rows % chunk == 0
    assert n_seq == 1 or n_t == 1
    assert chunk % SUBLANES == 0 and c % MXU_DIM == 0 and MXU_DIM % dk == 0
    blk0 = row0 // r
    n_tiles = sum(chunk // SUBLANES - s // SUBLANES for s in range(chunk)) // 2
    p_rows = (c // MXU_DIM) * n_tiles * 16
    a_specs, a_args, amap = _with_aliases(7, prev)
    body = functools.partial(_hgrn_body, n_seq=n_seq, t_rows=t_rows, n_t=n_t, chunk=chunk, layer=l,
                             n_alias=len(a_args))

    def col(cb):
        return pl.BlockSpec((r, c), lambda s, t: (blk0 + s * n_t + t, cb))

    return pl.pallas_call(
        body,
        grid=(n_batch // n_seq, n_t),
        in_specs=[
            col(UF_LOGF), col(UB_K), col(UB_Q), col(UB_V), col(UB_OG),
            pl.BlockSpec((None, n_seq, HG_HEADS, dk, dk), lambda s, t: (s0_layer, s, 0, 0, 0)),
            pl.BlockSpec((None, 1, c), lambda s, t: (l, 0, 0)),
        ] + a_specs,
        out_specs=[
            pl.BlockSpec((r, c), lambda s, t: (blk0 + s * n_t + t, 0)),
            pl.BlockSpec((None, n_seq, HG_HEADS, dk, dk), lambda s, t: (l, s, 0, 0, 0)),
        ],
        out_shape=[
            jax.ShapeDtypeStruct((total_rows, c), BF16),
            jax.ShapeDtypeStruct((depth, n_batch, HG_HEADS, dk, dk), F32),
        ],
        scratch_shapes=[
            pltpu.VMEM((n_seq, HG_HEADS, dk, dk), F32),
            pltpu.VMEM((p_rows, MXU_DIM), BF16),
            pltpu.VMEM((p_rows, MXU_DIM), F32),
            pltpu.VMEM((r, c), F32),
        ],
        input_output_aliases=amap,
        compiler_params=_params(("arbitrary", "arbitrary")),
        name=f"hgrn_{tag}_l{l}",
    )(uf, ub, ub, ub, ub, s0, gn, *a_args)


def _merge_body(x_ref, ya_ref, yb_ref, ga_ref, gb_ref, wpa_ref, wpb_ref, wo_ref, g_ref, b_ref, o_ref, *, alpha):
    pa = jnp.dot(ya_ref[...], wpa_ref[...], preferred_element_type=F32)
    pb = jnp.dot(yb_ref[...], wpb_ref[...], preferred_element_type=F32)
    merged = ga_ref[...].astype(F32) * pa + gb_ref[...].astype(F32) * pb
    m = jnp.dot(merged.astype(BF16), wo_ref[...], preferred_element_type=F32)
    o_ref[...] = _layer_norm(alpha * x_ref[...] + m, g_ref[...], b_ref[...])


def _merge(x, ya, yb, ub, wpa_b, wpb_b, wo_b, ln_g, ln_b, l, ln_idx, alpha):
    m, d = x.shape
    c = ya.shape[1]
    tm = TM_MERGE
    assert m % tm == 0
    body = functools.partial(_merge_body, alpha=alpha)
    once = pl.Buffered(1)
    return pl.pallas_call(
        body,
        grid=(m // tm,),
        in_specs=[
            pl.BlockSpec((tm, d), lambda i: (i, 0)),
            pl.BlockSpec((tm, c), lambda i: (i, 0)),
            pl.BlockSpec((tm, c), lambda i: (i, 0)),
            pl.BlockSpec((tm, d), lambda i: (i, UB_SGA)),
            pl.BlockSpec((tm, d), lambda i: (i, UB_SGB)),
            pl.BlockSpec((None, c, d), lambda i: (l, 0, 0), pipeline_mode=once),
            pl.BlockSpec((None, c, d), lambda i: (l, 0, 0), pipeline_mode=once),
            pl.BlockSpec((None, d, d), lambda i: (l, 0, 0), pipeline_mode=once),
            pl.BlockSpec((None, 1, d), lambda i: (ln_idx, 0, 0)),
            pl.BlockSpec((None, 1, d), lambda i: (ln_idx, 0, 0)),
        ],
        out_specs=pl.BlockSpec((tm, d), lambda i: (i, 0)),
        out_shape=jax.ShapeDtypeStruct((m, d), F32),
        compiler_params=_params(("parallel",)),
        name=f"merge_l{l}",
    )(x, ya, yb, ub, ub, wpa_b, wpb_b, wo_b, ln_g, ln_b)


def kernel(x_prompt, x_sample, state_rglru_h, state_rglru_conv, state_hgrn, meta_tokens, ln_g, ln_b, ffn_w_in, ffn_w_out, w_in, conv_w, conv_b, rg_wa, rg_ba, rg_wi, rg_bi, rg_lambda, hg_lb, hg_norm_g, w_pa, w_pb, w_o):
    batch, seq, d_model = x_prompt.shape
    dec_batch, dec_seq, _ = x_sample.shape
    depth = w_in.shape[0]
    n_meta = meta_tokens.shape[0]
    d_rnn = conv_w.shape[-1]
    dk = d_rnn // HG_HEADS
    d_ff = ffn_w_out.shape[-2]
    alpha = (2.0 * depth) ** 0.25
    dt = x_prompt.dtype

    p_len = n_meta + seq
    p_rows = batch * p_len
    total_rows = p_rows + dec_batch * dec_seq
    meta = jnp.broadcast_to(meta_tokens.astype(dt)[None], (batch, n_meta, d_model))
    x = jnp.concatenate(
        [jnp.concatenate([meta, x_prompt], axis=1).reshape(p_rows, d_model),
         x_sample.reshape(dec_batch * dec_seq, d_model)], axis=0)

    w_in2 = ffn_w_in.reshape(depth * 2 * d_model, 2 * d_ff)
    w_out2 = ffn_w_out.reshape(depth * 2 * d_ff, d_model)
    wpa_b, wpb_b, wo_b = w_pa.astype(BF16), w_pb.astype(BF16), w_o.astype(BF16)
    wa_b, wi_b = rg_wa.astype(BF16), rg_wi.astype(BF16)
    ln_g3 = ln_g.reshape(depth * 3, 1, d_model)
    ln_b3 = ln_b.reshape(depth * 3, 1, d_model)
    vec = lambda a: a.reshape(depth, 1, a.shape[-1])
    conv_b3, ba3, bi3, lam3, gn3 = vec(conv_b), vec(rg_ba), vec(rg_bi), vec(rg_lambda), vec(hg_norm_g)

    zeros_h = jnp.zeros((batch, 1, d_rnn), dt)
    zeros_c = jnp.zeros((batch, CONV_W - 1, d_rnn), dt)
    zeros_s = jnp.zeros((1, batch, HG_HEADS, dk, dk), dt)

    common = dict(depth=depth, total_rows=total_rows)
    prompt = dict(row0=0, n_batch=batch, seq_len=p_len, n_seq=1, t_rows=PROMPT_TILE_ROWS, tag="p", **common)
    sample = dict(row0=p_rows, n_batch=dec_batch, seq_len=dec_seq, n_seq=SAMPLE_SEQS_PER_STEP, t_rows=dec_seq,
                  tag="s", **common)

    ph = pc = ps = sh = sc = ss = None
    for l in range(depth):
        x = _ffn(x, w_in2, w_out2, ln_g3, ln_b3, 2 * l, 3 * l, alpha)
        uf, ub = _proj(x, w_in, hg_lb, l)
        rg = (conv_w, conv_b3, wa_b, ba3, wi_b, bi3, lam3, l)
        ya, ph, pc = _rglru(uf, ub, zeros_c, zeros_h, *rg, (None, ph, pc), **prompt)
        ya, sh, sc = _rglru(uf, ub, state_rglru_conv[l], state_rglru_h[l][:, None, :], *rg, (ya, sh, sc), **sample)
        yb, ps = _hgrn(uf, ub, zeros_s, 0, gn3, l, (None, ps), chunk=CHUNK, **prompt)
        yb, ss = _hgrn(uf, ub, state_hgrn, l, gn3, l, (yb, ss), chunk=dec_seq, **sample)
        x = _merge(x, ya, yb, ub, wpa_b, wpb_b, wo_b, ln_g3, ln_b3, l, 3 * l + 1, alpha)
        x = _ffn(x, w_in2, w_out2, ln_g3, ln_b3, 2 * l + 1, 3 * l + 2, alpha)

    y_prompt = jnp.stack([x[b * p_len + n_meta:(b + 1) * p_len] for b in range(batch)])
    y_sample = x[p_rows:].reshape(dec_batch, dec_seq, d_model)
    return (y_prompt, y_sample, ph[:, :, 0], pc, ps, sh[:, :, 0], sc, ss)
```

```python
import functools

import jax
import jax.numpy as jnp
from jax import lax
from jax.experimental import pallas as pl
from jax.experimental.pallas import tpu as pltpu

F32 = jnp.float32
BF16 = jnp.bfloat16

CONV_W = 4
C_RG = 8.0
RG_BLOCKS = 8
HG_HEADS = 8
CHUNK = 16
LN_EPS = 1e-5
RMS_EPS = 1e-6

LANES = 128
SUBLANES = 8
MXU_DIM = 256
VMEM_LIMIT_BYTES = 60 * 1024 * 1024

TM_FFN = 928
TF_FFN = 256
LN_ROWS = 232
TM_PROJ = 928
TN_PROJ = 1024
TM_MERGE = 464
PROMPT_TILE_ROWS = 688
SAMPLE_SEQS_PER_STEP = 8

_NT = (((1,), (1,)), ((), ()))
_TN = (((0,), (0,)), ((), ()))
_MASKED = -1e30
LOG2_E = 1.4426950408889634


def _params(semantics):
    return pltpu.CompilerParams(dimension_semantics=semantics, vmem_limit_bytes=VMEM_LIMIT_BYTES)


def _layer_norm(y, g, b):
    mu = jnp.mean(y, axis=-1, keepdims=True)
    d = y - mu
    var = jnp.mean(d * d, axis=-1, keepdims=True)
    return d * lax.rsqrt(var + LN_EPS) * g + b


_ANY = pl.BlockSpec(memory_space=pl.ANY)


def _ffn_body(x_ref, wg_ref, wu_ref, wo_ref, g_ref, b_ref, o_ref, xb_ref, *, tf, n_f, f_dim, alpha):
    j = pl.program_id(1)

    @pl.when(j == 0)
    def _():
        xb_ref[...] = x_ref[...].astype(BF16)
        o_ref[...] = alpha * x_ref[...]

    xb = xb_ref[...]
    gate = jnp.dot(xb, wg_ref[...].astype(BF16), preferred_element_type=F32)
    up = jnp.dot(xb, wu_ref[...].astype(BF16), preferred_element_type=F32)
    h = (0.5 * gate) * jax.nn.sigmoid(gate) * up
    overlap = n_f * tf - f_dim
    if overlap:
        col = lax.broadcasted_iota(jnp.int32, (1, tf), 1)
        h = jnp.where((j < n_f - 1) | (col >= overlap), h, 0.0)
    o_ref[...] += jnp.dot(h.astype(BF16), wo_ref[...].astype(BF16), preferred_element_type=F32)

    @pl.when(j == n_f - 1)
    def _():
        def rows(ci, carry):
            r = pl.ds(pl.multiple_of(ci * LN_ROWS, SUBLANES), LN_ROWS)
            o_ref[r, :] = _layer_norm(o_ref[r, :], g_ref[...], b_ref[...])
            return carry

        lax.fori_loop(0, o_ref.shape[0] // LN_ROWS, rows, 0)


def _ffn(x, w_in2, w_out2, ln_g, ln_b, wsel, ln_idx, alpha):
    m, d = x.shape
    f_dim = w_in2.shape[1] // 2
    tf, tm = TF_FFN, TM_FFN
    n_f = -(-f_dim // tf)
    assert m % tm == 0 and tm % LN_ROWS == 0 and f_dim % LANES == 0 and tf % LANES == 0 and f_dim >= tf
    assert (n_f * tf - f_dim) < tf and d % SUBLANES == 0 and f_dim % SUBLANES == 0
    last = (f_dim - tf) // LANES
    step = tf // LANES

    def chunk(j):
        return jnp.minimum(j * step, last)

    body = functools.partial(_ffn_body, tf=tf, n_f=n_f, f_dim=f_dim, alpha=alpha)
    return pl.pallas_call(
        body,
        grid=(m // tm, n_f),
        in_specs=[
            pl.BlockSpec((tm, d), lambda i, j: (i, 0)),
            pl.BlockSpec((pl.Element(d), pl.Element(tf)), lambda i, j: (wsel * d, chunk(j) * LANES)),
            pl.BlockSpec((pl.Element(d), pl.Element(tf)),
                         lambda i, j: (wsel * d, (f_dim // LANES + chunk(j)) * LANES)),
            pl.BlockSpec((pl.Element(tf), pl.Element(d)),
                         lambda i, j: ((wsel * (f_dim // SUBLANES) + chunk(j) * (LANES // SUBLANES)) * SUBLANES, 0)),
            pl.BlockSpec((None, 1, d), lambda i, j: (ln_idx, 0, 0)),
            pl.BlockSpec((None, 1, d), lambda i, j: (ln_idx, 0, 0)),
        ],
        out_specs=pl.BlockSpec((tm, d), lambda i, j: (i, 0)),
        out_shape=jax.ShapeDtypeStruct((m, d), F32),
        scratch_shapes=[pltpu.VMEM((tm, d), BF16)],
        compiler_params=_params(("parallel", "arbitrary")),
        name=f"ffn_{wsel}",
    )(x, w_in2, w_in2, w_out2, ln_g, ln_b)


PROJ_ORDER = (6, 7, 8, 9, 3, 0, 1, 2, 4, 5)
UF_LOGF, UF_XR = 0, 1
UB_SGA, UB_SGB, UB_K, UB_GR, UB_Q, UB_V, UB_OG = 0, 1, 4, 5, 6, 7, 8


def _forget_lower_bound(hg_lb, layer):
    e = jnp.exp(hg_lb - jnp.max(hg_lb, axis=0, keepdims=True))
    sm = e / jnp.sum(e, axis=0, keepdims=True)
    cum = sm[0:1, :]
    for m in range(1, layer + 1):
        cum = cum + sm[m:m + 1, :]
    return cum - sm[0:1, :]


def _proj_body(x_ref, w_ref, lb_ref, uf_ref, ub_ref, xb_ref, *, layer):
    j = pl.program_id(1)

    @pl.when(j == 0)
    def _():
        xb_ref[...] = x_ref[...].astype(BF16)

    def acc():
        return jnp.dot(xb_ref[...], w_ref[...].astype(BF16), preferred_element_type=F32)

    col = [PROJ_ORDER.index(cb) for cb in range(len(PROJ_ORDER))]

    @pl.when(j <= col[9])
    def _():
        ub_ref[...] = jax.nn.sigmoid(acc()).astype(BF16)

    @pl.when(j == col[3])
    def _():
        lb = _forget_lower_bound(lb_ref[...], layer)
        f = lb + (1.0 - lb) * jax.nn.sigmoid(acc())
        uf_ref[...] = jnp.log(f)
        ub_ref[...] = (1.0 - f).astype(BF16)

    @pl.when(j == col[0])
    def _():
        uf_ref[...] = acc()

    @pl.when(j == col[1])
    def _():
        ub_ref[...] = jax.nn.gelu(acc()).astype(BF16)

    @pl.when((j == col[2]) | (j == col[4]))
    def _():
        ub_ref[...] = acc().astype(BF16)

    @pl.when(j == col[5])
    def _():
        a = acc()
        ub_ref[...] = (a * jax.nn.sigmoid(a)).astype(BF16)


def _proj(x, w, hg_lb, l):
    m, d = x.shape
    tm, tn = TM_PROJ, TN_PROJ
    n_blk = w.shape[-1] // tn
    assert m % tm == 0 and w.shape[-1] % tn == 0 and n_blk == len(PROJ_ORDER)
    assert PROJ_ORDER[:4] == (6, 7, 8, 9) and PROJ_ORDER[4:6] == (3, 0)

    def wcol(j):
        cb = jnp.int32(PROJ_ORDER[-1])
        for step in range(n_blk - 1):
            cb = jnp.where(j == step, PROJ_ORDER[step], cb)
        return cb

    body = functools.partial(_proj_body, layer=l)
    return pl.pallas_call(
        body,
        grid=(m // tm, n_blk),
        in_specs=[
            pl.BlockSpec((tm, d), lambda i, j: (i, 0)),
            pl.BlockSpec((None, d, tn), lambda i, j: (l, 0, wcol(j))),
            pl.BlockSpec(hg_lb.shape, lambda i, j: (0, 0)),
        ],
        out_specs=[
            pl.BlockSpec((tm, tn), lambda i, j: (i, jnp.where(j >= 5, 1, 0))),
            pl.BlockSpec((tm, tn), lambda i, j: (i, jnp.where(j >= 5, j - 1, j))),
        ],
        out_shape=[jax.ShapeDtypeStruct((m, 2 * tn), F32), jax.ShapeDtypeStruct((m, (n_blk - 1) * tn), BF16)],
        scratch_shapes=[pltpu.VMEM((tm, d), BF16)],
        compiler_params=_params(("parallel", "arbitrary")),
        name=f"proj_l{l}",
    )(x, w, hg_lb)


def _rglru_body(*refs, n_seq, t_rows, n_t, n_alias):
    xr_ref, gr_ref, c0_ref, h0_ref, cw_ref, cb_ref, wa_ref, ba_ref, wi_ref, bi_ref, lam_ref = refs[:11]
    ya_ref, hl_ref, cn_ref, xbuf, a_scr, b_scr, hcar = refs[11 + n_alias:]
    t = pl.program_id(1)
    c = a_scr.shape[1]
    hist = CONV_W - 1
    base = SUBLANES - hist

    @pl.when(t == 0)
    def _():
        for g in range(n_seq):
            xbuf[g, base:SUBLANES, :] = c0_ref[g]
            hcar[g] = h0_ref[g]

    for g in range(n_seq):
        xbuf[g, SUBLANES:SUBLANES + t_rows, :] = xr_ref[g * t_rows:(g + 1) * t_rows, :]

    log_a_scale = -C_RG * jax.nn.softplus(-lam_ref[...])
    bs = c // RG_BLOCKS
    for n in range(RG_BLOCKS):
        ls = slice(n * bs, (n + 1) * bs)
        for g in range(n_seq):
            xc = cb_ref[:, ls] + sum(
                xbuf[g, base + j:base + j + t_rows, ls] * cw_ref[j:j + 1, ls] for j in range(CONV_W))
            a_scr[g * t_rows:(g + 1) * t_rows, ls] = xc
        xc = a_scr[:, ls]
        xcb = xc.astype(BF16)
        r = jax.nn.sigmoid(jnp.dot(xcb, wa_ref[n], preferred_element_type=F32) + ba_ref[:, ls])
        i = jax.nn.sigmoid(jnp.dot(xcb, wi_ref[n], preferred_element_type=F32) + bi_ref[:, ls])
        log_a = log_a_scale[:, ls] * r
        a = jnp.exp(log_a)
        mult = jnp.sqrt(-jnp.tanh(log_a) * (a * a + 1.0))
        a_scr[:, ls] = a
        b_scr[:, ls] = mult * (i * xc)

    for g in range(n_seq):
        xbuf[g, base:SUBLANES, :] = xbuf[g, base + t_rows:SUBLANES + t_rows, :]

    row = lax.broadcasted_iota(jnp.int32, (SUBLANES, c), 0)
    for g in range(n_seq):
        def group(j, h_in, g=g):
            r0 = pl.multiple_of(g * t_rows + j * SUBLANES, SUBLANES)
            a = a_scr[pl.ds(r0, SUBLANES), :]
            b = b_scr[pl.ds(r0, SUBLANES), :]
            for s in (1, 2, 4):
                keep = row >= s
                b = jnp.where(keep, a * pltpu.roll(b, s, 0) + b, b)
                a = jnp.where(keep, a * pltpu.roll(a, s, 0), a)
            h = a * h_in + b
            a_scr[pl.ds(r0, SUBLANES), :] = h
            return h[SUBLANES - 1:SUBLANES, :]

        hcar[g] = lax.fori_loop(0, t_rows // SUBLANES, group, hcar[g])

    for n in range(RG_BLOCKS):
        ls = slice(n * bs, (n + 1) * bs)
        ya_ref[:, ls] = (a_scr[:, ls] * gr_ref[:, ls].astype(F32)).astype(BF16)

    @pl.when(t == n_t - 1)
    def _():
        for g in range(n_seq):
            hl_ref[g] = hcar[g]
            cn_ref[g] = xbuf[g, base:SUBLANES, :]


def _with_aliases(n_in, prev):
    specs, args, amap = [], [], {}
    for out_idx, p in enumerate(prev):
        if p is not None:
            amap[n_in + len(args)] = out_idx
            specs.append(_ANY)
            args.append(p)
    return specs, args, amap


def _rglru(uf, ub, conv0, h0, conv_w, conv_b, wa_b, ba, wi_b, bi, lam, l, prev, *,
           depth, total_rows, row0, n_batch, seq_len, n_seq, t_rows, tag):
    c = conv_w.shape[-1]
    n_t = seq_len // t_rows
    r = n_seq * t_rows
    assert seq_len % t_rows == 0 and n_batch % n_seq == 0 and row0 % r == 0 and t_rows % SUBLANES == 0
    assert n_seq == 1 or n_t == 1
    blk0 = row0 // r
    hist = CONV_W - 1
    a_specs, a_args, amap = _with_aliases(11, prev)
    body = functools.partial(_rglru_body, n_seq=n_seq, t_rows=t_rows, n_t=n_t, n_alias=len(a_args))
    vec = pl.BlockSpec((None, 1, c), lambda s, t: (l, 0, 0))
    gate_w = pl.BlockSpec((None, RG_BLOCKS, c // RG_BLOCKS, c // RG_BLOCKS), lambda s, t: (l, 0, 0, 0))
    return pl.pallas_call(
        body,
        grid=(n_batch // n_seq, n_t),
        in_specs=[
            pl.BlockSpec((r, c), lambda s, t: (blk0 + s * n_t + t, UF_XR)),
            pl.BlockSpec((r, c), lambda s, t: (blk0 + s * n_t + t, UB_GR)),
            pl.BlockSpec((n_seq, hist, c), lambda s, t: (s, 0, 0)),
            pl.BlockSpec((n_seq, 1, c), lambda s, t: (s, 0, 0)),
            pl.BlockSpec((None, CONV_W, c), lambda s, t: (l, 0, 0)),
            vec, gate_w, vec, gate_w, vec, vec,
        ] + a_specs,
        out_specs=[
            pl.BlockSpec((r, c), lambda s, t: (blk0 + s * n_t + t, 0)),
            pl.BlockSpec((None, n_seq, 1, c), lambda s, t: (l, s, 0, 0)),
            pl.BlockSpec((None, n_seq, hist, c), lambda s, t: (l, s, 0, 0)),
        ],
        out_shape=[
            jax.ShapeDtypeStruct((total_rows, c), BF16),
            jax.ShapeDtypeStruct((depth, n_batch, 1, c), F32),
            jax.ShapeDtypeStruct((depth, n_batch, hist, c), F32),
        ],
        scratch_shapes=[
            pltpu.VMEM((n_seq, SUBLANES + t_rows, c), F32),
            pltpu.VMEM((r, c), F32),
            pltpu.VMEM((r, c), F32),
            pltpu.VMEM((n_seq, 1, c), F32),
        ],
        input_output_aliases=amap,
        compiler_params=_params(("arbitrary", "arbitrary")),
        name=f"rglru_{tag}_l{l}",
    )(uf, ub, conv0, h0, conv_w, conv_b, wa_b, ba, wi_b, bi, lam, *a_args)


def _hgrn_body(*refs, n_seq, t_rows, n_t, chunk, n_alias):
    lf_ref, k_ref, q_ref, v_ref, og_ref, s0_ref, gn_ref = refs[:7]
    yb_ref, so_ref, st_scr, p_scr, yo_scr = refs[7 + n_alias:]
    t = pl.program_id(1)
    c = q_ref.shape[1]
    dk = c // HG_HEADS
    n_lane_blk = c // MXU_DIM
    n_rb = chunk // SUBLANES
    n_units = sum(n_rb - s // SUBLANES for s in range(chunk))
    assert n_units % 2 == 0
    n_tiles = n_units // 2

    single_chunk = t_rows == chunk

    if not single_chunk:
        @pl.when(t == 0)
        def _():
            for g in range(n_seq):
                for h in range(HG_HEADS):
                    st_scr[g, h] = s0_ref[g, h].T

    gn = gn_ref[...]

    row_c = lax.broadcasted_iota(jnp.int32, (chunk, c), 0)
    row_8 = lax.broadcasted_iota(jnp.int32, (SUBLANES, MXU_DIM), 0)
    shifts = [s for s in (1, 2, 4, 8, 16) if s < chunk]
    ri = lax.broadcasted_iota(jnp.int32, (MXU_DIM, MXU_DIM), 0)
    ci_ = lax.broadcasted_iota(jnp.int32, (MXU_DIM, MXU_DIM), 1)
    head_ones = ((ri // dk) == (ci_ // dk)).astype(BF16)

    packed_rows = 2 * SUBLANES

    row_h = lax.broadcasted_iota(jnp.int32, (SUBLANES, dk), 0)
    ones_h = jnp.ones((SUBLANES, dk), BF16)

    def chunk_body(g, ci):
        r0 = g * t_rows if single_chunk else pl.multiple_of(g * t_rows + ci * chunk, chunk)
        rows = pl.ds(r0, chunk)

        def load(ref):
            if chunk % packed_rows == 0:
                return ref[rows, :].astype(F32)
            tile0 = (r0 // packed_rows) * packed_rows
            return ref[tile0:tile0 + packed_rows, :].astype(F32)[r0 - tile0:r0 - tile0 + chunk, :]

        b = lf_ref[rows, :] * LOG2_E
        for s in shifts:
            b = b + jnp.where(row_c >= s, pltpu.roll(b, s, 0), 0.0)
        k = load(k_ref)
        q = load(q_ref)
        v = load(v_ref)

        cs = b - jnp.log2(k)
        b_last = b[chunk - 1:chunk, :]
        qe = (q * jnp.exp2(b)).astype(BF16)
        kd = (k * jnp.exp2(b_last - b)).astype(BF16)
        e_last = jnp.exp2(b_last)
        vb = v.astype(BF16)
        units = [(s, tb) for s in range(chunk) for tb in range(s // SUBLANES, n_rb)]

        outs = []
        for j in range(n_lane_blk):
            lj = slice(j * MXU_DIM, (j + 1) * MXU_DIM)
            bb = [b[i * SUBLANES:(i + 1) * SUBLANES, lj] for i in range(n_rb)]
            qq = [q[i * SUBLANES:(i + 1) * SUBLANES, lj] for i in range(n_rb)]
            prods = []
            for s in range(chunk):
                cs_s = jnp.broadcast_to(cs[s:s + 1, lj], (SUBLANES, MXU_DIM))
                for tb in range(s // SUBLANES, n_rb):
                    d = bb[tb] - cs_s
                    if tb == s // SUBLANES:
                        d = jnp.where(row_8 >= (s % SUBLANES), d, _MASKED)
                    prods.append(qq[tb] * jnp.exp2(d))
                    if len(prods) == 2:
                        ti = units.index((s, tb)) // 2
                        p_scr[g, j, ti * 16:(ti + 1) * 16, :] = jnp.concatenate(prods, axis=0).astype(BF16)
                        prods = []
            a_cols = jnp.dot(p_scr[g, j], head_ones, preferred_element_type=F32)
            intra = [jnp.zeros((SUBLANES, MXU_DIM), F32) for _ in range(n_rb)]
            for s in range(chunk):
                v_s = jnp.broadcast_to(v[s:s + 1, lj], (SUBLANES, MXU_DIM))
                for tb in range(s // SUBLANES, n_rb):
                    un = units.index((s, tb))
                    intra[tb] = intra[tb] + a_cols[un * SUBLANES:(un + 1) * SUBLANES, :] * v_s
            intra = jnp.concatenate(intra, axis=0) if n_rb > 1 else intra[0]
            for h in range(j * (MXU_DIM // dk), (j + 1) * (MXU_DIM // dk)):
                ls = slice(h * dk, (h + 1) * dk)
                li = slice(h * dk - j * MXU_DIM, (h + 1) * dk - j * MXU_DIM)
                if single_chunk:
                    s_in = s0_ref[g, h]
                    inter = jnp.dot(qe[:, ls], s_in.astype(BF16), preferred_element_type=F32)
                    upd = lax.dot_general(kd[:, ls], vb[:, ls], _TN, preferred_element_type=F32)
                    e_hi = e_last[:, ls].astype(BF16)
                    rem = e_last[:, ls] - e_hi.astype(F32)
                    e_mid = rem.astype(BF16)
                    e_lo = (rem - e_mid.astype(F32)).astype(BF16)
                    pieces = jnp.where(row_h == 0, e_hi.astype(F32), jnp.where(
                        row_h == 1, e_mid.astype(F32), jnp.where(row_h == 2, e_lo.astype(F32), 0.0)))
                    decay = lax.dot_general(pieces.astype(BF16), ones_h, _TN, preferred_element_type=F32)
                    so_ref[g, h] = s_in * decay + upd
                else:
                    st = st_scr[g, h]
                    inter = lax.dot_general(qe[:, ls], st.astype(BF16), _NT, preferred_element_type=F32)
                    upd = lax.dot_general(vb[:, ls], kd[:, ls], _TN, preferred_element_type=F32)
                    decay = jnp.broadcast_to(e_last[:, ls], (SUBLANES, dk))
                    new = (st.reshape(dk // SUBLANES, SUBLANES, dk) * decay[None]
                           + upd.reshape(dk // SUBLANES, SUBLANES, dk))
                    st_scr[g, h] = new.reshape(dk, dk)
                o = intra[:, li] + inter
                outs.append(o * lax.rsqrt(jnp.mean(o * o, axis=-1, keepdims=True) + RMS_EPS))
        yo_scr[rows, :] = jnp.concatenate(outs, axis=-1) * gn * load(og_ref)

    if single_chunk:
        for g in range(n_seq):
            chunk_body(g, 0)
    else:
        def step(ci, carry):
            for g in range(n_seq):
                chunk_body(g, ci)
            return carry

        lax.fori_loop(0, t_rows // chunk, step, 0)

    yb_ref[...] = yo_scr[...].astype(BF16)

    if not single_chunk:
        @pl.when(t == n_t - 1)
        def _():
            for g in range(n_seq):
                for h in range(HG_HEADS):
                    so_ref[g, h] = st_scr[g, h].T


def _hgrn(uf, ub, s0, s0_layer, gn, l, prev, *, depth, total_rows, row0, n_batch, seq_len, n_seq, t_rows, chunk, tag):
    c = gn.shape[-1]
    dk = c // HG_HEADS
    n_t = seq_len // t_rows
    r = n_seq * t_rows
    assert seq_len % t_rows == 0 and n_batch % n_seq == 0 and row0 % r == 0 and t_rows % chunk == 0
    assert n_seq == 1 or n_t == 1
    assert t_rows != chunk or n_t == 1
    assert chunk % SUBLANES == 0 and c % MXU_DIM == 0 and MXU_DIM % dk == 0
    blk0 = row0 // r
    n_tiles = sum(chunk // SUBLANES - s // SUBLANES for s in range(chunk)) // 2
    a_specs, a_args, amap = _with_aliases(7, prev)
    body = functools.partial(_hgrn_body, n_seq=n_seq, t_rows=t_rows, n_t=n_t, chunk=chunk, n_alias=len(a_args))

    def col(cb):
        return pl.BlockSpec((r, c), lambda s, t: (blk0 + s * n_t + t, cb))

    return pl.pallas_call(
        body,
        grid=(n_batch // n_seq, n_t),
        in_specs=[
            col(UF_LOGF), col(UB_K), col(UB_Q), col(UB_V), col(UB_OG),
            pl.BlockSpec((None, n_seq, HG_HEADS, dk, dk), lambda s, t: (s0_layer, s, 0, 0, 0)),
            pl.BlockSpec((None, 1, c), lambda s, t: (l, 0, 0)),
        ] + a_specs,
        out_specs=[
            pl.BlockSpec((r, c), lambda s, t: (blk0 + s * n_t + t, 0)),
            pl.BlockSpec((None, n_seq, HG_HEADS, dk, dk), lambda s, t: (l, s, 0, 0, 0)),
        ],
        out_shape=[
            jax.ShapeDtypeStruct((total_rows, c), BF16),
            jax.ShapeDtypeStruct((depth, n_batch, HG_HEADS, dk, dk), F32),
        ],
        scratch_shapes=[
            pltpu.VMEM((n_seq, HG_HEADS, dk, dk), F32),
            pltpu.VMEM((n_seq, c // MXU_DIM, n_tiles * 16, MXU_DIM), BF16),
            pltpu.VMEM((r, c), F32),
        ],
        input_output_aliases=amap,
        compiler_params=_params(("arbitrary", "arbitrary")),
        name=f"hgrn_{tag}_l{l}",
    )(uf, ub, ub, ub, ub, s0, gn, *a_args)


def _merge_body(x_ref, ya_ref, yb_ref, ga_ref, gb_ref, wpa_ref, wpb_ref, wo_ref, g_ref, b_ref, o_ref, *, alpha):
    pa = jnp.dot(ya_ref[...], wpa_ref[...], preferred_element_type=F32)
    pb = jnp.dot(yb_ref[...], wpb_ref[...], preferred_element_type=F32)
    merged = ga_ref[...].astype(F32) * pa + gb_ref[...].astype(F32) * pb
    m = jnp.dot(merged.astype(BF16), wo_ref[...], preferred_element_type=F32)
    o_ref[...] = _layer_norm(alpha * x_ref[...] + m, g_ref[...], b_ref[...])


def _merge(x, ya, yb, ub, wpa_b, wpb_b, wo_b, ln_g, ln_b, l, ln_idx, alpha):
    m, d = x.shape
    c = ya.shape[1]
    tm = TM_MERGE
    assert m % tm == 0
    body = functools.partial(_merge_body, alpha=alpha)
    once = pl.Buffered(1)
    return pl.pallas_call(
        body,
        grid=(m // tm,),
        in_specs=[
            pl.BlockSpec((tm, d), lambda i: (i, 0)),
            pl.BlockSpec((tm, c), lambda i: (i, 0)),
            pl.BlockSpec((tm, c), lambda i: (i, 0)),
            pl.BlockSpec((tm, d), lambda i: (i, UB_SGA)),
            pl.BlockSpec((tm, d), lambda i: (i, UB_SGB)),
            pl.BlockSpec((None, c, d), lambda i: (l, 0, 0), pipeline_mode=once),
            pl.BlockSpec((None, c, d), lambda i: (l, 0, 0), pipeline_mode=once),
            pl.BlockSpec((None, d, d), lambda i: (l, 0, 0), pipeline_mode=once),
            pl.BlockSpec((None, 1, d), lambda i: (ln_idx, 0, 0)),
            pl.BlockSpec((None, 1, d), lambda i: (ln_idx, 0, 0)),
        ],
        out_specs=pl.BlockSpec((tm, d), lambda i: (i, 0)),
        out_shape=jax.ShapeDtypeStruct((m, d), F32),
        compiler_params=_params(("parallel",)),
        name=f"merge_l{l}",
    )(x, ya, yb, ub, ub, wpa_b, wpb_b, wo_b, ln_g, ln_b)


def _start_then_wait(copies):
    for cp in copies:
        cp.start()
    for cp in copies:
        cp.wait()


def _assemble_body(xp_ref, xs_ref, meta_ref, o_ref, sems, *, batch, p_len, n_meta):
    seq = p_len - n_meta
    copies = []
    for b in range(batch):
        copies.append(pltpu.make_async_copy(meta_ref, o_ref.at[pl.ds(b * p_len, n_meta), :], sems.at[2 * b]))
        copies.append(pltpu.make_async_copy(xp_ref.at[b], o_ref.at[pl.ds(b * p_len + n_meta, seq), :],
                                            sems.at[2 * b + 1]))
    copies.append(pltpu.make_async_copy(xs_ref, o_ref.at[pl.ds(batch * p_len, xs_ref.shape[0]), :],
                                        sems.at[2 * batch]))
    _start_then_wait(copies)


def _assemble(x_prompt, x_sample2, meta):
    batch, seq, d = x_prompt.shape
    n_meta = meta.shape[0]
    p_len = n_meta + seq
    body = functools.partial(_assemble_body, batch=batch, p_len=p_len, n_meta=n_meta)
    return pl.pallas_call(
        body,
        in_specs=[_ANY, _ANY, _ANY],
        out_specs=_ANY,
        out_shape=jax.ShapeDtypeStruct((batch * p_len + x_sample2.shape[0], d), x_prompt.dtype),
        scratch_shapes=[pltpu.SemaphoreType.DMA((2 * batch + 1,))],
        name="assemble",
    )(x_prompt, x_sample2, meta)


def _extract_body(x_ref, yp_ref, ys_ref, sems, *, batch, p_len, n_meta):
    seq = p_len - n_meta
    copies = [pltpu.make_async_copy(x_ref.at[pl.ds(b * p_len + n_meta, seq), :], yp_ref.at[b], sems.at[b])
              for b in range(batch)]
    copies.append(pltpu.make_async_copy(x_ref.at[pl.ds(batch * p_len, ys_ref.shape[0]), :], ys_ref, sems.at[batch]))
    _start_then_wait(copies)


def _extract(x, batch, p_len, n_meta):
    d = x.shape[1]
    s_rows = x.shape[0] - batch * p_len
    body = functools.partial(_extract_body, batch=batch, p_len=p_len, n_meta=n_meta)
    return pl.pallas_call(
        body,
        in_specs=[_ANY],
        out_specs=[_ANY, _ANY],
        out_shape=[jax.ShapeDtypeStruct((batch, p_len - n_meta, d), x.dtype),
                   jax.ShapeDtypeStruct((s_rows, d), x.dtype)],
        scratch_shapes=[pltpu.SemaphoreType.DMA((batch + 1,))],
        name="extract",
    )(x)


def kernel(x_prompt, x_sample, state_rglru_h, state_rglru_conv, state_hgrn, meta_tokens, ln_g, ln_b, ffn_w_in, ffn_w_out, w_in, conv_w, conv_b, rg_wa, rg_ba, rg_wi, rg_bi, rg_lambda, hg_lb, hg_norm_g, w_pa, w_pb, w_o):
    batch, seq, d_model = x_prompt.shape
    dec_batch, dec_seq, _ = x_sample.shape
    depth = w_in.shape[0]
    n_meta = meta_tokens.shape[0]
    d_rnn = conv_w.shape[-1]
    dk = d_rnn // HG_HEADS
    d_ff = ffn_w_out.shape[-2]
    alpha = (2.0 * depth) ** 0.25
    dt = x_prompt.dtype

    p_len = n_meta + seq
    p_rows = batch * p_len
    total_rows = p_rows + dec_batch * dec_seq
    x = _assemble(x_prompt, x_sample.reshape(dec_batch * dec_seq, d_model), meta_tokens.astype(dt))

    w_in2 = ffn_w_in.reshape(depth * 2 * d_model, 2 * d_ff)
    w_out2 = ffn_w_out.reshape(depth * 2 * d_ff, d_model)
    wpa_b, wpb_b, wo_b = w_pa.astype(BF16), w_pb.astype(BF16), w_o.astype(BF16)
    wa_b, wi_b = rg_wa.astype(BF16), rg_wi.astype(BF16)
    ln_g3 = ln_g.reshape(depth * 3, 1, d_model)
    ln_b3 = ln_b.reshape(depth * 3, 1, d_model)
    vec = lambda a: a.reshape(depth, 1, a.shape[-1])
    conv_b3, ba3, bi3, lam3, gn3 = vec(conv_b), vec(rg_ba), vec(rg_bi), vec(rg_lambda), vec(hg_norm_g)

    zeros_h = jnp.zeros((batch, 1, d_rnn), dt)
    zeros_c = jnp.zeros((batch, CONV_W - 1, d_rnn), dt)
    zeros_s = jnp.zeros((1, batch, HG_HEADS, dk, dk), dt)

    common = dict(depth=depth, total_rows=total_rows)
    prompt = dict(row0=0, n_batch=batch, seq_len=p_len, n_seq=1, t_rows=PROMPT_TILE_ROWS, tag="p", **common)
    sample = dict(row0=p_rows, n_batch=dec_batch, seq_len=dec_seq, n_seq=SAMPLE_SEQS_PER_STEP, t_rows=dec_seq,
                  tag="s", **common)

    ph = pc = ps = sh = sc = ss = None
    for l in range(depth):
        x = _ffn(x, w_in2, w_out2, ln_g3, ln_b3, 2 * l, 3 * l, alpha)
        uf, ub = _proj(x, w_in, hg_lb, l)
        rg = (conv_w, conv_b3, wa_b, ba3, wi_b, bi3, lam3, l)
        ya, ph, pc = _rglru(uf, ub, zeros_c, zeros_h, *rg, (None, ph, pc), **prompt)
        ya, sh, sc = _rglru(uf, ub, state_rglru_conv[l], state_rglru_h[l][:, None, :], *rg, (ya, sh, sc), **sample)
        yb, ps = _hgrn(uf, ub, zeros_s, 0, gn3, l, (None, ps), chunk=CHUNK, **prompt)
        yb, ss = _hgrn(uf, ub, state_hgrn, l, gn3, l, (yb, ss), chunk=dec_seq, **sample)
        x = _merge(x, ya, yb, ub, wpa_b, wpb_b, wo_b, ln_g3, ln_b3, l, 3 * l + 1, alpha)
        x = _ffn(x, w_in2, w_out2, ln_g3, ln_b3, 2 * l + 1, 3 * l + 2, alpha)

    y_prompt, y_sample = _extract(x, batch, p_len, n_meta)
    y_sample = y_sample.reshape(dec_batch, dec_seq, d_model)
    return (y_prompt, y_sample, ph[:, :, 0], pc, ps, sh[:, :, 0], sc, ss)
```

```python
import functools

import jax
import jax.numpy as jnp
from jax import lax
from jax.experimental import pallas as pl
from jax.experimental.pallas import tpu as pltpu

F32 = jnp.float32
BF16 = jnp.bfloat16

CONV_W = 4
C_RG = 8.0
RG_BLOCKS = 8
HG_HEADS = 8
CHUNK = 16
LN_EPS = 1e-5
RMS_EPS = 1e-6

LANES = 128
SUBLANES = 8
MXU_DIM = 256
VMEM_LIMIT_BYTES = 60 * 1024 * 1024

TM_FFN = 928
TF_FFN = 256
LN_ROWS = 232
TM_PROJ = 1160
TN_PROJ = 1024
TM_MERGE = 464
PROMPT_TILE_ROWS = 688
SAMPLE_SEQS_PER_STEP = 8

_NT = (((1,), (1,)), ((), ()))
_TN = (((0,), (0,)), ((), ()))
_MASKED = -1e30
LOG2_E = 1.4426950408889634


def _params(semantics):
    return pltpu.CompilerParams(dimension_semantics=semantics, vmem_limit_bytes=VMEM_LIMIT_BYTES)


def _layer_norm(y, g, b):
    mu = jnp.mean(y, axis=-1, keepdims=True)
    d = y - mu
    var = jnp.mean(d * d, axis=-1, keepdims=True)
    return d * lax.rsqrt(var + LN_EPS) * g + b


_ANY = pl.BlockSpec(memory_space=pl.ANY)


def _ffn_body(x_ref, wg_ref, wu_ref, wo_ref, g_ref, b_ref, o_ref, *rest, tf, n_f, f_dim, alpha):
    xb_ref = rest[-1]
    ob_ref = rest[0] if len(rest) == 2 else None
    j = pl.program_id(1)

    @pl.when(j == 0)
    def _():
        xb_ref[...] = x_ref[...].astype(BF16)
        o_ref[...] = alpha * x_ref[...]

    xb = xb_ref[...]
    gate = jnp.dot(xb, wg_ref[...].astype(BF16), preferred_element_type=F32)
    up = jnp.dot(xb, wu_ref[...].astype(BF16), preferred_element_type=F32)
    h = (0.5 * gate) * jax.nn.sigmoid(gate) * up
    overlap = n_f * tf - f_dim
    if overlap:
        col = lax.broadcasted_iota(jnp.int32, (1, tf), 1)
        h = jnp.where((j < n_f - 1) | (col >= overlap), h, 0.0)
    o_ref[...] += jnp.dot(h.astype(BF16), wo_ref[...].astype(BF16), preferred_element_type=F32)

    @pl.when(j == n_f - 1)
    def _():
        def rows(ci, carry):
            r = pl.ds(pl.multiple_of(ci * LN_ROWS, SUBLANES), LN_ROWS)
            o_ref[r, :] = _layer_norm(o_ref[r, :], g_ref[...], b_ref[...])
            return carry

        lax.fori_loop(0, o_ref.shape[0] // LN_ROWS, rows, 0)
        if ob_ref is not None:
            ob_ref[...] = o_ref[...].astype(BF16)


def _ffn(x, w_in2, w_out2, ln_g, ln_b, wsel, ln_idx, alpha, emit_bf16=False):
    m, d = x.shape
    f_dim = w_in2.shape[1] // 2
    tf, tm = TF_FFN, TM_FFN
    n_f = -(-f_dim // tf)
    assert m % tm == 0 and tm % LN_ROWS == 0 and f_dim % LANES == 0 and tf % LANES == 0 and f_dim >= tf
    assert (n_f * tf - f_dim) < tf and d % SUBLANES == 0 and f_dim % SUBLANES == 0
    last = (f_dim - tf) // LANES
    step = tf // LANES

    def chunk(j):
        return jnp.minimum(j * step, last)

    body = functools.partial(_ffn_body, tf=tf, n_f=n_f, f_dim=f_dim, alpha=alpha)
    return pl.pallas_call(
        body,
        grid=(m // tm, n_f),
        in_specs=[
            pl.BlockSpec((tm, d), lambda i, j: (i, 0)),
            pl.BlockSpec((pl.Element(d), pl.Element(tf)), lambda i, j: (wsel * d, chunk(j) * LANES)),
            pl.BlockSpec((pl.Element(d), pl.Element(tf)),
                         lambda i, j: (wsel * d, (f_dim // LANES + chunk(j)) * LANES)),
            pl.BlockSpec((pl.Element(tf), pl.Element(d)),
                         lambda i, j: ((wsel * (f_dim // SUBLANES) + chunk(j) * (LANES // SUBLANES)) * SUBLANES, 0)),
            pl.BlockSpec((None, 1, d), lambda i, j: (ln_idx, 0, 0)),
            pl.BlockSpec((None, 1, d), lambda i, j: (ln_idx, 0, 0)),
        ],
        out_specs=[pl.BlockSpec((tm, d), lambda i, j: (i, 0))] * (2 if emit_bf16 else 1),
        out_shape=[jax.ShapeDtypeStruct((m, d), F32)] + ([jax.ShapeDtypeStruct((m, d), BF16)] if emit_bf16 else []),
        scratch_shapes=[pltpu.VMEM((tm, d), BF16)],
        compiler_params=_params(("parallel", "arbitrary")),
        name=f"ffn_{wsel}",
    )(x, w_in2, w_in2, w_out2, ln_g, ln_b)


PROJ_ORDER = (6, 7, 8, 9, 3, 0, 1, 2, 4, 5)
UF_LOGF, UF_XR = 0, 1
UB_SGA, UB_SGB, UB_K, UB_GR, UB_Q, UB_V, UB_OG = 0, 1, 4, 5, 6, 7, 8


def _forget_lower_bound(hg_lb, layer):
    e = jnp.exp(hg_lb - jnp.max(hg_lb, axis=0, keepdims=True))
    sm = e / jnp.sum(e, axis=0, keepdims=True)
    cum = sm[0:1, :]
    for m in range(1, layer + 1):
        cum = cum + sm[m:m + 1, :]
    return cum - sm[0:1, :]


def _proj_body(xb_ref, w_ref, lb_ref, uf_ref, ub_ref, *, layer):
    j = pl.program_id(1)

    def acc():
        return jnp.dot(xb_ref[...], w_ref[...].astype(BF16), preferred_element_type=F32)

    col = [PROJ_ORDER.index(cb) for cb in range(len(PROJ_ORDER))]

    @pl.when(j <= col[9])
    def _():
        ub_ref[...] = jax.nn.sigmoid(acc()).astype(BF16)

    @pl.when(j == col[3])
    def _():
        lb = _forget_lower_bound(lb_ref[...], layer)
        f = lb + (1.0 - lb) * jax.nn.sigmoid(acc())
        uf_ref[...] = jnp.log(f)
        ub_ref[...] = (1.0 - f).astype(BF16)

    @pl.when(j == col[0])
    def _():
        uf_ref[...] = acc()

    @pl.when(j == col[1])
    def _():
        ub_ref[...] = jax.nn.gelu(acc()).astype(BF16)

    @pl.when((j == col[2]) | (j == col[4]))
    def _():
        ub_ref[...] = acc().astype(BF16)

    @pl.when(j == col[5])
    def _():
        a = acc()
        ub_ref[...] = (a * jax.nn.sigmoid(a)).astype(BF16)


def _proj(xb, w, hg_lb, l):
    m, d = xb.shape
    tm, tn = TM_PROJ, TN_PROJ
    n_blk = w.shape[-1] // tn
    assert m % tm == 0 and w.shape[-1] % tn == 0 and n_blk == len(PROJ_ORDER)
    assert PROJ_ORDER[:4] == (6, 7, 8, 9) and PROJ_ORDER[4:6] == (3, 0)

    def wcol(j):
        cb = jnp.int32(PROJ_ORDER[-1])
        for step in range(n_blk - 1):
            cb = jnp.where(j == step, PROJ_ORDER[step], cb)
        return cb

    body = functools.partial(_proj_body, layer=l)
    return pl.pallas_call(
        body,
        grid=(m // tm, n_blk),
        in_specs=[
            pl.BlockSpec((tm, d), lambda i, j: (i, 0)),
            pl.BlockSpec((None, d, tn), lambda i, j: (l, 0, wcol(j))),
            pl.BlockSpec(hg_lb.shape, lambda i, j: (0, 0)),
        ],
        out_specs=[
            pl.BlockSpec((tm, tn), lambda i, j: (i, jnp.where(j >= 5, 1, 0))),
            pl.BlockSpec((tm, tn), lambda i, j: (i, jnp.where(j >= 5, j - 1, j))),
        ],
        out_shape=[jax.ShapeDtypeStruct((m, 2 * tn), F32), jax.ShapeDtypeStruct((m, (n_blk - 1) * tn), BF16)],
        compiler_params=_params(("parallel", "arbitrary")),
        name=f"proj_l{l}",
    )(xb, w, hg_lb)


def _rglru_body(*refs, n_seq, t_rows, n_t, n_alias):
    xr_ref, gr_ref, c0_ref, h0_ref, cw_ref, cb_ref, wa_ref, ba_ref, wi_ref, bi_ref, lam_ref = refs[:11]
    ya_ref, hl_ref, cn_ref, xbuf, a_scr, b_scr, hcar = refs[11 + n_alias:]
    t = pl.program_id(1)
    c = a_scr.shape[1]
    hist = CONV_W - 1
    base = SUBLANES - hist

    @pl.when(t == 0)
    def _():
        for g in range(n_seq):
            xbuf[g, base:SUBLANES, :] = c0_ref[g]
            hcar[g] = h0_ref[g]

    for g in range(n_seq):
        xbuf[g, SUBLANES:SUBLANES + t_rows, :] = xr_ref[g * t_rows:(g + 1) * t_rows, :]

    log_a_scale = -C_RG * jax.nn.softplus(-lam_ref[...])
    bs = c // RG_BLOCKS
    for n in range(RG_BLOCKS):
        ls = slice(n * bs, (n + 1) * bs)
        for g in range(n_seq):
            xc = cb_ref[:, ls] + sum(
                xbuf[g, base + j:base + j + t_rows, ls] * cw_ref[j:j + 1, ls] for j in range(CONV_W))
            a_scr[g * t_rows:(g + 1) * t_rows, ls] = xc
        xc = a_scr[:, ls]
        xcb = xc.astype(BF16)
        r = jax.nn.sigmoid(jnp.dot(xcb, wa_ref[n], preferred_element_type=F32) + ba_ref[:, ls])
        i = jax.nn.sigmoid(jnp.dot(xcb, wi_ref[n], preferred_element_type=F32) + bi_ref[:, ls])
        log_a = log_a_scale[:, ls] * r
        a = jnp.exp(log_a)
        mult = jnp.sqrt(-jnp.tanh(log_a) * (a * a + 1.0))
        a_scr[:, ls] = a
        b_scr[:, ls] = mult * (i * xc)

    for g in range(n_seq):
        xbuf[g, base:SUBLANES, :] = xbuf[g, base + t_rows:SUBLANES + t_rows, :]

    row = lax.broadcasted_iota(jnp.int32, (SUBLANES, c), 0)
    for g in range(n_seq):
        def group(j, h_in, g=g):
            r0 = pl.multiple_of(g * t_rows + j * SUBLANES, SUBLANES)
            a = a_scr[pl.ds(r0, SUBLANES), :]
            b = b_scr[pl.ds(r0, SUBLANES), :]
            for s in (1, 2, 4):
                keep = row >= s
                b = jnp.where(keep, a * pltpu.roll(b, s, 0) + b, b)
                a = jnp.where(keep, a * pltpu.roll(a, s, 0), a)
            h = a * h_in + b
            a_scr[pl.ds(r0, SUBLANES), :] = h
            return h[SUBLANES - 1:SUBLANES, :]

        hcar[g] = lax.fori_loop(0, t_rows // SUBLANES, group, hcar[g])

    for n in range(RG_BLOCKS):
        ls = slice(n * bs, (n + 1) * bs)
        ya_ref[:, ls] = (a_scr[:, ls] * gr_ref[:, ls].astype(F32)).astype(BF16)

    @pl.when(t == n_t - 1)
    def _():
        for g in range(n_seq):
            hl_ref[g] = hcar[g]
            cn_ref[g] = xbuf[g, base:SUBLANES, :]


def _with_aliases(n_in, prev):
    specs, args, amap = [], [], {}
    for out_idx, p in enumerate(prev):
        if p is not None:
            amap[n_in + len(args)] = out_idx
            specs.append(_ANY)
            args.append(p)
    return specs, args, amap


def _rglru(uf, ub, conv0, h0, conv_w, conv_b, wa_b, ba, wi_b, bi, lam, l, prev, *,
           depth, total_rows, row0, n_batch, seq_len, n_seq, t_rows, tag):
    c = conv_w.shape[-1]
    n_t = seq_len // t_rows
    r = n_seq * t_rows
    assert seq_len % t_rows == 0 and n_batch % n_seq == 0 and row0 % r == 0 and t_rows % SUBLANES == 0
    assert n_seq == 1 or n_t == 1
    blk0 = row0 // r
    hist = CONV_W - 1
    a_specs, a_args, amap = _with_aliases(11, prev)
    body = functools.partial(_rglru_body, n_seq=n_seq, t_rows=t_rows, n_t=n_t, n_alias=len(a_args))
    vec = pl.BlockSpec((None, 1, c), lambda s, t: (l, 0, 0))
    gate_w = pl.BlockSpec((None, RG_BLOCKS, c // RG_BLOCKS, c // RG_BLOCKS), lambda s, t: (l, 0, 0, 0))
    return pl.pallas_call(
        body,
        grid=(n_batch // n_seq, n_t),
        in_specs=[
            pl.BlockSpec((r, c), lambda s, t: (blk0 + s * n_t + t, UF_XR)),
            pl.BlockSpec((r, c), lambda s, t: (blk0 + s * n_t + t, UB_GR)),
            pl.BlockSpec((n_seq, hist, c), lambda s, t: (s, 0, 0)),
            pl.BlockSpec((n_seq, 1, c), lambda s, t: (s, 0, 0)),
            pl.BlockSpec((None, CONV_W, c), lambda s, t: (l, 0, 0)),
            vec, gate_w, vec, gate_w, vec, vec,
        ] + a_specs,
        out_specs=[
            pl.BlockSpec((r, c), lambda s, t: (blk0 + s * n_t + t, 0)),
            pl.BlockSpec((None, n_seq, 1, c), lambda s, t: (l, s, 0, 0)),
            pl.BlockSpec((None, n_seq, hist, c), lambda s, t: (l, s, 0, 0)),
        ],
        out_shape=[
            jax.ShapeDtypeStruct((total_rows, c), BF16),
            jax.ShapeDtypeStruct((depth, n_batch, 1, c), F32),
            jax.ShapeDtypeStruct((depth, n_batch, hist, c), F32),
        ],
        scratch_shapes=[
            pltpu.VMEM((n_seq, SUBLANES + t_rows, c), F32),
            pltpu.VMEM((r, c), F32),
            pltpu.VMEM((r, c), F32),
            pltpu.VMEM((n_seq, 1, c), F32),
        ],
        input_output_aliases=amap,
        compiler_params=_params(("arbitrary", "arbitrary")),
        name=f"rglru_{tag}_l{l}",
    )(uf, ub, conv0, h0, conv_w, conv_b, wa_b, ba, wi_b, bi, lam, *a_args)


def _hgrn_body(*refs, n_seq, t_rows, n_t, chunk, n_alias):
    lf_ref, k_ref, q_ref, v_ref, og_ref, s0_ref, gn_ref = refs[:7]
    yb_ref, so_ref, st_scr, p_scr, yo_scr = refs[7 + n_alias:]
    t = pl.program_id(1)
    c = q_ref.shape[1]
    dk = c // HG_HEADS
    n_lane_blk = c // MXU_DIM
    n_rb = chunk // SUBLANES
    n_units = sum(n_rb - s // SUBLANES for s in range(chunk))
    assert n_units % 2 == 0
    n_tiles = n_units // 2

    single_chunk = t_rows == chunk

    if not single_chunk:
        @pl.when(t == 0)
        def _():
            for g in range(n_seq):
                for h in range(HG_HEADS):
                    st_scr[g, h] = s0_ref[g, h].T

    gn = gn_ref[...]

    row_c = lax.broadcasted_iota(jnp.int32, (chunk, c), 0)
    row_8 = lax.broadcasted_iota(jnp.int32, (SUBLANES, MXU_DIM), 0)
    shifts = [s for s in (1, 2, 4, 8, 16) if s < chunk]
    ri = lax.broadcasted_iota(jnp.int32, (MXU_DIM, MXU_DIM), 0)
    ci_ = lax.broadcasted_iota(jnp.int32, (MXU_DIM, MXU_DIM), 1)
    head_ones = ((ri // dk) == (ci_ // dk)).astype(BF16)

    packed_rows = 2 * SUBLANES

    row_h = lax.broadcasted_iota(jnp.int32, (SUBLANES, dk), 0)
    ones_h = jnp.ones((SUBLANES, dk), BF16)

    def chunk_body(g, ci):
        r0 = g * t_rows if single_chunk else pl.multiple_of(g * t_rows + ci * chunk, chunk)
        rows = pl.ds(r0, chunk)

        def load(ref):
            if chunk % packed_rows == 0:
                return ref[rows, :].astype(F32)
            tile0 = (r0 // packed_rows) * packed_rows
            return ref[tile0:tile0 + packed_rows, :].astype(F32)[r0 - tile0:r0 - tile0 + chunk, :]

        b = lf_ref[rows, :] * LOG2_E
        for s in shifts:
            b = b + jnp.where(row_c >= s, pltpu.roll(b, s, 0), 0.0)
        k = load(k_ref)
        q = load(q_ref)
        v = load(v_ref)

        cs = b - jnp.log2(k)
        b_last = b[chunk - 1:chunk, :]
        qe = (q * jnp.exp2(b)).astype(BF16)
        kd = (k * jnp.exp2(b_last - b)).astype(BF16)
        e_last = jnp.exp2(b_last)
        vb = v.astype(BF16)
        units = [(s, tb) for s in range(chunk) for tb in range(s // SUBLANES, n_rb)]

        outs = []
        for j in range(n_lane_blk):
            lj = slice(j * MXU_DIM, (j + 1) * MXU_DIM)
            bb = [b[i * SUBLANES:(i + 1) * SUBLANES, lj] for i in range(n_rb)]
            qq = [q[i * SUBLANES:(i + 1) * SUBLANES, lj] for i in range(n_rb)]
            prods = []
            for s in range(chunk):
                cs_s = jnp.broadcast_to(cs[s:s + 1, lj], (SUBLANES, MXU_DIM))
                for tb in range(s // SUBLANES, n_rb):
                    d = bb[tb] - cs_s
                    if tb == s // SUBLANES:
                        d = jnp.where(row_8 >= (s % SUBLANES), d, _MASKED)
                    prods.append(qq[tb] * jnp.exp2(d))
                    if len(prods) == 2:
                        ti = units.index((s, tb)) // 2
                        p_scr[g, j, ti * 16:(ti + 1) * 16, :] = jnp.concatenate(prods, axis=0).astype(BF16)
                        prods = []
            a_cols = jnp.dot(p_scr[g, j], head_ones, preferred_element_type=F32)
            intra = [jnp.zeros((SUBLANES, MXU_DIM), F32) for _ in range(n_rb)]
            for s in range(chunk):
                v_s = jnp.broadcast_to(v[s:s + 1, lj], (SUBLANES, MXU_DIM))
                for tb in range(s // SUBLANES, n_rb):
                    un = units.index((s, tb))
                    intra[tb] = intra[tb] + a_cols[un * SUBLANES:(un + 1) * SUBLANES, :] * v_s
            intra = jnp.concatenate(intra, axis=0) if n_rb > 1 else intra[0]
            for h in range(j * (MXU_DIM // dk), (j + 1) * (MXU_DIM // dk)):
                ls = slice(h * dk, (h + 1) * dk)
                li = slice(h * dk - j * MXU_DIM, (h + 1) * dk - j * MXU_DIM)
                if single_chunk:
                    s_in = s0_ref[g, h]
                    inter = jnp.dot(qe[:, ls], s_in.astype(BF16), preferred_element_type=F32)
                    upd = lax.dot_general(kd[:, ls], vb[:, ls], _TN, preferred_element_type=F32)
                    e_hi = e_last[:, ls].astype(BF16)
                    rem = e_last[:, ls] - e_hi.astype(F32)
                    e_mid = rem.astype(BF16)
                    e_lo = (rem - e_mid.astype(F32)).astype(BF16)
                    pieces = jnp.where(row_h == 0, e_hi.astype(F32), jnp.where(
                        row_h == 1, e_mid.astype(F32), jnp.where(row_h == 2, e_lo.astype(F32), 0.0)))
                    decay = lax.dot_general(pieces.astype(BF16), ones_h, _TN, preferred_element_type=F32)
                    so_ref[g, h] = s_in * decay + upd
                else:
                    st = st_scr[g, h]
                    inter = lax.dot_general(qe[:, ls], st.astype(BF16), _NT, preferred_element_type=F32)
                    upd = lax.dot_general(vb[:, ls], kd[:, ls], _TN, preferred_element_type=F32)
                    decay = jnp.broadcast_to(e_last[:, ls], (SUBLANES, dk))
                    new = (st.reshape(dk // SUBLANES, SUBLANES, dk) * decay[None]
                           + upd.reshape(dk // SUBLANES, SUBLANES, dk))
                    st_scr[g, h] = new.reshape(dk, dk)
                o = intra[:, li] + inter
                outs.append(o * lax.rsqrt(jnp.mean(o * o, axis=-1, keepdims=True) + RMS_EPS))
        yo_scr[rows, :] = jnp.concatenate(outs, axis=-1) * gn * load(og_ref)

    if single_chunk:
        for g in range(n_seq):
            chunk_body(g, 0)
    else:
        def step(ci, carry):
            for g in range(n_seq):
                chunk_body(g, ci)
            return carry

        lax.fori_loop(0, t_rows // chunk, step, 0)

    yb_ref[...] = yo_scr[...].astype(BF16)

    if not single_chunk:
        @pl.when(t == n_t - 1)
        def _():
            for g in range(n_seq):
                for h in range(HG_HEADS):
                    so_ref[g, h] = st_scr[g, h].T


def _hgrn(uf, ub, s0, s0_layer, gn, l, prev, *, depth, total_rows, row0, n_batch, seq_len, n_seq, t_rows, chunk, tag):
    c = gn.shape[-1]
    dk = c // HG_HEADS
    n_t = seq_len // t_rows
    r = n_seq * t_rows
    assert seq_len % t_rows == 0 and n_batch % n_seq == 0 and row0 % r == 0 and t_rows % chunk == 0
    assert n_seq == 1 or n_t == 1
    assert t_rows != chunk or n_t == 1
    assert chunk % SUBLANES == 0 and c % MXU_DIM == 0 and MXU_DIM % dk == 0
    blk0 = row0 // r
    n_tiles = sum(chunk // SUBLANES - s // SUBLANES for s in range(chunk)) // 2
    a_specs, a_args, amap = _with_aliases(7, prev)
    body = functools.partial(_hgrn_body, n_seq=n_seq, t_rows=t_rows, n_t=n_t, chunk=chunk, n_alias=len(a_args))

    def col(cb):
        return pl.BlockSpec((r, c), lambda s, t: (blk0 + s * n_t + t, cb))

    return pl.pallas_call(
        body,
        grid=(n_batch // n_seq, n_t),
        in_specs=[
            col(UF_LOGF), col(UB_K), col(UB_Q), col(UB_V), col(UB_OG),
            pl.BlockSpec((None, n_seq, HG_HEADS, dk, dk), lambda s, t: (s0_layer, s, 0, 0, 0)),
            pl.BlockSpec((None, 1, c), lambda s, t: (l, 0, 0)),
        ] + a_specs,
        out_specs=[
            pl.BlockSpec((r, c), lambda s, t: (blk0 + s * n_t + t, 0)),
            pl.BlockSpec((None, n_seq, HG_HEADS, dk, dk), lambda s, t: (l, s, 0, 0, 0)),
        ],
        out_shape=[
            jax.ShapeDtypeStruct((total_rows, c), BF16),
            jax.ShapeDtypeStruct((depth, n_batch, HG_HEADS, dk, dk), F32),
        ],
        scratch_shapes=[
            pltpu.VMEM((n_seq, HG_HEADS, dk, dk), F32),
            pltpu.VMEM((n_seq, c // MXU_DIM, n_tiles * 16, MXU_DIM), BF16),
            pltpu.VMEM((r, c), F32),
        ],
        input_output_aliases=amap,
        compiler_params=_params(("arbitrary", "arbitrary")),
        name=f"hgrn_{tag}_l{l}",
    )(uf, ub, ub, ub, ub, s0, gn, *a_args)


def _merge_body(x_ref, ya_ref, yb_ref, ga_ref, gb_ref, wpa_ref, wpb_ref, wo_ref, g_ref, b_ref, o_ref, *, alpha):
    pa = jnp.dot(ya_ref[...], wpa_ref[...], preferred_element_type=F32)
    pb = jnp.dot(yb_ref[...], wpb_ref[...], preferred_element_type=F32)
    merged = ga_ref[...].astype(F32) * pa + gb_ref[...].astype(F32) * pb
    m = jnp.dot(merged.astype(BF16), wo_ref[...], preferred_element_type=F32)
    o_ref[...] = _layer_norm(alpha * x_ref[...] + m, g_ref[...], b_ref[...])


def _merge(x, ya, yb, ub, wpa_b, wpb_b, wo_b, ln_g, ln_b, l, ln_idx, alpha):
    m, d = x.shape
    c = ya.shape[1]
    tm = TM_MERGE
    assert m % tm == 0
    body = functools.partial(_merge_body, alpha=alpha)
    once = pl.Buffered(1)
    return pl.pallas_call(
        body,
        grid=(m // tm,),
        in_specs=[
            pl.BlockSpec((tm, d), lambda i: (i, 0)),
            pl.BlockSpec((tm, c), lambda i: (i, 0)),
            pl.BlockSpec((tm, c), lambda i: (i, 0)),
            pl.BlockSpec((tm, d), lambda i: (i, UB_SGA)),
            pl.BlockSpec((tm, d), lambda i: (i, UB_SGB)),
            pl.BlockSpec((None, c, d), lambda i: (l, 0, 0), pipeline_mode=once),
            pl.BlockSpec((None, c, d), lambda i: (l, 0, 0), pipeline_mode=once),
            pl.BlockSpec((None, d, d), lambda i: (l, 0, 0), pipeline_mode=once),
            pl.BlockSpec((None, 1, d), lambda i: (ln_idx, 0, 0)),
            pl.BlockSpec((None, 1, d), lambda i: (ln_idx, 0, 0)),
        ],
        out_specs=pl.BlockSpec((tm, d), lambda i: (i, 0)),
        out_shape=jax.ShapeDtypeStruct((m, d), F32),
        compiler_params=_params(("parallel",)),
        name=f"merge_l{l}",
    )(x, ya, yb, ub, ub, wpa_b, wpb_b, wo_b, ln_g, ln_b)


EXTRACT_ROWS = 1024


def _copy_body(x_ref, o_ref):
    o_ref[...] = x_ref[...]


def _extract_prompt(x, batch, p_len, n_meta):
    d = x.shape[1]
    seq = p_len - n_meta
    tr = EXTRACT_ROWS
    assert seq % tr == 0 and p_len % SUBLANES == 0 and n_meta % SUBLANES == 0 and tr % SUBLANES == 0
    return pl.pallas_call(
        _copy_body,
        grid=(batch, seq // tr),
        in_specs=[pl.BlockSpec(
            (pl.Element(tr), pl.Element(d)),
            lambda b, h: ((b * (p_len // SUBLANES) + n_meta // SUBLANES + h * (tr // SUBLANES)) * SUBLANES, 0))],
        out_specs=pl.BlockSpec((None, tr, d), lambda b, h: (b, h, 0)),
        out_shape=jax.ShapeDtypeStruct((batch, seq, d), x.dtype),
        compiler_params=_params(("parallel", "parallel")),
        name="extract_prompt",
    )(x)


def kernel(x_prompt, x_sample, state_rglru_h, state_rglru_conv, state_hgrn, meta_tokens, ln_g, ln_b, ffn_w_in, ffn_w_out, w_in, conv_w, conv_b, rg_wa, rg_ba, rg_wi, rg_bi, rg_lambda, hg_lb, hg_norm_g, w_pa, w_pb, w_o):
    batch, seq, d_model = x_prompt.shape
    dec_batch, dec_seq, _ = x_sample.shape
    depth = w_in.shape[0]
    n_meta = meta_tokens.shape[0]
    d_rnn = conv_w.shape[-1]
    dk = d_rnn // HG_HEADS
    d_ff = ffn_w_out.shape[-2]
    alpha = (2.0 * depth) ** 0.25
    dt = x_prompt.dtype

    p_len = n_meta + seq
    p_rows = batch * p_len
    total_rows = p_rows + dec_batch * dec_seq
    meta = jnp.broadcast_to(meta_tokens.astype(dt)[None], (batch, n_meta, d_model))
    x = jnp.concatenate(
        [jnp.concatenate([meta, x_prompt], axis=1).reshape(p_rows, d_model),
         x_sample.reshape(dec_batch * dec_seq, d_model)], axis=0)

    w_in2 = ffn_w_in.reshape(depth * 2 * d_model, 2 * d_ff)
    w_out2 = ffn_w_out.reshape(depth * 2 * d_ff, d_model)
    wpa_b, wpb_b, wo_b = w_pa.astype(BF16), w_pb.astype(BF16), w_o.astype(BF16)
    wa_b, wi_b = rg_wa.astype(BF16), rg_wi.astype(BF16)
    ln_g3 = ln_g.reshape(depth * 3, 1, d_model)
    ln_b3 = ln_b.reshape(depth * 3, 1, d_model)
    vec = lambda a: a.reshape(depth, 1, a.shape[-1])
    conv_b3, ba3, bi3, lam3, gn3 = vec(conv_b), vec(rg_ba), vec(rg_bi), vec(rg_lambda), vec(hg_norm_g)

    zeros_h = jnp.zeros((batch, 1, d_rnn), dt)
    zeros_c = jnp.zeros((batch, CONV_W - 1, d_rnn), dt)
    zeros_s = jnp.zeros((1, batch, HG_HEADS, dk, dk), dt)

    common = dict(depth=depth, total_rows=total_rows)
    prompt = dict(row0=0, n_batch=batch, seq_len=p_len, n_seq=1, t_rows=PROMPT_TILE_ROWS, tag="p", **common)
    sample = dict(row0=p_rows, n_batch=dec_batch, seq_len=dec_seq, n_seq=SAMPLE_SEQS_PER_STEP, t_rows=dec_seq,
                  tag="s", **common)

    ph = pc = ps = sh = sc = ss = None
    for l in range(depth):
        x, xb = _ffn(x, w_in2, w_out2, ln_g3, ln_b3, 2 * l, 3 * l, alpha, emit_bf16=True)
        uf, ub = _proj(xb, w_in, hg_lb, l)
        rg = (conv_w, conv_b3, wa_b, ba3, wi_b, bi3, lam3, l)
        ya, ph, pc = _rglru(uf, ub, zeros_c, zeros_h, *rg, (None, ph, pc), **prompt)
        ya, sh, sc = _rglru(uf, ub, state_rglru_conv[l], state_rglru_h[l][:, None, :], *rg, (ya, sh, sc), **sample)
        yb, ps = _hgrn(uf, ub, zeros_s, 0, gn3, l, (None, ps), chunk=CHUNK, **prompt)
        yb, ss = _hgrn(uf, ub, state_hgrn, l, gn3, l, (yb, ss), chunk=dec_seq, **sample)
        x = _merge(x, ya, yb, ub, wpa_b, wpb_b, wo_b, ln_g3, ln_b3, l, 3 * l + 1, alpha)
        (x,) = _ffn(x, w_in2, w_out2, ln_g3, ln_b3, 2 * l + 1, 3 * l + 2, alpha)

    y_prompt = _extract_prompt(x, batch, p_len, n_meta)
    y_sample = x[p_rows:].reshape(dec_batch, dec_seq, d_model)
    return (y_prompt, y_sample, ph[:, :, 0], pc, ps, sh[:, :, 0], sc, ss)
```

```python
import functools

import jax
import jax.numpy as jnp
from jax import lax
from jax.experimental import pallas as pl
from jax.experimental.pallas import tpu as pltpu

F32 = jnp.float32
BF16 = jnp.bfloat16

CONV_W = 4
C_RG = 8.0
RG_BLOCKS = 8
HG_HEADS = 8
CHUNK = 16
LN_EPS = 1e-5
RMS_EPS = 1e-6

LANES = 128
SUBLANES = 8
MXU_DIM = 256
VMEM_LIMIT_BYTES = 62 * 1024 * 1024

TM_FFN = 928
TF_FFN = 256
LN_ROWS = 464
TM_PROJ = 1160
TN_PROJ = 1024
TM_MERGE = 464
EXTRACT_ROWS = 1024
PROMPT_TILE_ROWS = 688
SAMPLE_SEQS_PER_STEP = 8

_NT = (((1,), (1,)), ((), ()))
_TN = (((0,), (0,)), ((), ()))
_MASKED = -1e30
LOG2_E = 1.4426950408889634


def _params(semantics):
    return pltpu.CompilerParams(dimension_semantics=semantics, vmem_limit_bytes=VMEM_LIMIT_BYTES)


def _layer_norm(y, g, b):
    mu = jnp.mean(y, axis=-1, keepdims=True)
    d = y - mu
    var = jnp.mean(d * d, axis=-1, keepdims=True)
    return d * lax.rsqrt(var + LN_EPS) * g + b


_ANY = pl.BlockSpec(memory_space=pl.ANY)


def _ffn_body(x_ref, wg_ref, wu_ref, wo_ref, g_ref, b_ref, o_ref, *rest, tf, n_f, f_dim, alpha):
    xb_ref = rest[-1]
    ob_ref = rest[0] if len(rest) == 2 else None
    j = pl.program_id(1)
    tm = o_ref.shape[0]
    overlap = n_f * tf - f_dim

    def contribution(xb, drop_cols):
        gate = jnp.dot(xb, wg_ref[...].astype(BF16), preferred_element_type=F32)
        up = jnp.dot(xb, wu_ref[...].astype(BF16), preferred_element_type=F32)
        h = (0.5 * gate) * jax.nn.sigmoid(gate) * up
        if drop_cols:
            col = lax.broadcasted_iota(jnp.int32, (1, tf), 1)
            h = jnp.where(col >= drop_cols, h, 0.0)
        return jnp.dot(h.astype(BF16), wo_ref[...].astype(BF16), preferred_element_type=F32)

    @pl.when(j == 0)
    def _():
        xb = x_ref[...].astype(BF16)
        xb_ref[...] = xb
        o_ref[...] = alpha * x_ref[...] + contribution(xb, 0)

    @pl.when((j > 0) & (j < n_f - 1))
    def _():
        o_ref[...] += contribution(xb_ref[...], 0)

    @pl.when(j == n_f - 1)
    def _():
        for r0 in range(0, tm, LN_ROWS):
            r = slice(r0, r0 + LN_ROWS)
            y = o_ref[r, :] + contribution(xb_ref[r, :], overlap)
            o_ref[r, :] = _layer_norm(y, g_ref[...], b_ref[...])
        if ob_ref is not None:
            ob_ref[...] = o_ref[...].astype(BF16)


def _ffn(x, w_in2, w_out2, ln_g, ln_b, wsel, ln_idx, alpha, emit_bf16=False):
    m, d = x.shape
    f_dim = w_in2.shape[1] // 2
    tf, tm = TF_FFN, TM_FFN
    n_f = -(-f_dim // tf)
    assert m % tm == 0 and tm % LN_ROWS == 0 and f_dim % LANES == 0 and tf % LANES == 0 and f_dim >= tf
    assert (n_f * tf - f_dim) < tf and d % SUBLANES == 0 and f_dim % SUBLANES == 0
    last = (f_dim - tf) // LANES
    step = tf // LANES

    def chunk(j):
        return jnp.minimum(j * step, last)

    body = functools.partial(_ffn_body, tf=tf, n_f=n_f, f_dim=f_dim, alpha=alpha)
    return pl.pallas_call(
        body,
        grid=(m // tm, n_f),
        in_specs=[
            pl.BlockSpec((tm, d), lambda i, j: (i, 0)),
            pl.BlockSpec((pl.Element(d), pl.Element(tf)), lambda i, j: (wsel * d, chunk(j) * LANES)),
            pl.BlockSpec((pl.Element(d), pl.Element(tf)),
                         lambda i, j: (wsel * d, (f_dim // LANES + chunk(j)) * LANES)),
            pl.BlockSpec((pl.Element(tf), pl.Element(d)),
                         lambda i, j: ((wsel * (f_dim // SUBLANES) + chunk(j) * (LANES // SUBLANES)) * SUBLANES, 0)),
            pl.BlockSpec((None, 1, d), lambda i, j: (ln_idx, 0, 0)),
            pl.BlockSpec((None, 1, d), lambda i, j: (ln_idx, 0, 0)),
        ],
        out_specs=[pl.BlockSpec((tm, d), lambda i, j: (i, 0))] * (2 if emit_bf16 else 1),
        out_shape=[jax.ShapeDtypeStruct((m, d), F32)] + ([jax.ShapeDtypeStruct((m, d), BF16)] if emit_bf16 else []),
        scratch_shapes=[pltpu.VMEM((tm, d), BF16)],
        compiler_params=_params(("parallel", "arbitrary")),
        name=f"ffn_{wsel}",
    )(x, w_in2, w_in2, w_out2, ln_g, ln_b)


PROJ_ORDER = (6, 7, 8, 9, 3, 0, 1, 2, 4, 5)
UF_LOGF, UF_XR = 0, 1
UB_SGA, UB_SGB, UB_K, UB_GR, UB_Q, UB_V, UB_OG = 0, 1, 4, 5, 6, 7, 8


def _forget_lower_bound(hg_lb, layer):
    e = jnp.exp(hg_lb - jnp.max(hg_lb, axis=0, keepdims=True))
    sm = e / jnp.sum(e, axis=0, keepdims=True)
    cum = sm[0:1, :]
    for m in range(1, layer + 1):
        cum = cum + sm[m:m + 1, :]
    return cum - sm[0:1, :]


def _proj_body(xb_ref, w_ref, lb_ref, uf_ref, ub_ref, *, layer):
    j = pl.program_id(1)

    def acc():
        return jnp.dot(xb_ref[...], w_ref[...].astype(BF16), preferred_element_type=F32)

    col = [PROJ_ORDER.index(cb) for cb in range(len(PROJ_ORDER))]

    @pl.when(j <= col[9])
    def _():
        ub_ref[...] = jax.nn.sigmoid(acc()).astype(BF16)

    @pl.when(j == col[3])
    def _():
        lb = _forget_lower_bound(lb_ref[...], layer)
        f = lb + (1.0 - lb) * jax.nn.sigmoid(acc())
        uf_ref[...] = jnp.log(f)
        ub_ref[...] = (1.0 - f).astype(BF16)

    @pl.when(j == col[0])
    def _():
        uf_ref[...] = acc()

    @pl.when(j == col[1])
    def _():
        ub_ref[...] = jax.nn.gelu(acc()).astype(BF16)

    @pl.when((j == col[2]) | (j == col[4]))
    def _():
        ub_ref[...] = acc().astype(BF16)

    @pl.when(j == col[5])
    def _():
        a = acc()
        ub_ref[...] = (a * jax.nn.sigmoid(a)).astype(BF16)


def _proj(xb, w, hg_lb, l):
    m, d = xb.shape
    tm, tn = TM_PROJ, TN_PROJ
    n_blk = w.shape[-1] // tn
    assert m % tm == 0 and w.shape[-1] % tn == 0 and n_blk == len(PROJ_ORDER)
    assert PROJ_ORDER[:4] == (6, 7, 8, 9) and PROJ_ORDER[4:6] == (3, 0)

    def wcol(j):
        cb = jnp.int32(PROJ_ORDER[-1])
        for step in range(n_blk - 1):
            cb = jnp.where(j == step, PROJ_ORDER[step], cb)
        return cb

    body = functools.partial(_proj_body, layer=l)
    return pl.pallas_call(
        body,
        grid=(m // tm, n_blk),
        in_specs=[
            pl.BlockSpec((tm, d), lambda i, j: (i, 0)),
            pl.BlockSpec((None, d, tn), lambda i, j: (l, 0, wcol(j))),
            pl.BlockSpec(hg_lb.shape, lambda i, j: (0, 0)),
        ],
        out_specs=[
            pl.BlockSpec((tm, tn), lambda i, j: (i, jnp.where(j >= 5, 1, 0))),
            pl.BlockSpec((tm, tn), lambda i, j: (i, jnp.where(j >= 5, j - 1, j))),
        ],
        out_shape=[jax.ShapeDtypeStruct((m, 2 * tn), F32), jax.ShapeDtypeStruct((m, (n_blk - 1) * tn), BF16)],
        compiler_params=_params(("parallel", "arbitrary")),
        name=f"proj_l{l}",
    )(xb, w, hg_lb)


def _rglru_body(*refs, n_seq, t_rows, n_t, n_alias):
    xr_ref, gr_ref, c0_ref, h0_ref, cw_ref, cb_ref, wa_ref, ba_ref, wi_ref, bi_ref, lam_ref = refs[:11]
    ya_ref, hl_ref, cn_ref, xbuf, a_scr, b_scr, hcar = refs[11 + n_alias:]
    t = pl.program_id(1)
    c = a_scr.shape[1]
    hist = CONV_W - 1
    base = SUBLANES - hist

    @pl.when(t == 0)
    def _():
        for g in range(n_seq):
            xbuf[g, base:SUBLANES, :] = c0_ref[g]
            hcar[g] = h0_ref[g]

    for g in range(n_seq):
        xbuf[g, SUBLANES:SUBLANES + t_rows, :] = xr_ref[g * t_rows:(g + 1) * t_rows, :]

    log_a_scale = -C_RG * jax.nn.softplus(-lam_ref[...])
    bs = c // RG_BLOCKS
    for n in range(RG_BLOCKS):
        ls = slice(n * bs, (n + 1) * bs)
        for g in range(n_seq):
            xc = cb_ref[:, ls] + sum(
                xbuf[g, base + j:base + j + t_rows, ls] * cw_ref[j:j + 1, ls] for j in range(CONV_W))
            a_scr[g * t_rows:(g + 1) * t_rows, ls] = xc
        xc = a_scr[:, ls]
        xcb = xc.astype(BF16)
        r = jax.nn.sigmoid(jnp.dot(xcb, wa_ref[n], preferred_element_type=F32) + ba_ref[:, ls])
        i = jax.nn.sigmoid(jnp.dot(xcb, wi_ref[n], preferred_element_type=F32) + bi_ref[:, ls])
        log_a = log_a_scale[:, ls] * r
        a = jnp.exp(log_a)
        mult = jnp.sqrt(-jnp.tanh(log_a) * (a * a + 1.0))
        a_scr[:, ls] = a
        b_scr[:, ls] = mult * (i * xc)

    for g in range(n_seq):
        xbuf[g, base:SUBLANES, :] = xbuf[g, base + t_rows:SUBLANES + t_rows, :]

    row = lax.broadcasted_iota(jnp.int32, (SUBLANES, c), 0)
    for g in range(n_seq):
        def group(j, h_in, g=g):
            r0 = pl.multiple_of(g * t_rows + j * SUBLANES, SUBLANES)
            a = a_scr[pl.ds(r0, SUBLANES), :]
            b = b_scr[pl.ds(r0, SUBLANES), :]
            for s in (1, 2, 4):
                keep = row >= s
                b = jnp.where(keep, a * pltpu.roll(b, s, 0) + b, b)
                a = jnp.where(keep, a * pltpu.roll(a, s, 0), a)
            h = a * h_in + b
            a_scr[pl.ds(r0, SUBLANES), :] = h
            return h[SUBLANES - 1:SUBLANES, :]

        hcar[g] = lax.fori_loop(0, t_rows // SUBLANES, group, hcar[g])

    for n in range(RG_BLOCKS):
        ls = slice(n * bs, (n + 1) * bs)
        ya_ref[:, ls] = (a_scr[:, ls] * gr_ref[:, ls].astype(F32)).astype(BF16)

    @pl.when(t == n_t - 1)
    def _():
        for g in range(n_seq):
            hl_ref[g] = hcar[g]
            cn_ref[g] = xbuf[g, base:SUBLANES, :]


def _with_aliases(n_in, prev):
    specs, args, amap = [], [], {}
    for out_idx, p in enumerate(prev):
        if p is not None:
            amap[n_in + len(args)] = out_idx
            specs.append(_ANY)
            args.append(p)
    return specs, args, amap


def _rglru(uf, ub, conv0, h0, conv_w, conv_b, wa_b, ba, wi_b, bi, lam, l, prev, *,
           depth, total_rows, row0, n_batch, seq_len, n_seq, t_rows, tag):
    c = conv_w.shape[-1]
    n_t = seq_len // t_rows
    r = n_seq * t_rows
    assert seq_len % t_rows == 0 and n_batch % n_seq == 0 and row0 % r == 0 and t_rows % SUBLANES == 0
    assert n_seq == 1 or n_t == 1
    blk0 = row0 // r
    hist = CONV_W - 1
    a_specs, a_args, amap = _with_aliases(11, prev)
    body = functools.partial(_rglru_body, n_seq=n_seq, t_rows=t_rows, n_t=n_t, n_alias=len(a_args))
    vec = pl.BlockSpec((None, 1, c), lambda s, t: (l, 0, 0))
    gate_w = pl.BlockSpec((None, RG_BLOCKS, c // RG_BLOCKS, c // RG_BLOCKS), lambda s, t: (l, 0, 0, 0))
    return pl.pallas_call(
        body,
        grid=(n_batch // n_seq, n_t),
        in_specs=[
            pl.BlockSpec((r, c), lambda s, t: (blk0 + s * n_t + t, UF_XR)),
            pl.BlockSpec((r, c), lambda s, t: (blk0 + s * n_t + t, UB_GR)),
            pl.BlockSpec((n_seq, hist, c), lambda s, t: (s, 0, 0)),
            pl.BlockSpec((n_seq, 1, c), lambda s, t: (s, 0, 0)),
            pl.BlockSpec((None, CONV_W, c), lambda s, t: (l, 0, 0)),
            vec, gate_w, vec, gate_w, vec, vec,
        ] + a_specs,
        out_specs=[
            pl.BlockSpec((r, c), lambda s, t: (blk0 + s * n_t + t, 0)),
            pl.BlockSpec((None, n_seq, 1, c), lambda s, t: (l, s, 0, 0)),
            pl.BlockSpec((None, n_seq, hist, c), lambda s, t: (l, s, 0, 0)),
        ],
        out_shape=[
            jax.ShapeDtypeStruct((total_rows, c), BF16),
            jax.ShapeDtypeStruct((depth, n_batch, 1, c), F32),
            jax.ShapeDtypeStruct((depth, n_batch, hist, c), F32),
        ],
        scratch_shapes=[
            pltpu.VMEM((n_seq, SUBLANES + t_rows, c), F32),
            pltpu.VMEM((r, c), F32),
            pltpu.VMEM((r, c), F32),
            pltpu.VMEM((n_seq, 1, c), F32),
        ],
        input_output_aliases=amap,
        compiler_params=_params(("arbitrary", "arbitrary")),
        name=f"rglru_{tag}_l{l}",
    )(uf, ub, conv0, h0, conv_w, conv_b, wa_b, ba, wi_b, bi, lam, *a_args)


def _hgrn_body(*refs, n_seq, t_rows, n_t, chunk, n_alias):
    lf_ref, k_ref, q_ref, v_ref, og_ref, s0_ref, gn_ref = refs[:7]
    yb_ref, so_ref, st_scr, p_scr, yo_scr = refs[7 + n_alias:]
    t = pl.program_id(1)
    c = q_ref.shape[1]
    dk = c // HG_HEADS
    n_lane_blk = c // MXU_DIM
    n_rb = chunk // SUBLANES
    n_units = sum(n_rb - s // SUBLANES for s in range(chunk))
    assert n_units % 2 == 0
    n_tiles = n_units // 2

    single_chunk = t_rows == chunk

    if not single_chunk:
        @pl.when(t == 0)
        def _():
            for g in range(n_seq):
                for h in range(HG_HEADS):
                    st_scr[g, h] = s0_ref[g, h].T

    gn = gn_ref[...]

    row_c = lax.broadcasted_iota(jnp.int32, (chunk, c), 0)
    row_8 = lax.broadcasted_iota(jnp.int32, (SUBLANES, MXU_DIM), 0)
    shifts = [s for s in (1, 2, 4, 8, 16) if s < chunk]
    ri = lax.broadcasted_iota(jnp.int32, (MXU_DIM, MXU_DIM), 0)
    ci_ = lax.broadcasted_iota(jnp.int32, (MXU_DIM, MXU_DIM), 1)
    head_ones = ((ri // dk) == (ci_ // dk)).astype(BF16)

    packed_rows = 2 * SUBLANES

    row_h = lax.broadcasted_iota(jnp.int32, (SUBLANES, dk), 0)
    ones_h = jnp.ones((SUBLANES, dk), BF16)

    def chunk_body(g, ci):
        r0 = g * t_rows if single_chunk else pl.multiple_of(g * t_rows + ci * chunk, chunk)
        rows = pl.ds(r0, chunk)

        def load(ref):
            if chunk % packed_rows == 0:
                return ref[rows, :].astype(F32)
            tile0 = (r0 // packed_rows) * packed_rows
            return ref[tile0:tile0 + packed_rows, :].astype(F32)[r0 - tile0:r0 - tile0 + chunk, :]

        b = lf_ref[rows, :] * LOG2_E
        for s in shifts:
            b = b + jnp.where(row_c >= s, pltpu.roll(b, s, 0), 0.0)
        k = load(k_ref)
        q = load(q_ref)
        v = load(v_ref)

        cs = b - jnp.log2(k)
        b_last = b[chunk - 1:chunk, :]
        qe = (q * jnp.exp2(b)).astype(BF16)
        kd = (k * jnp.exp2(b_last - b)).astype(BF16)
        e_last = jnp.exp2(b_last)
        vb = v.astype(BF16)
        units = [(s, tb) for s in range(chunk) for tb in range(s // SUBLANES, n_rb)]

        outs = []
        for j in range(n_lane_blk):
            lj = slice(j * MXU_DIM, (j + 1) * MXU_DIM)
            bb = [b[i * SUBLANES:(i + 1) * SUBLANES, lj] for i in range(n_rb)]
            qq = [q[i * SUBLANES:(i + 1) * SUBLANES, lj] for i in range(n_rb)]
            prods = []
            for s in range(chunk):
                cs_s = jnp.broadcast_to(cs[s:s + 1, lj], (SUBLANES, MXU_DIM))
                for tb in range(s // SUBLANES, n_rb):
                    d = bb[tb] - cs_s
                    if tb == s // SUBLANES:
                        d = jnp.where(row_8 >= (s % SUBLANES), d, _MASKED)
                    prods.append(qq[tb] * jnp.exp2(d))
                    if len(prods) == 2:
                        ti = units.index((s, tb)) // 2
                        p_scr[g, j, ti * 16:(ti + 1) * 16, :] = jnp.concatenate(prods, axis=0).astype(BF16)
                        prods = []
            a_cols = jnp.dot(p_scr[g, j], head_ones, preferred_element_type=F32)
            intra = [jnp.zeros((SUBLANES, MXU_DIM), F32) for _ in range(n_rb)]
            for s in range(chunk):
                v_s = jnp.broadcast_to(v[s:s + 1, lj], (SUBLANES, MXU_DIM))
                for tb in range(s // SUBLANES, n_rb):
                    un = units.index((s, tb))
                    intra[tb] = intra[tb] + a_cols[un * SUBLANES:(un + 1) * SUBLANES, :] * v_s
            intra = jnp.concatenate(intra, axis=0) if n_rb > 1 else intra[0]
            for h in range(j * (MXU_DIM // dk), (j + 1) * (MXU_DIM // dk)):
                ls = slice(h * dk, (h + 1) * dk)
                li = slice(h * dk - j * MXU_DIM, (h + 1) * dk - j * MXU_DIM)
                if single_chunk:
                    s_in = s0_ref[g, h]
                    inter = jnp.dot(qe[:, ls], s_in.astype(BF16), preferred_element_type=F32)
                    upd = lax.dot_general(kd[:, ls], vb[:, ls], _TN, preferred_element_type=F32)
                    e_hi = e_last[:, ls].astype(BF16)
                    rem = e_last[:, ls] - e_hi.astype(F32)
                    e_mid = rem.astype(BF16)
                    e_lo = (rem - e_mid.astype(F32)).astype(BF16)
                    pieces = jnp.where(row_h == 0, e_hi.astype(F32), jnp.where(
                        row_h == 1, e_mid.astype(F32), jnp.where(row_h == 2, e_lo.astype(F32), 0.0)))
                    decay = lax.dot_general(pieces.astype(BF16), ones_h, _TN, preferred_element_type=F32)
                    so_ref[g, h] = s_in * decay + upd
                else:
                    st = st_scr[g, h]
                    inter = lax.dot_general(qe[:, ls], st.astype(BF16), _NT, preferred_element_type=F32)
                    upd = lax.dot_general(vb[:, ls], kd[:, ls], _TN, preferred_element_type=F32)
                    decay = jnp.broadcast_to(e_last[:, ls], (SUBLANES, dk))
                    new = (st.reshape(dk // SUBLANES, SUBLANES, dk) * decay[None]
                           + upd.reshape(dk // SUBLANES, SUBLANES, dk))
                    st_scr[g, h] = new.reshape(dk, dk)
                o = intra[:, li] + inter
                outs.append(o * lax.rsqrt(jnp.mean(o * o, axis=-1, keepdims=True) + RMS_EPS))
        yo_scr[rows, :] = jnp.concatenate(outs, axis=-1) * gn * load(og_ref)

    if single_chunk:
        for g in range(n_seq):
            chunk_body(g, 0)
    else:
        def step(ci, carry):
            for g in range(n_seq):
                chunk_body(g, ci)
            return carry

        lax.fori_loop(0, t_rows // chunk, step, 0)

    yb_ref[...] = yo_scr[...].astype(BF16)

    if not single_chunk:
        @pl.when(t == n_t - 1)
        def _():
            for g in range(n_seq):
                for h in range(HG_HEADS):
                    so_ref[g, h] = st_scr[g, h].T


def _hgrn(uf, ub, s0, s0_layer, gn, l, prev, *, depth, total_rows, row0, n_batch, seq_len, n_seq, t_rows, chunk, tag):
    c = gn.shape[-1]
    dk = c // HG_HEADS
    n_t = seq_len // t_rows
    r = n_seq * t_rows
    assert seq_len % t_rows == 0 and n_batch % n_seq == 0 and row0 % r == 0 and t_rows % chunk == 0
    assert n_seq == 1 or n_t == 1
    assert t_rows != chunk or n_t == 1
    assert chunk % SUBLANES == 0 and c % MXU_DIM == 0 and MXU_DIM % dk == 0
    blk0 = row0 // r
    n_tiles = sum(chunk // SUBLANES - s // SUBLANES for s in range(chunk)) // 2
    a_specs, a_args, amap = _with_aliases(7, prev)
    body = functools.partial(_hgrn_body, n_seq=n_seq, t_rows=t_rows, n_t=n_t, chunk=chunk, n_alias=len(a_args))

    def col(cb):
        return pl.BlockSpec((r, c), lambda s, t: (blk0 + s * n_t + t, cb))

    return pl.pallas_call(
        body,
        grid=(n_batch // n_seq, n_t),
        in_specs=[
            col(UF_LOGF), col(UB_K), col(UB_Q), col(UB_V), col(UB_OG),
            pl.BlockSpec((None, n_seq, HG_HEADS, dk, dk), lambda s, t: (s0_layer, s, 0, 0, 0)),
            pl.BlockSpec((None, 1, c), lambda s, t: (l, 0, 0)),
        ] + a_specs,
        out_specs=[
            pl.BlockSpec((r, c), lambda s, t: (blk0 + s * n_t + t, 0)),
            pl.BlockSpec((None, n_seq, HG_HEADS, dk, dk), lambda s, t: (l, s, 0, 0, 0)),
        ],
        out_shape=[
            jax.ShapeDtypeStruct((total_rows, c), BF16),
            jax.ShapeDtypeStruct((depth, n_batch, HG_HEADS, dk, dk), F32),
        ],
        scratch_shapes=[
            pltpu.VMEM((n_seq, HG_HEADS, dk, dk), F32),
            pltpu.VMEM((n_seq, c // MXU_DIM, n_tiles * 16, MXU_DIM), BF16),
            pltpu.VMEM((r, c), F32),
        ],
        input_output_aliases=amap,
        compiler_params=_params(("arbitrary", "arbitrary")),
        name=f"hgrn_{tag}_l{l}",
    )(uf, ub, ub, ub, ub, s0, gn, *a_args)


def _merge_body(x_ref, ya_ref, yb_ref, ga_ref, gb_ref, wpa_ref, wpb_ref, wo_ref, g_ref, b_ref, o_ref, *, alpha):
    tm = o_ref.shape[0]
    packed = 2 * SUBLANES
    split = (tm // 2) // packed * packed
    for r in (slice(0, split), slice(split, tm)):
        pa = jnp.dot(ya_ref[r, :], wpa_ref[...], preferred_element_type=F32)
        pb = jnp.dot(yb_ref[r, :], wpb_ref[...], preferred_element_type=F32)
        merged = ga_ref[r, :].astype(F32) * pa + gb_ref[r, :].astype(F32) * pb
        m = jnp.dot(merged.astype(BF16), wo_ref[...], preferred_element_type=F32)
        o_ref[r, :] = _layer_norm(alpha * x_ref[r, :] + m, g_ref[...], b_ref[...])


def _merge(x, ya, yb, ub, wpa_b, wpb_b, wo_b, ln_g, ln_b, l, ln_idx, alpha):
    m, d = x.shape
    c = ya.shape[1]
    tm = TM_MERGE
    assert m % tm == 0
    body = functools.partial(_merge_body, alpha=alpha)
    once = pl.Buffered(1)
    return pl.pallas_call(
        body,
        grid=(m // tm,),
        in_specs=[
            pl.BlockSpec((tm, d), lambda i: (i, 0)),
            pl.BlockSpec((tm, c), lambda i: (i, 0)),
            pl.BlockSpec((tm, c), lambda i: (i, 0)),
            pl.BlockSpec((tm, d), lambda i: (i, UB_SGA)),
            pl.BlockSpec((tm, d), lambda i: (i, UB_SGB)),
            pl.BlockSpec((None, c, d), lambda i: (l, 0, 0), pipeline_mode=once),
            pl.BlockSpec((None, c, d), lambda i: (l, 0, 0), pipeline_mode=once),
            pl.BlockSpec((None, d, d), lambda i: (l, 0, 0), pipeline_mode=once),
            pl.BlockSpec((None, 1, d), lambda i: (ln_idx, 0, 0)),
            pl.BlockSpec((None, 1, d), lambda i: (ln_idx, 0, 0)),
        ],
        out_specs=pl.BlockSpec((tm, d), lambda i: (i, 0)),
        out_shape=jax.ShapeDtypeStruct((m, d), F32),
        compiler_params=_params(("parallel",)),
        name=f"merge_l{l}",
    )(x, ya, yb, ub, ub, wpa_b, wpb_b, wo_b, ln_g, ln_b)


def _assemble_body(xp_ref, xs_ref, o_ref, *, n_prompt_blocks):
    i = pl.program_id(0)

    @pl.when(i < n_prompt_blocks)
    def _():
        o_ref[...] = xp_ref[...]

    @pl.when(i >= n_prompt_blocks)
    def _():
        o_ref[...] = xs_ref[...]


def _meta_body(m_ref, prev_ref, o_ref):
    del prev_ref
    o_ref[...] = m_ref[...]


def _assemble(x_prompt, x_sample2, meta):
    batch, seq, d = x_prompt.shape
    n_meta = meta.shape[0]
    p_len = n_meta + seq
    s_rows = x_sample2.shape[0]
    tr = EXTRACT_ROWS
    per_seq = seq // tr
    n_pb = batch * per_seq
    assert seq % tr == 0 and s_rows % tr == 0 and p_len % SUBLANES == 0 and n_meta % SUBLANES == 0
    total = batch * p_len + s_rows

    def out_row(i):
        ip = jnp.minimum(i, n_pb - 1)
        prompt = (ip // per_seq) * (p_len // SUBLANES) + n_meta // SUBLANES + (ip % per_seq) * (tr // SUBLANES)
        sample = (batch * p_len) // SUBLANES + (i - n_pb) * (tr // SUBLANES)
        return jnp.where(i < n_pb, prompt, sample) * SUBLANES

    x = pl.pallas_call(
        functools.partial(_assemble_body, n_prompt_blocks=n_pb),
        grid=(n_pb + s_rows // tr,),
        in_specs=[
            pl.BlockSpec((None, tr, d), lambda i: (jnp.minimum(i, n_pb - 1) // per_seq,
                                                   jnp.minimum(i, n_pb - 1) % per_seq, 0)),
            pl.BlockSpec((tr, d), lambda i: (jnp.maximum(i - n_pb, 0), 0)),
        ],
        out_specs=pl.BlockSpec((pl.Element(tr), pl.Element(d)), lambda i: (out_row(i), 0)),
        out_shape=jax.ShapeDtypeStruct((total, d), x_prompt.dtype),
        compiler_params=_params(("arbitrary",)),
        name="assemble",
    )(x_prompt, x_sample2)
    return pl.pallas_call(
        _meta_body,
        grid=(batch,),
        in_specs=[pl.BlockSpec((n_meta, d), lambda b: (0, 0)), _ANY],
        out_specs=pl.BlockSpec((pl.Element(n_meta), pl.Element(d)), lambda b: (b * (p_len // SUBLANES) * SUBLANES, 0)),
        out_shape=jax.ShapeDtypeStruct((total, d), x_prompt.dtype),
        input_output_aliases={1: 0},
        compiler_params=_params(("arbitrary",)),
        name="assemble_meta",
    )(meta, x)


def _copy_body(x_ref, o_ref):
    o_ref[...] = x_ref[...]


def _extract_prompt(x, batch, p_len, n_meta):
    d = x.shape[1]
    seq = p_len - n_meta
    tr = EXTRACT_ROWS
    assert seq % tr == 0 and p_len % SUBLANES == 0 and n_meta % SUBLANES == 0 and tr % SUBLANES == 0
    return pl.pallas_call(
        _copy_body,
        grid=(batch, seq // tr),
        in_specs=[pl.BlockSpec(
            (pl.Element(tr), pl.Element(d)),
            lambda b, h: ((b * (p_len // SUBLANES) + n_meta // SUBLANES + h * (tr // SUBLANES)) * SUBLANES, 0))],
        out_specs=pl.BlockSpec((None, tr, d), lambda b, h: (b, h, 0)),
        out_shape=jax.ShapeDtypeStruct((batch, seq, d), x.dtype),
        compiler_params=_params(("parallel", "parallel")),
        name="extract_prompt",
    )(x)


def kernel(x_prompt, x_sample, state_rglru_h, state_rglru_conv, state_hgrn, meta_tokens, ln_g, ln_b, ffn_w_in, ffn_w_out, w_in, conv_w, conv_b, rg_wa, rg_ba, rg_wi, rg_bi, rg_lambda, hg_lb, hg_norm_g, w_pa, w_pb, w_o):
    batch, seq, d_model = x_prompt.shape
    dec_batch, dec_seq, _ = x_sample.shape
    depth = w_in.shape[0]
    n_meta = meta_tokens.shape[0]
    d_rnn = conv_w.shape[-1]
    dk = d_rnn // HG_HEADS
    d_ff = ffn_w_out.shape[-2]
    alpha = (2.0 * depth) ** 0.25
    dt = x_prompt.dtype

    p_len = n_meta + seq
    p_rows = batch * p_len
    total_rows = p_rows + dec_batch * dec_seq
    x = _assemble(x_prompt, x_sample.reshape(dec_batch * dec_seq, d_model), meta_tokens.astype(dt))

    w_in2 = ffn_w_in.reshape(depth * 2 * d_model, 2 * d_ff)
    w_out2 = ffn_w_out.reshape(depth * 2 * d_ff, d_model)
    wpa_b, wpb_b, wo_b = w_pa.astype(BF16), w_pb.astype(BF16), w_o.astype(BF16)
    wa_b, wi_b = rg_wa.astype(BF16), rg_wi.astype(BF16)
    ln_g3 = ln_g.reshape(depth * 3, 1, d_model)
    ln_b3 = ln_b.reshape(depth * 3, 1, d_model)
    vec = lambda a: a.reshape(depth, 1, a.shape[-1])
    conv_b3, ba3, bi3, lam3, gn3 = vec(conv_b), vec(rg_ba), vec(rg_bi), vec(rg_lambda), vec(hg_norm_g)

    zeros_h = jnp.zeros((batch, 1, d_rnn), dt)
    zeros_c = jnp.zeros((batch, CONV_W - 1, d_rnn), dt)
    zeros_s = jnp.zeros((1, batch, HG_HEADS, dk, dk), dt)

    common = dict(depth=depth, total_rows=total_rows)
    prompt = dict(row0=0, n_batch=batch, seq_len=p_len, n_seq=1, t_rows=PROMPT_TILE_ROWS, tag="p", **common)
    sample = dict(row0=p_rows, n_batch=dec_batch, seq_len=dec_seq, n_seq=SAMPLE_SEQS_PER_STEP, t_rows=dec_seq,
                  tag="s", **common)

    ph = pc = ps = sh = sc = ss = None
    for l in range(depth):
        x, xb = _ffn(x, w_in2, w_out2, ln_g3, ln_b3, 2 * l, 3 * l, alpha, emit_bf16=True)
        uf, ub = _proj(xb, w_in, hg_lb, l)
        rg = (conv_w, conv_b3, wa_b, ba3, wi_b, bi3, lam3, l)
        ya, ph, pc = _rglru(uf, ub, zeros_c, zeros_h, *rg, (None, ph, pc), **prompt)
        ya, sh, sc = _rglru(uf, ub, state_rglru_conv[l], state_rglru_h[l][:, None, :], *rg, (ya, sh, sc), **sample)
        yb, ps = _hgrn(uf, ub, zeros_s, 0, gn3, l, (None, ps), chunk=CHUNK, **prompt)
        yb, ss = _hgrn(uf, ub, state_hgrn, l, gn3, l, (yb, ss), chunk=dec_seq, **sample)
        x = _merge(x, ya, yb, ub, wpa_b, wpb_b, wo_b, ln_g3, ln_b3, l, 3 * l + 1, alpha)
        (x,) = _ffn(x, w_in2, w_out2, ln_g3, ln_b3, 2 * l + 1, 3 * l + 2, alpha)

    y_prompt = _extract_prompt(x, batch, p_len, n_meta)
    y_sample = x[p_rows:].reshape(dec_batch, dec_seq, d_model)
    return (y_prompt, y_sample, ph[:, :, 0], pc, ps, sh[:, :, 0], sc, ss)
```

```python
import functools

import jax
import jax.numpy as jnp
from jax import lax
from jax.experimental import pallas as pl
from jax.experimental.pallas import tpu as pltpu

F32 = jnp.float32
BF16 = jnp.bfloat16

CONV_W = 4
C_RG = 8.0
RG_BLOCKS = 8
HG_HEADS = 8
CHUNK = 16
LN_EPS = 1e-5
RMS_EPS = 1e-6

LANES = 128
SUBLANES = 8
MXU_DIM = 256
VMEM_LIMIT_BYTES = 62 * 1024 * 1024

TM_FFN = 928
TF_FFN = 512
LN_ROWS = 464
TM_PROJ = 1160
TN_PROJ = 1024
TM_MERGE = 464
EXTRACT_ROWS = 1024
PROMPT_TILE_ROWS = 688
SAMPLE_SEQS_PER_STEP = 8

_NT = (((1,), (1,)), ((), ()))
_TN = (((0,), (0,)), ((), ()))
_MASKED = -1e30
LOG2_E = 1.4426950408889634


def _params(semantics):
    return pltpu.CompilerParams(dimension_semantics=semantics, vmem_limit_bytes=VMEM_LIMIT_BYTES)


def _layer_norm(y, g, b):
    mu = jnp.mean(y, axis=-1, keepdims=True)
    d = y - mu
    var = jnp.mean(d * d, axis=-1, keepdims=True)
    return d * lax.rsqrt(var + LN_EPS) * g + b


_ANY = pl.BlockSpec(memory_space=pl.ANY)


def _ffn_body(x_ref, wg_ref, wu_ref, wo_ref, g_ref, b_ref, o_ref, *rest, tf, n_f, f_dim, alpha):
    xb_ref = rest[-1]
    ob_ref = rest[0] if len(rest) == 2 else None
    j = pl.program_id(1)
    tm = o_ref.shape[0]
    overlap = n_f * tf - f_dim

    def contribution(xb, drop_cols):
        gate = jnp.dot(xb, wg_ref[...].astype(BF16), preferred_element_type=F32)
        up = jnp.dot(xb, wu_ref[...].astype(BF16), preferred_element_type=F32)
        h = (0.5 * gate) * jax.nn.sigmoid(gate) * up
        if drop_cols:
            col = lax.broadcasted_iota(jnp.int32, (1, tf), 1)
            h = jnp.where(col >= drop_cols, h, 0.0)
        return jnp.dot(h.astype(BF16), wo_ref[...].astype(BF16), preferred_element_type=F32)

    @pl.when(j == 0)
    def _():
        xb = x_ref[...].astype(BF16)
        xb_ref[...] = xb
        o_ref[...] = alpha * x_ref[...] + contribution(xb, 0)

    @pl.when((j > 0) & (j < n_f - 1))
    def _():
        o_ref[...] += contribution(xb_ref[...], 0)

    @pl.when(j == n_f - 1)
    def _():
        for r0 in range(0, tm, LN_ROWS):
            r = slice(r0, r0 + LN_ROWS)
            y = o_ref[r, :] + contribution(xb_ref[r, :], overlap)
            o_ref[r, :] = _layer_norm(y, g_ref[...], b_ref[...])
        if ob_ref is not None:
            ob_ref[...] = o_ref[...].astype(BF16)


def _ffn(x, w_in2, w_out2, ln_g, ln_b, wsel, ln_idx, alpha, emit_bf16=False):
    m, d = x.shape
    f_dim = w_in2.shape[1] // 2
    tf, tm = TF_FFN, TM_FFN
    n_f = -(-f_dim // tf)
    assert m % tm == 0 and tm % LN_ROWS == 0 and f_dim % LANES == 0 and tf % LANES == 0 and f_dim >= tf
    assert (n_f * tf - f_dim) < tf and d % SUBLANES == 0 and f_dim % SUBLANES == 0
    last = (f_dim - tf) // LANES
    step = tf // LANES

    def chunk(j):
        return jnp.minimum(j * step, last)

    body = functools.partial(_ffn_body, tf=tf, n_f=n_f, f_dim=f_dim, alpha=alpha)
    return pl.pallas_call(
        body,
        grid=(m // tm, n_f),
        in_specs=[
            pl.BlockSpec((tm, d), lambda i, j: (i, 0), pipeline_mode=pl.Buffered(1)),
            pl.BlockSpec((pl.Element(d), pl.Element(tf)), lambda i, j: (wsel * d, chunk(j) * LANES)),
            pl.BlockSpec((pl.Element(d), pl.Element(tf)),
                         lambda i, j: (wsel * d, (f_dim // LANES + chunk(j)) * LANES)),
            pl.BlockSpec((pl.Element(tf), pl.Element(d)),
                         lambda i, j: ((wsel * (f_dim // SUBLANES) + chunk(j) * (LANES // SUBLANES)) * SUBLANES, 0)),
            pl.BlockSpec((None, 1, d), lambda i, j: (ln_idx, 0, 0)),
            pl.BlockSpec((None, 1, d), lambda i, j: (ln_idx, 0, 0)),
        ],
        out_specs=[pl.BlockSpec((tm, d), lambda i, j: (i, 0))] + (
            [pl.BlockSpec((tm, d), lambda i, j: (i, 0), pipeline_mode=pl.Buffered(1))] if emit_bf16 else []),
        out_shape=[jax.ShapeDtypeStruct((m, d), F32)] + ([jax.ShapeDtypeStruct((m, d), BF16)] if emit_bf16 else []),
        scratch_shapes=[pltpu.VMEM((tm, d), BF16)],
        compiler_params=_params(("parallel", "arbitrary")),
        name=f"ffn_{wsel}",
    )(x, w_in2, w_in2, w_out2, ln_g, ln_b)


PROJ_ORDER = (6, 7, 8, 9, 3, 0, 1, 2, 4, 5)
UF_LOGF, UF_XR = 0, 1
UB_SGA, UB_SGB, UB_K, UB_GR, UB_Q, UB_V, UB_OG = 0, 1, 4, 5, 6, 7, 8


def _forget_lower_bound(hg_lb, layer):
    e = jnp.exp(hg_lb - jnp.max(hg_lb, axis=0, keepdims=True))
    sm = e / jnp.sum(e, axis=0, keepdims=True)
    cum = sm[0:1, :]
    for m in range(1, layer + 1):
        cum = cum + sm[m:m + 1, :]
    return cum - sm[0:1, :]


def _proj_body(xb_ref, w_ref, lb_ref, uf_ref, ub_ref, *, layer):
    j = pl.program_id(1)

    def acc():
        return jnp.dot(xb_ref[...], w_ref[...].astype(BF16), preferred_element_type=F32)

    col = [PROJ_ORDER.index(cb) for cb in range(len(PROJ_ORDER))]

    @pl.when(j <= col[9])
    def _():
        ub_ref[...] = jax.nn.sigmoid(acc()).astype(BF16)

    @pl.when(j == col[3])
    def _():
        lb = _forget_lower_bound(lb_ref[...], layer)
        f = lb + (1.0 - lb) * jax.nn.sigmoid(acc())
        uf_ref[...] = jnp.log(f)
        ub_ref[...] = (1.0 - f).astype(BF16)

    @pl.when(j == col[0])
    def _():
        uf_ref[...] = acc()

    @pl.when(j == col[1])
    def _():
        ub_ref[...] = jax.nn.gelu(acc()).astype(BF16)

    @pl.when((j == col[2]) | (j == col[4]))
    def _():
        ub_ref[...] = acc().astype(BF16)

    @pl.when(j == col[5])
    def _():
        a = acc()
        ub_ref[...] = (a * jax.nn.sigmoid(a)).astype(BF16)


def _proj(xb, w, hg_lb, l):
    m, d = xb.shape
    tm, tn = TM_PROJ, TN_PROJ
    n_blk = w.shape[-1] // tn
    assert m % tm == 0 and w.shape[-1] % tn == 0 and n_blk == len(PROJ_ORDER)
    assert PROJ_ORDER[:4] == (6, 7, 8, 9) and PROJ_ORDER[4:6] == (3, 0)

    def wcol(j):
        cb = jnp.int32(PROJ_ORDER[-1])
        for step in range(n_blk - 1):
            cb = jnp.where(j == step, PROJ_ORDER[step], cb)
        return cb

    body = functools.partial(_proj_body, layer=l)
    return pl.pallas_call(
        body,
        grid=(m // tm, n_blk),
        in_specs=[
            pl.BlockSpec((tm, d), lambda i, j: (i, 0)),
            pl.BlockSpec((None, d, tn), lambda i, j: (l, 0, wcol(j))),
            pl.BlockSpec(hg_lb.shape, lambda i, j: (0, 0)),
        ],
        out_specs=[
            pl.BlockSpec((tm, tn), lambda i, j: (i, jnp.where(j >= 5, 1, 0))),
            pl.BlockSpec((tm, tn), lambda i, j: (i, jnp.where(j >= 5, j - 1, j))),
        ],
        out_shape=[jax.ShapeDtypeStruct((m, 2 * tn), F32), jax.ShapeDtypeStruct((m, (n_blk - 1) * tn), BF16)],
        compiler_params=_params(("parallel", "arbitrary")),
        name=f"proj_l{l}",
    )(xb, w, hg_lb)


def _rglru_body(*refs, n_seq, t_rows, n_t, n_alias):
    xr_ref, gr_ref, c0_ref, h0_ref, cw_ref, cb_ref, wa_ref, ba_ref, wi_ref, bi_ref, lam_ref = refs[:11]
    ya_ref, hl_ref, cn_ref, xbuf, a_scr, b_scr, hcar = refs[11 + n_alias:]
    t = pl.program_id(1)
    c = a_scr.shape[1]
    hist = CONV_W - 1
    base = SUBLANES - hist

    @pl.when(t == 0)
    def _():
        for g in range(n_seq):
            xbuf[g, base:SUBLANES, :] = c0_ref[g]
            hcar[g] = h0_ref[g]

    for g in range(n_seq):
        xbuf[g, SUBLANES:SUBLANES + t_rows, :] = xr_ref[g * t_rows:(g + 1) * t_rows, :]

    log_a_scale = -C_RG * jax.nn.softplus(-lam_ref[...])
    bs = c // RG_BLOCKS
    for n in range(RG_BLOCKS):
        ls = slice(n * bs, (n + 1) * bs)
        for g in range(n_seq):
            xc = cb_ref[:, ls] + sum(
                xbuf[g, base + j:base + j + t_rows, ls] * cw_ref[j:j + 1, ls] for j in range(CONV_W))
            a_scr[g * t_rows:(g + 1) * t_rows, ls] = xc
        xc = a_scr[:, ls]
        xcb = xc.astype(BF16)
        r = jax.nn.sigmoid(jnp.dot(xcb, wa_ref[n], preferred_element_type=F32) + ba_ref[:, ls])
        i = jax.nn.sigmoid(jnp.dot(xcb, wi_ref[n], preferred_element_type=F32) + bi_ref[:, ls])
        log_a = log_a_scale[:, ls] * r
        a = jnp.exp(log_a)
        mult = jnp.sqrt(-jnp.tanh(log_a) * (a * a + 1.0))
        a_scr[:, ls] = a
        b_scr[:, ls] = mult * (i * xc)

    for g in range(n_seq):
        xbuf[g, base:SUBLANES, :] = xbuf[g, base + t_rows:SUBLANES + t_rows, :]

    row = lax.broadcasted_iota(jnp.int32, (SUBLANES, c), 0)
    for g in range(n_seq):
        def group(j, h_in, g=g):
            r0 = pl.multiple_of(g * t_rows + j * SUBLANES, SUBLANES)
            a = a_scr[pl.ds(r0, SUBLANES), :]
            b = b_scr[pl.ds(r0, SUBLANES), :]
            for s in (1, 2, 4):
                keep = row >= s
                b = jnp.where(keep, a * pltpu.roll(b, s, 0) + b, b)
                a = jnp.where(keep, a * pltpu.roll(a, s, 0), a)
            h = a * h_in + b
            a_scr[pl.ds(r0, SUBLANES), :] = h
            return h[SUBLANES - 1:SUBLANES, :]

        hcar[g] = lax.fori_loop(0, t_rows // SUBLANES, group, hcar[g])

    for n in range(RG_BLOCKS):
        ls = slice(n * bs, (n + 1) * bs)
        ya_ref[:, ls] = (a_scr[:, ls] * gr_ref[:, ls].astype(F32)).astype(BF16)

    @pl.when(t == n_t - 1)
    def _():
        for g in range(n_seq):
            hl_ref[g] = hcar[g]
            cn_ref[g] = xbuf[g, base:SUBLANES, :]


def _with_aliases(n_in, prev):
    specs, args, amap = [], [], {}
    for out_idx, p in enumerate(prev):
        if p is not None:
            amap[n_in + len(args)] = out_idx
            specs.append(_ANY)
            args.append(p)
    return specs, args, amap


def _rglru(uf, ub, conv0, h0, conv_w, conv_b, wa_b, ba, wi_b, bi, lam, l, prev, *,
           depth, total_rows, row0, n_batch, seq_len, n_seq, t_rows, tag):
    c = conv_w.shape[-1]
    n_t = seq_len // t_rows
    r = n_seq * t_rows
    assert seq_len % t_rows == 0 and n_batch % n_seq == 0 and row0 % r == 0 and t_rows % SUBLANES == 0
    assert n_seq == 1 or n_t == 1
    blk0 = row0 // r
    hist = CONV_W - 1
    a_specs, a_args, amap = _with_aliases(11, prev)
    body = functools.partial(_rglru_body, n_seq=n_seq, t_rows=t_rows, n_t=n_t, n_alias=len(a_args))
    vec = pl.BlockSpec((None, 1, c), lambda s, t: (l, 0, 0))
    gate_w = pl.BlockSpec((None, RG_BLOCKS, c // RG_BLOCKS, c // RG_BLOCKS), lambda s, t: (l, 0, 0, 0))
    return pl.pallas_call(
        body,
        grid=(n_batch // n_seq, n_t),
        in_specs=[
            pl.BlockSpec((r, c), lambda s, t: (blk0 + s * n_t + t, UF_XR)),
            pl.BlockSpec((r, c), lambda s, t: (blk0 + s * n_t + t, UB_GR)),
            pl.BlockSpec((n_seq, hist, c), lambda s, t: (s, 0, 0)),
            pl.BlockSpec((n_seq, 1, c), lambda s, t: (s, 0, 0)),
            pl.BlockSpec((None, CONV_W, c), lambda s, t: (l, 0, 0)),
            vec, gate_w, vec, gate_w, vec, vec,
        ] + a_specs,
        out_specs=[
            pl.BlockSpec((r, c), lambda s, t: (blk0 + s * n_t + t, 0)),
            pl.BlockSpec((None, n_seq, 1, c), lambda s, t: (l, s, 0, 0)),
            pl.BlockSpec((None, n_seq, hist, c), lambda s, t: (l, s, 0, 0)),
        ],
        out_shape=[
            jax.ShapeDtypeStruct((total_rows, c), BF16),
            jax.ShapeDtypeStruct((depth, n_batch, 1, c), F32),
            jax.ShapeDtypeStruct((depth, n_batch, hist, c), F32),
        ],
        scratch_shapes=[
            pltpu.VMEM((n_seq, SUBLANES + t_rows, c), F32),
            pltpu.VMEM((r, c), F32),
            pltpu.VMEM((r, c), F32),
            pltpu.VMEM((n_seq, 1, c), F32),
        ],
        input_output_aliases=amap,
        compiler_params=_params(("arbitrary", "arbitrary")),
        name=f"rglru_{tag}_l{l}",
    )(uf, ub, conv0, h0, conv_w, conv_b, wa_b, ba, wi_b, bi, lam, *a_args)


def _hgrn_body(*refs, n_seq, t_rows, n_t, chunk, n_alias):
    lf_ref, k_ref, q_ref, v_ref, og_ref, s0_ref, gn_ref = refs[:7]
    yb_ref, so_ref, st_scr, p_scr, yo_scr = refs[7 + n_alias:]
    t = pl.program_id(1)
    c = q_ref.shape[1]
    dk = c // HG_HEADS
    n_lane_blk = c // MXU_DIM
    n_rb = chunk // SUBLANES
    n_units = sum(n_rb - s // SUBLANES for s in range(chunk))
    assert n_units % 2 == 0
    n_tiles = n_units // 2

    single_chunk = t_rows == chunk

    if not single_chunk:
        @pl.when(t == 0)
        def _():
            for g in range(n_seq):
                for h in range(HG_HEADS):
                    st_scr[g, h] = s0_ref[g, h].T

    gn = gn_ref[...]

    row_c = lax.broadcasted_iota(jnp.int32, (chunk, c), 0)
    row_8 = lax.broadcasted_iota(jnp.int32, (SUBLANES, MXU_DIM), 0)
    shifts = [s for s in (1, 2, 4, 8, 16) if s < chunk]
    ri = lax.broadcasted_iota(jnp.int32, (MXU_DIM, MXU_DIM), 0)
    ci_ = lax.broadcasted_iota(jnp.int32, (MXU_DIM, MXU_DIM), 1)
    head_ones = ((ri // dk) == (ci_ // dk)).astype(BF16)

    packed_rows = 2 * SUBLANES

    row_h = lax.broadcasted_iota(jnp.int32, (SUBLANES, dk), 0)
    ones_h = jnp.ones((SUBLANES, dk), BF16)

    def chunk_body(g, ci):
        r0 = g * t_rows if single_chunk else pl.multiple_of(g * t_rows + ci * chunk, chunk)
        rows = pl.ds(r0, chunk)

        def load(ref):
            if chunk % packed_rows == 0:
                return ref[rows, :].astype(F32)
            tile0 = (r0 // packed_rows) * packed_rows
            return ref[tile0:tile0 + packed_rows, :].astype(F32)[r0 - tile0:r0 - tile0 + chunk, :]

        b = lf_ref[rows, :] * LOG2_E
        for s in shifts:
            b = b + jnp.where(row_c >= s, pltpu.roll(b, s, 0), 0.0)
        k = load(k_ref)
        q = load(q_ref)
        v = load(v_ref)

        cs = b - jnp.log2(k)
        b_last = b[chunk - 1:chunk, :]
        qe = (q * jnp.exp2(b)).astype(BF16)
        kd = (k * jnp.exp2(b_last - b)).astype(BF16)
        e_last = jnp.exp2(b_last)
        vb = v.astype(BF16)
        units = [(s, tb) for s in range(chunk) for tb in range(s // SUBLANES, n_rb)]

        outs = []
        for j in range(n_lane_blk):
            lj = slice(j * MXU_DIM, (j + 1) * MXU_DIM)
            bb = [b[i * SUBLANES:(i + 1) * SUBLANES, lj] for i in range(n_rb)]
            qq = [q[i * SUBLANES:(i + 1) * SUBLANES, lj] for i in range(n_rb)]
            prods = []
            for s in range(chunk):
                cs_s = jnp.broadcast_to(cs[s:s + 1, lj], (SUBLANES, MXU_DIM))
                for tb in range(s // SUBLANES, n_rb):
                    d = bb[tb] - cs_s
                    if tb == s // SUBLANES:
                        d = jnp.where(row_8 >= (s % SUBLANES), d, _MASKED)
                    prods.append(qq[tb] * jnp.exp2(d))
                    if len(prods) == 2:
                        ti = units.index((s, tb)) // 2
                        p_scr[g, j, ti * 16:(ti + 1) * 16, :] = jnp.concatenate(prods, axis=0).astype(BF16)
                        prods = []
            a_cols = jnp.dot(p_scr[g, j], head_ones, preferred_element_type=F32)
            intra = [jnp.zeros((SUBLANES, MXU_DIM), F32) for _ in range(n_rb)]
            for s in range(chunk):
                v_s = jnp.broadcast_to(v[s:s + 1, lj], (SUBLANES, MXU_DIM))
                for tb in range(s // SUBLANES, n_rb):
                    un = units.index((s, tb))
                    intra[tb] = intra[tb] + a_cols[un * SUBLANES:(un + 1) * SUBLANES, :] * v_s
            intra = jnp.concatenate(intra, axis=0) if n_rb > 1 else intra[0]
            for h in range(j * (MXU_DIM // dk), (j + 1) * (MXU_DIM // dk)):
                ls = slice(h * dk, (h + 1) * dk)
                li = slice(h * dk - j * MXU_DIM, (h + 1) * dk - j * MXU_DIM)
                if single_chunk:
                    s_in = s0_ref[g, h]
                    inter = jnp.dot(qe[:, ls], s_in.astype(BF16), preferred_element_type=F32)
                    upd = lax.dot_general(kd[:, ls], vb[:, ls], _TN, preferred_element_type=F32)
                    e_hi = e_last[:, ls].astype(BF16)
                    rem = e_last[:, ls] - e_hi.astype(F32)
                    e_mid = rem.astype(BF16)
                    e_lo = (rem - e_mid.astype(F32)).astype(BF16)
                    pieces = jnp.where(row_h == 0, e_hi.astype(F32), jnp.where(
                        row_h == 1, e_mid.astype(F32), jnp.where(row_h == 2, e_lo.astype(F32), 0.0)))
                    decay = lax.dot_general(pieces.astype(BF16), ones_h, _TN, preferred_element_type=F32)
                    so_ref[g, h] = s_in * decay + upd
                else:
                    st = st_scr[g, h]
                    inter = lax.dot_general(qe[:, ls], st.astype(BF16), _NT, preferred_element_type=F32)
                    upd = lax.dot_general(vb[:, ls], kd[:, ls], _TN, preferred_element_type=F32)
                    decay = jnp.broadcast_to(e_last[:, ls], (SUBLANES, dk))
                    new = (st.reshape(dk // SUBLANES, SUBLANES, dk) * decay[None]
                           + upd.reshape(dk // SUBLANES, SUBLANES, dk))
                    st_scr[g, h] = new.reshape(dk, dk)
                o = intra[:, li] + inter
                outs.append(o * lax.rsqrt(jnp.mean(o * o, axis=-1, keepdims=True) + RMS_EPS))
        yo_scr[rows, :] = jnp.concatenate(outs, axis=-1) * gn * load(og_ref)

    if single_chunk:
        for g in range(n_seq):
            chunk_body(g, 0)
    else:
        def step(ci, carry):
            for g in range(n_seq):
                chunk_body(g, ci)
            return carry

        lax.fori_loop(0, t_rows // chunk, step, 0)

    yb_ref[...] = yo_scr[...].astype(BF16)

    if not single_chunk:
        @pl.when(t == n_t - 1)
        def _():
            for g in range(n_seq):
                for h in range(HG_HEADS):
                    so_ref[g, h] = st_scr[g, h].T


def _hgrn(uf, ub, s0, s0_layer, gn, l, prev, *, depth, total_rows, row0, n_batch, seq_len, n_seq, t_rows, chunk, tag):
    c = gn.shape[-1]
    dk = c // HG_HEADS
    n_t = seq_len // t_rows
    r = n_seq * t_rows
    assert seq_len % t_rows == 0 and n_batch % n_seq == 0 and row0 % r == 0 and t_rows % chunk == 0
    assert n_seq == 1 or n_t == 1
    assert t_rows != chunk or n_t == 1
    assert chunk % SUBLANES == 0 and c % MXU_DIM == 0 and MXU_DIM % dk == 0
    blk0 = row0 // r
    n_tiles = sum(chunk // SUBLANES - s // SUBLANES for s in range(chunk)) // 2
    a_specs, a_args, amap = _with_aliases(7, prev)
    body = functools.partial(_hgrn_body, n_seq=n_seq, t_rows=t_rows, n_t=n_t, chunk=chunk, n_alias=len(a_args))

    def col(cb):
        return pl.BlockSpec((r, c), lambda s, t: (blk0 + s * n_t + t, cb))

    return pl.pallas_call(
        body,
        grid=(n_batch // n_seq, n_t),
        in_specs=[
            col(UF_LOGF), col(UB_K), col(UB_Q), col(UB_V), col(UB_OG),
            pl.BlockSpec((None, n_seq, HG_HEADS, dk, dk), lambda s, t: (s0_layer, s, 0, 0, 0)),
            pl.BlockSpec((None, 1, c), lambda s, t: (l, 0, 0)),
        ] + a_specs,
        out_specs=[
            pl.BlockSpec((r, c), lambda s, t: (blk0 + s * n_t + t, 0)),
            pl.BlockSpec((None, n_seq, HG_HEADS, dk, dk), lambda s, t: (l, s, 0, 0, 0)),
        ],
        out_shape=[
            jax.ShapeDtypeStruct((total_rows, c), BF16),
            jax.ShapeDtypeStruct((depth, n_batch, HG_HEADS, dk, dk), F32),
        ],
        scratch_shapes=[
            pltpu.VMEM((n_seq, HG_HEADS, dk, dk), F32),
            pltpu.VMEM((n_seq, c // MXU_DIM, n_tiles * 16, MXU_DIM), BF16),
            pltpu.VMEM((r, c), F32),
        ],
        input_output_aliases=amap,
        compiler_params=_params(("arbitrary", "arbitrary")),
        name=f"hgrn_{tag}_l{l}",
    )(uf, ub, ub, ub, ub, s0, gn, *a_args)


def _merge_body(x_ref, ya_ref, yb_ref, ga_ref, gb_ref, wpa_ref, wpb_ref, wo_ref, g_ref, b_ref, o_ref, *, alpha):
    tm = o_ref.shape[0]
    packed = 2 * SUBLANES
    split = (tm // 2) // packed * packed
    for r in (slice(0, split), slice(split, tm)):
        pa = jnp.dot(ya_ref[r, :], wpa_ref[...], preferred_element_type=F32)
        pb = jnp.dot(yb_ref[r, :], wpb_ref[...], preferred_element_type=F32)
        merged = ga_ref[r, :].astype(F32) * pa + gb_ref[r, :].astype(F32) * pb
        m = jnp.dot(merged.astype(BF16), wo_ref[...], preferred_element_type=F32)
        o_ref[r, :] = _layer_norm(alpha * x_ref[r, :] + m, g_ref[...], b_ref[...])


def _merge(x, ya, yb, ub, wpa_b, wpb_b, wo_b, ln_g, ln_b, l, ln_idx, alpha):
    m, d = x.shape
    c = ya.shape[1]
    tm = TM_MERGE
    assert m % tm == 0
    body = functools.partial(_merge_body, alpha=alpha)
    once = pl.Buffered(1)
    return pl.pallas_call(
        body,
        grid=(m // tm,),
        in_specs=[
            pl.BlockSpec((tm, d), lambda i: (i, 0)),
            pl.BlockSpec((tm, c), lambda i: (i, 0)),
            pl.BlockSpec((tm, c), lambda i: (i, 0)),
            pl.BlockSpec((tm, d), lambda i: (i, UB_SGA)),
            pl.BlockSpec((tm, d), lambda i: (i, UB_SGB)),
            pl.BlockSpec((None, c, d), lambda i: (l, 0, 0), pipeline_mode=once),
            pl.BlockSpec((None, c, d), lambda i: (l, 0, 0), pipeline_mode=once),
            pl.BlockSpec((None, d, d), lambda i: (l, 0, 0), pipeline_mode=once),
            pl.BlockSpec((None, 1, d), lambda i: (ln_idx, 0, 0)),
            pl.BlockSpec((None, 1, d), lambda i: (ln_idx, 0, 0)),
        ],
        out_specs=pl.BlockSpec((tm, d), lambda i: (i, 0)),
        out_shape=jax.ShapeDtypeStruct((m, d), F32),
        compiler_params=_params(("parallel",)),
        name=f"merge_l{l}",
    )(x, ya, yb, ub, ub, wpa_b, wpb_b, wo_b, ln_g, ln_b)


def _assemble_body(xp_ref, xs_ref, o_ref, *, n_prompt_blocks):
    i = pl.program_id(0)

    @pl.when(i < n_prompt_blocks)
    def _():
        o_ref[...] = xp_ref[...]

    @pl.when(i >= n_prompt_blocks)
    def _():
        o_ref[...] = xs_ref[...]


def _meta_body(m_ref, prev_ref, o_ref):
    del prev_ref
    o_ref[...] = m_ref[...]


def _assemble(x_prompt, x_sample2, meta):
    batch, seq, d = x_prompt.shape
    n_meta = meta.shape[0]
    p_len = n_meta + seq
    s_rows = x_sample2.shape[0]
    tr = EXTRACT_ROWS
    per_seq = seq // tr
    n_pb = batch * per_seq
    assert seq % tr == 0 and s_rows % tr == 0 and p_len % SUBLANES == 0 and n_meta % SUBLANES == 0
    total = batch * p_len + s_rows

    def out_row(i):
        ip = jnp.minimum(i, n_pb - 1)
        prompt = (ip // per_seq) * (p_len // SUBLANES) + n_meta // SUBLANES + (ip % per_seq) * (tr // SUBLANES)
        sample = (batch * p_len) // SUBLANES + (i - n_pb) * (tr // SUBLANES)
        return jnp.where(i < n_pb, prompt, sample) * SUBLANES

    x = pl.pallas_call(
        functools.partial(_assemble_body, n_prompt_blocks=n_pb),
        grid=(n_pb + s_rows // tr,),
        in_specs=[
            pl.BlockSpec((None, tr, d), lambda i: (jnp.minimum(i, n_pb - 1) // per_seq,
                                                   jnp.minimum(i, n_pb - 1) % per_seq, 0)),
            pl.BlockSpec((tr, d), lambda i: (jnp.maximum(i - n_pb, 0), 0)),
        ],
        out_specs=pl.BlockSpec((pl.Element(tr), pl.Element(d)), lambda i: (out_row(i), 0)),
        out_shape=jax.ShapeDtypeStruct((total, d), x_prompt.dtype),
        compiler_params=_params(("arbitrary",)),
        name="assemble",
    )(x_prompt, x_sample2)
    return pl.pallas_call(
        _meta_body,
        grid=(batch,),
        in_specs=[pl.BlockSpec((n_meta, d), lambda b: (0, 0)), _ANY],
        out_specs=pl.BlockSpec((pl.Element(n_meta), pl.Element(d)), lambda b: (b * (p_len // SUBLANES) * SUBLANES, 0)),
        out_shape=jax.ShapeDtypeStruct((total, d), x_prompt.dtype),
        input_output_aliases={1: 0},
        compiler_params=_params(("arbitrary",)),
        name="assemble_meta",
    )(meta, x)


def _copy_body(x_ref, o_ref):
    o_ref[...] = x_ref[...]


def _extract_prompt(x, batch, p_len, n_meta):
    d = x.shape[1]
    seq = p_len - n_meta
    tr = EXTRACT_ROWS
    assert seq % tr == 0 and p_len % SUBLANES == 0 and n_meta % SUBLANES == 0 and tr % SUBLANES == 0
    return pl.pallas_call(
        _copy_body,
        grid=(batch, seq // tr),
        in_specs=[pl.BlockSpec(
            (pl.Element(tr), pl.Element(d)),
            lambda b, h: ((b * (p_len // SUBLANES) + n_meta // SUBLANES + h * (tr // SUBLANES)) * SUBLANES, 0))],
        out_specs=pl.BlockSpec((None, tr, d), lambda b, h: (b, h, 0)),
        out_shape=jax.ShapeDtypeStruct((batch, seq, d), x.dtype),
        compiler_params=_params(("parallel", "parallel")),
        name="extract_prompt",
    )(x)


def kernel(x_prompt, x_sample, state_rglru_h, state_rglru_conv, state_hgrn, meta_tokens, ln_g, ln_b, ffn_w_in, ffn_w_out, w_in, conv_w, conv_b, rg_wa, rg_ba, rg_wi, rg_bi, rg_lambda, hg_lb, hg_norm_g, w_pa, w_pb, w_o):
    batch, seq, d_model = x_prompt.shape
    dec_batch, dec_seq, _ = x_sample.shape
    depth = w_in.shape[0]
    n_meta = meta_tokens.shape[0]
    d_rnn = conv_w.shape[-1]
    dk = d_rnn // HG_HEADS
    d_ff = ffn_w_out.shape[-2]
    alpha = (2.0 * depth) ** 0.25
    dt = x_prompt.dtype

    p_len = n_meta + seq
    p_rows = batch * p_len
    total_rows = p_rows + dec_batch * dec_seq
    x = _assemble(x_prompt, x_sample.reshape(dec_batch * dec_seq, d_model), meta_tokens.astype(dt))

    w_in2 = ffn_w_in.reshape(depth * 2 * d_model, 2 * d_ff)
    w_out2 = ffn_w_out.reshape(depth * 2 * d_ff, d_model)
    wpa_b, wpb_b, wo_b = w_pa.astype(BF16), w_pb.astype(BF16), w_o.astype(BF16)
    wa_b, wi_b = rg_wa.astype(BF16), rg_wi.astype(BF16)
    ln_g3 = ln_g.reshape(depth * 3, 1, d_model)
    ln_b3 = ln_b.reshape(depth * 3, 1, d_model)
    vec = lambda a: a.reshape(depth, 1, a.shape[-1])
    conv_b3, ba3, bi3, lam3, gn3 = vec(conv_b), vec(rg_ba), vec(rg_bi), vec(rg_lambda), vec(hg_norm_g)

    zeros_h = jnp.zeros((batch, 1, d_rnn), dt)
    zeros_c = jnp.zeros((batch, CONV_W - 1, d_rnn), dt)
    zeros_s = jnp.zeros((1, batch, HG_HEADS, dk, dk), dt)

    common = dict(depth=depth, total_rows=total_rows)
    prompt = dict(row0=0, n_batch=batch, seq_len=p_len, n_seq=1, t_rows=PROMPT_TILE_ROWS, tag="p", **common)
    sample = dict(row0=p_rows, n_batch=dec_batch, seq_len=dec_seq, n_seq=SAMPLE_SEQS_PER_STEP, t_rows=dec_seq,
                  tag="s", **common)

    ph = pc = ps = sh = sc = ss = None
    for l in range(depth):
        x, xb = _ffn(x, w_in2, w_out2, ln_g3, ln_b3, 2 * l, 3 * l, alpha, emit_bf16=True)
        uf, ub = _proj(xb, w_in, hg_lb, l)
        rg = (conv_w, conv_b3, wa_b, ba3, wi_b, bi3, lam3, l)
        ya, ph, pc = _rglru(uf, ub, zeros_c, zeros_h, *rg, (None, ph, pc), **prompt)
        ya, sh, sc = _rglru(uf, ub, state_rglru_conv[l], state_rglru_h[l][:, None, :], *rg, (ya, sh, sc), **sample)
        yb, ps = _hgrn(uf, ub, zeros_s, 0, gn3, l, (None, ps), chunk=CHUNK, **prompt)
        yb, ss = _hgrn(uf, ub, state_hgrn, l, gn3, l, (yb, ss), chunk=dec_seq, **sample)
        x = _merge(x, ya, yb, ub, wpa_b, wpb_b, wo_b, ln_g3, ln_b3, l, 3 * l + 1, alpha)
        (x,) = _ffn(x, w_in2, w_out2, ln_g3, ln_b3, 2 * l + 1, 3 * l + 2, alpha)

    y_prompt = _extract_prompt(x, batch, p_len, n_meta)
    y_sample = x[p_rows:].reshape(dec_batch, dec_seq, d_model)
    return (y_prompt, y_sample, ph[:, :, 0], pc, ps, sh[:, :, 0], sc, ss)
```

```python
import functools

import jax
import jax.numpy as jnp
from jax import lax
from jax.experimental import pallas as pl
from jax.experimental.pallas import tpu as pltpu

F32 = jnp.float32
BF16 = jnp.bfloat16

CONV_W = 4
C_RG = 8.0
RG_BLOCKS = 8
HG_HEADS = 8
CHUNK = 16
LN_EPS = 1e-5
RMS_EPS = 1e-6

LANES = 128
SUBLANES = 8
MXU_DIM = 256
VMEM_LIMIT_BYTES = 62 * 1024 * 1024

TM_FFN = 928
TF_FFN = 256
TM_PROJ = 1160
TN_PROJ = 1024
TM_MERGE = 464
EXTRACT_ROWS = 1024
PROMPT_TILE_ROWS = 688
SAMPLE_SEQS_PER_STEP = 8

_NT = (((1,), (1,)), ((), ()))
_TN = (((0,), (0,)), ((), ()))
_MASKED = -1e30
LOG2_E = 1.4426950408889634


def _params(semantics):
    return pltpu.CompilerParams(dimension_semantics=semantics, vmem_limit_bytes=VMEM_LIMIT_BYTES)


def _layer_norm(y, g, b):
    mu = jnp.mean(y, axis=-1, keepdims=True)
    d = y - mu
    var = jnp.mean(d * d, axis=-1, keepdims=True)
    return d * lax.rsqrt(var + LN_EPS) * g + b


_ANY = pl.BlockSpec(memory_space=pl.ANY)


def _ffn_body(x_ref, wg_ref, wu_ref, wo_ref, g_ref, b_ref, o_ref, *rest, tf, n_f, f_dim, alpha):
    xb_ref = rest[-1]
    ob_ref = rest[0] if len(rest) == 2 else None
    j = pl.program_id(1)
    tm = o_ref.shape[0]
    overlap = n_f * tf - f_dim

    def contribution(xb, drop_cols):
        gate = jnp.dot(xb, wg_ref[...].astype(BF16), preferred_element_type=F32)
        up = jnp.dot(xb, wu_ref[...].astype(BF16), preferred_element_type=F32)
        h = (0.5 * gate) * jax.nn.sigmoid(gate) * up
        if drop_cols:
            col = lax.broadcasted_iota(jnp.int32, (1, tf), 1)
            h = jnp.where(col >= drop_cols, h, 0.0)
        return jnp.dot(h.astype(BF16), wo_ref[...].astype(BF16), preferred_element_type=F32)

    @pl.when(j == 0)
    def _():
        xb = x_ref[...].astype(BF16)
        xb_ref[...] = xb
        o_ref[...] = alpha * x_ref[...] + contribution(xb, 0)

    @pl.when((j > 0) & (j < n_f - 1))
    def _():
        o_ref[...] += contribution(xb_ref[...], 0)

    @pl.when(j == n_f - 1)
    def _():
        packed = 2 * SUBLANES
        split = (tm // 2) // packed * packed
        for r in (slice(0, split), slice(split, tm)):
            y = o_ref[r, :] + contribution(xb_ref[r, :], overlap)
            o_ref[r, :] = _layer_norm(y, g_ref[...], b_ref[...])
        if ob_ref is not None:
            ob_ref[...] = o_ref[...].astype(BF16)


def _ffn(x, w_in2, w_out2, ln_g, ln_b, wsel, ln_idx, alpha, emit_bf16=False):
    m, d = x.shape
    f_dim = w_in2.shape[1] // 2
    tf, tm = TF_FFN, TM_FFN
    n_f = -(-f_dim // tf)
    assert m % tm == 0 and tm % SUBLANES == 0 and f_dim % LANES == 0 and tf % LANES == 0 and f_dim >= tf
    assert (n_f * tf - f_dim) < tf and d % SUBLANES == 0 and f_dim % SUBLANES == 0
    last = (f_dim - tf) // LANES
    step = tf // LANES

    def chunk(j):
        return jnp.minimum(j * step, last)

    body = functools.partial(_ffn_body, tf=tf, n_f=n_f, f_dim=f_dim, alpha=alpha)
    return pl.pallas_call(
        body,
        grid=(m // tm, n_f),
        in_specs=[
            pl.BlockSpec((tm, d), lambda i, j: (i, 0)),
            pl.BlockSpec((pl.Element(d), pl.Element(tf)), lambda i, j: (wsel * d, chunk(j) * LANES)),
            pl.BlockSpec((pl.Element(d), pl.Element(tf)),
                         lambda i, j: (wsel * d, (f_dim // LANES + chunk(j)) * LANES)),
            pl.BlockSpec((pl.Element(tf), pl.Element(d)),
                         lambda i, j: ((wsel * (f_dim // SUBLANES) + chunk(j) * (LANES // SUBLANES)) * SUBLANES, 0)),
            pl.BlockSpec((None, 1, d), lambda i, j: (ln_idx, 0, 0)),
            pl.BlockSpec((None, 1, d), lambda i, j: (ln_idx, 0, 0)),
        ],
        out_specs=[pl.BlockSpec((tm, d), lambda i, j: (i, 0))] + (
            [pl.BlockSpec((tm, d), lambda i, j: (i, 0), pipeline_mode=pl.Buffered(1))] if emit_bf16 else []),
        out_shape=[jax.ShapeDtypeStruct((m, d), F32)] + ([jax.ShapeDtypeStruct((m, d), BF16)] if emit_bf16 else []),
        scratch_shapes=[pltpu.VMEM((tm, d), BF16)],
        compiler_params=_params(("parallel", "arbitrary")),
        name=f"ffn_{wsel}",
    )(x, w_in2, w_in2, w_out2, ln_g, ln_b)


PROJ_ORDER = (6, 7, 8, 9, 3, 0, 1, 2, 4, 5)
UF_LOGF, UF_XR = 0, 1
UB_SGA, UB_SGB, UB_K, UB_GR, UB_Q, UB_V, UB_OG = 0, 1, 4, 5, 6, 7, 8


def _forget_lower_bound(hg_lb, layer):
    e = jnp.exp(hg_lb - jnp.max(hg_lb, axis=0, keepdims=True))
    sm = e / jnp.sum(e, axis=0, keepdims=True)
    cum = sm[0:1, :]
    for m in range(1, layer + 1):
        cum = cum + sm[m:m + 1, :]
    return cum - sm[0:1, :]


def _proj_body(xb_ref, w_ref, lb_ref, uf_ref, ub_ref, *, layer):
    j = pl.program_id(1)

    def acc():
        return jnp.dot(xb_ref[...], w_ref[...].astype(BF16), preferred_element_type=F32)

    col = [PROJ_ORDER.index(cb) for cb in range(len(PROJ_ORDER))]

    @pl.when(j <= col[9])
    def _():
        ub_ref[...] = jax.nn.sigmoid(acc()).astype(BF16)

    @pl.when(j == col[3])
    def _():
        lb = _forget_lower_bound(lb_ref[...], layer)
        f = lb + (1.0 - lb) * jax.nn.sigmoid(acc())
        uf_ref[...] = jnp.log(f)
        ub_ref[...] = (1.0 - f).astype(BF16)

    @pl.when(j == col[0])
    def _():
        uf_ref[...] = acc()

    @pl.when(j == col[1])
    def _():
        ub_ref[...] = jax.nn.gelu(acc()).astype(BF16)

    @pl.when((j == col[2]) | (j == col[4]))
    def _():
        ub_ref[...] = acc().astype(BF16)

    @pl.when(j == col[5])
    def _():
        a = acc()
        ub_ref[...] = (a * jax.nn.sigmoid(a)).astype(BF16)


def _proj(xb, w, hg_lb, l):
    m, d = xb.shape
    tm, tn = TM_PROJ, TN_PROJ
    n_blk = w.shape[-1] // tn
    assert m % tm == 0 and w.shape[-1] % tn == 0 and n_blk == len(PROJ_ORDER)
    assert PROJ_ORDER[:4] == (6, 7, 8, 9) and PROJ_ORDER[4:6] == (3, 0)

    def wcol(j):
        cb = jnp.int32(PROJ_ORDER[-1])
        for step in range(n_blk - 1):
            cb = jnp.where(j == step, PROJ_ORDER[step], cb)
        return cb

    body = functools.partial(_proj_body, layer=l)
    return pl.pallas_call(
        body,
        grid=(m // tm, n_blk),
        in_specs=[
            pl.BlockSpec((tm, d), lambda i, j: (i, 0)),
            pl.BlockSpec((None, d, tn), lambda i, j: (l, 0, wcol(j))),
            pl.BlockSpec(hg_lb.shape, lambda i, j: (0, 0)),
        ],
        out_specs=[
            pl.BlockSpec((tm, tn), lambda i, j: (i, jnp.where(j >= 5, 1, 0))),
            pl.BlockSpec((tm, tn), lambda i, j: (i, jnp.where(j >= 5, j - 1, j))),
        ],
        out_shape=[jax.ShapeDtypeStruct((m, 2 * tn), F32), jax.ShapeDtypeStruct((m, (n_blk - 1) * tn), BF16)],
        compiler_params=_params(("parallel", "arbitrary")),
        name=f"proj_l{l}",
    )(xb, w, hg_lb)


def _rglru_body(*refs, n_seq, t_rows, n_t, n_alias):
    xr_ref, gr_ref, c0_ref, h0_ref, cw_ref, cb_ref, wa_ref, ba_ref, wi_ref, bi_ref, lam_ref = refs[:11]
    ya_ref, hl_ref, cn_ref, xbuf, a_scr, b_scr, hcar = refs[11 + n_alias:]
    t = pl.program_id(1)
    c = a_scr.shape[1]
    hist = CONV_W - 1
    base = SUBLANES - hist

    @pl.when(t == 0)
    def _():
        for g in range(n_seq):
            xbuf[g, base:SUBLANES, :] = c0_ref[g]
            hcar[g] = h0_ref[g]

    for g in range(n_seq):
        xbuf[g, SUBLANES:SUBLANES + t_rows, :] = xr_ref[g * t_rows:(g + 1) * t_rows, :]

    log_a_scale = -C_RG * jax.nn.softplus(-lam_ref[...])
    bs = c // RG_BLOCKS
    rows_total = n_seq * t_rows
    grp = (rows_total // SUBLANES, SUBLANES, bs)

    def row_vec(v):
        return jnp.broadcast_to(v, (SUBLANES, bs))[None]

    def sigmoid(z):
        return 0.5 * jnp.tanh(0.5 * z) + 0.5

    for n in range(RG_BLOCKS):
        ls = slice(n * bs, (n + 1) * bs)
        taps = [row_vec(cw_ref[j:j + 1, ls]) for j in range(CONV_W)]
        for g in range(n_seq):
            xc = row_vec(cb_ref[:, ls]) + sum(
                xbuf[g, base + j:base + j + t_rows, ls].reshape(t_rows // SUBLANES, SUBLANES, bs) * taps[j]
                for j in range(CONV_W))
            a_scr[g * t_rows:(g + 1) * t_rows, ls] = xc.reshape(t_rows, bs)
        xc = a_scr[:, ls]
        xcb = xc.astype(BF16)
        xc = xc.reshape(grp)
        r = sigmoid(jnp.dot(xcb, wa_ref[n], preferred_element_type=F32).reshape(grp) + row_vec(ba_ref[:, ls]))
        i = sigmoid(jnp.dot(xcb, wi_ref[n], preferred_element_type=F32).reshape(grp) + row_vec(bi_ref[:, ls]))
        log_a = row_vec(log_a_scale[:, ls]) * r
        a = jnp.exp(log_a)
        m2 = jnp.tanh(-log_a) * (a * a + 1.0)
        mult = jnp.where(m2 > 0.0, m2 * lax.rsqrt(m2), 0.0)
        a_scr[:, ls] = a.reshape(rows_total, bs)
        b_scr[:, ls] = (mult * (i * xc)).reshape(rows_total, bs)

    for g in range(n_seq):
        xbuf[g, base:SUBLANES, :] = xbuf[g, base + t_rows:SUBLANES + t_rows, :]

    row = lax.broadcasted_iota(jnp.int32, (SUBLANES, c), 0)
    for g in range(n_seq):
        def group(j, h_in, g=g):
            r0 = pl.multiple_of(g * t_rows + j * SUBLANES, SUBLANES)
            a = a_scr[pl.ds(r0, SUBLANES), :]
            b = b_scr[pl.ds(r0, SUBLANES), :]
            for s in (1, 2, 4):
                keep = row >= s
                b = jnp.where(keep, a * pltpu.roll(b, s, 0) + b, b)
                a = jnp.where(keep, a * pltpu.roll(a, s, 0), a)
            h = a * h_in + b
            a_scr[pl.ds(r0, SUBLANES), :] = h
            return h[SUBLANES - 1:SUBLANES, :]

        hcar[g] = lax.fori_loop(0, t_rows // SUBLANES, group, hcar[g])

    for n in range(RG_BLOCKS):
        ls = slice(n * bs, (n + 1) * bs)
        ya_ref[:, ls] = (a_scr[:, ls] * gr_ref[:, ls].astype(F32)).astype(BF16)

    @pl.when(t == n_t - 1)
    def _():
        for g in range(n_seq):
            hl_ref[g] = hcar[g]
            cn_ref[g] = xbuf[g, base:SUBLANES, :]


def _with_aliases(n_in, prev):
    specs, args, amap = [], [], {}
    for out_idx, p in enumerate(prev):
        if p is not None:
            amap[n_in + len(args)] = out_idx
            specs.append(_ANY)
            args.append(p)
    return specs, args, amap


def _rglru(uf, ub, conv0, h0, conv_w, conv_b, wa_b, ba, wi_b, bi, lam, l, prev, *,
           depth, total_rows, row0, n_batch, seq_len, n_seq, t_rows, tag):
    c = conv_w.shape[-1]
    n_t = seq_len // t_rows
    r = n_seq * t_rows
    assert seq_len % t_rows == 0 and n_batch % n_seq == 0 and row0 % r == 0 and t_rows % SUBLANES == 0
    assert n_seq == 1 or n_t == 1
    blk0 = row0 // r
    hist = CONV_W - 1
    a_specs, a_args, amap = _with_aliases(11, prev)
    body = functools.partial(_rglru_body, n_seq=n_seq, t_rows=t_rows, n_t=n_t, n_alias=len(a_args))
    vec = pl.BlockSpec((None, 1, c), lambda s, t: (l, 0, 0))
    gate_w = pl.BlockSpec((None, RG_BLOCKS, c // RG_BLOCKS, c // RG_BLOCKS), lambda s, t: (l, 0, 0, 0))
    return pl.pallas_call(
        body,
        grid=(n_batch // n_seq, n_t),
        in_specs=[
            pl.BlockSpec((r, c), lambda s, t: (blk0 + s * n_t + t, UF_XR)),
            pl.BlockSpec((r, c), lambda s, t: (blk0 + s * n_t + t, UB_GR)),
            pl.BlockSpec((n_seq, hist, c), lambda s, t: (s, 0, 0)),
            pl.BlockSpec((n_seq, 1, c), lambda s, t: (s, 0, 0)),
            pl.BlockSpec((None, CONV_W, c), lambda s, t: (l, 0, 0)),
            vec, gate_w, vec, gate_w, vec, vec,
        ] + a_specs,
        out_specs=[
            pl.BlockSpec((r, c), lambda s, t: (blk0 + s * n_t + t, 0)),
            pl.BlockSpec((None, n_seq, 1, c), lambda s, t: (l, s, 0, 0)),
            pl.BlockSpec((None, n_seq, hist, c), lambda s, t: (l, s, 0, 0)),
        ],
        out_shape=[
            jax.ShapeDtypeStruct((total_rows, c), BF16),
            jax.ShapeDtypeStruct((depth, n_batch, 1, c), F32),
            jax.ShapeDtypeStruct((depth, n_batch, hist, c), F32),
        ],
        scratch_shapes=[
            pltpu.VMEM((n_seq, SUBLANES + t_rows, c), F32),
            pltpu.VMEM((r, c), F32),
            pltpu.VMEM((r, c), F32),
            pltpu.VMEM((n_seq, 1, c), F32),
        ],
        input_output_aliases=amap,
        compiler_params=_params(("arbitrary", "arbitrary")),
        name=f"rglru_{tag}_l{l}",
    )(uf, ub, conv0, h0, conv_w, conv_b, wa_b, ba, wi_b, bi, lam, *a_args)


def _hgrn_body(*refs, n_seq, t_rows, n_t, chunk, n_alias):
    lf_ref, k_ref, q_ref, v_ref, og_ref, s0_ref, gn_ref = refs[:7]
    yb_ref, so_ref, st_scr, p_scr, yo_scr = refs[7 + n_alias:]
    t = pl.program_id(1)
    c = q_ref.shape[1]
    dk = c // HG_HEADS
    n_lane_blk = c // MXU_DIM
    n_rb = chunk // SUBLANES
    n_units = sum(n_rb - s // SUBLANES for s in range(chunk))
    assert n_units % 2 == 0
    n_tiles = n_units // 2

    single_chunk = t_rows == chunk

    if not single_chunk:
        @pl.when(t == 0)
        def _():
            for g in range(n_seq):
                for h in range(HG_HEADS):
                    st_scr[g, h] = s0_ref[g, h].T

    gn = gn_ref[...]

    row_c = lax.broadcasted_iota(jnp.int32, (chunk, c), 0)
    row_8 = lax.broadcasted_iota(jnp.int32, (SUBLANES, MXU_DIM), 0)
    shifts = [s for s in (1, 2, 4, 8, 16) if s < chunk]
    ri = lax.broadcasted_iota(jnp.int32, (MXU_DIM, MXU_DIM), 0)
    ci_ = lax.broadcasted_iota(jnp.int32, (MXU_DIM, MXU_DIM), 1)
    head_ones = ((ri // dk) == (ci_ // dk)).astype(BF16)

    packed_rows = 2 * SUBLANES

    row_h = lax.broadcasted_iota(jnp.int32, (SUBLANES, dk), 0)
    ones_h = jnp.ones((SUBLANES, dk), BF16)

    def chunk_body(g, ci):
        r0 = g * t_rows if single_chunk else pl.multiple_of(g * t_rows + ci * chunk, chunk)
        rows = pl.ds(r0, chunk)

        def load(ref):
            if chunk % packed_rows == 0:
                return ref[rows, :].astype(F32)
            tile0 = (r0 // packed_rows) * packed_rows
            return ref[tile0:tile0 + packed_rows, :].astype(F32)[r0 - tile0:r0 - tile0 + chunk, :]

        b = lf_ref[rows, :] * LOG2_E
        for s in shifts:
            b = b + jnp.where(row_c >= s, pltpu.roll(b, s, 0), 0.0)
        k = load(k_ref)
        q = load(q_ref)
        v = load(v_ref)

        cs = b - jnp.log2(k)
        b_last = b[chunk - 1:chunk, :]
        qe = (q * jnp.exp2(b)).astype(BF16)
        kd = (k * jnp.exp2(b_last - b)).astype(BF16)
        e_last = jnp.exp2(b_last)
        vb = v.astype(BF16)
        units = [(s, tb) for s in range(chunk) for tb in range(s // SUBLANES, n_rb)]

        outs = []
        for j in range(n_lane_blk):
            lj = slice(j * MXU_DIM, (j + 1) * MXU_DIM)
            bb = [b[i * SUBLANES:(i + 1) * SUBLANES, lj] for i in range(n_rb)]
            qq = [q[i * SUBLANES:(i + 1) * SUBLANES, lj] for i in range(n_rb)]
            prods = []
            for s in range(chunk):
                cs_s = jnp.broadcast_to(cs[s:s + 1, lj], (SUBLANES, MXU_DIM))
                for tb in range(s // SUBLANES, n_rb):
                    d = bb[tb] - cs_s
                    if tb == s // SUBLANES:
                        d = jnp.where(row_8 >= (s % SUBLANES), d, _MASKED)
                    prods.append(qq[tb] * jnp.exp2(d))
                    if len(prods) == 2:
                        ti = units.index((s, tb)) // 2
                        p_scr[g, j, ti * 16:(ti + 1) * 16, :] = jnp.concatenate(prods, axis=0).astype(BF16)
                        prods = []
            a_cols = jnp.dot(p_scr[g, j], head_ones, preferred_element_type=F32)
            intra = [jnp.zeros((SUBLANES, MXU_DIM), F32) for _ in range(n_rb)]
            for s in range(chunk):
                v_s = jnp.broadcast_to(v[s:s + 1, lj], (SUBLANES, MXU_DIM))
                for tb in range(s // SUBLANES, n_rb):
                    un = units.index((s, tb))
                    intra[tb] = intra[tb] + a_cols[un * SUBLANES:(un + 1) * SUBLANES, :] * v_s
            intra = jnp.concatenate(intra, axis=0) if n_rb > 1 else intra[0]
            for h in range(j * (MXU_DIM // dk), (j + 1) * (MXU_DIM // dk)):
                ls = slice(h * dk, (h + 1) * dk)
                li = slice(h * dk - j * MXU_DIM, (h + 1) * dk - j * MXU_DIM)
                if single_chunk:
                    s_in = s0_ref[g, h]
                    inter = jnp.dot(qe[:, ls], s_in.astype(BF16), preferred_element_type=F32)
                    upd = lax.dot_general(kd[:, ls], vb[:, ls], _TN, preferred_element_type=F32)
                    e_hi = e_last[:, ls].astype(BF16)
                    rem = e_last[:, ls] - e_hi.astype(F32)
                    e_mid = rem.astype(BF16)
                    e_lo = (rem - e_mid.astype(F32)).astype(BF16)
                    pieces = jnp.where(row_h == 0, e_hi.astype(F32), jnp.where(
                        row_h == 1, e_mid.astype(F32), jnp.where(row_h == 2, e_lo.astype(F32), 0.0)))
                    decay = lax.dot_general(pieces.astype(BF16), ones_h, _TN, preferred_element_type=F32)
                    so_ref[g, h] = s_in * decay + upd
                else:
                    st = st_scr[g, h]
                    inter = lax.dot_general(qe[:, ls], st.astype(BF16), _NT, preferred_element_type=F32)
                    upd = lax.dot_general(vb[:, ls], kd[:, ls], _TN, preferred_element_type=F32)
                    decay = jnp.broadcast_to(e_last[:, ls], (SUBLANES, dk))
                    new = (st.reshape(dk // SUBLANES, SUBLANES, dk) * decay[None]
                           + upd.reshape(dk // SUBLANES, SUBLANES, dk))
                    st_scr[g, h] = new.reshape(dk, dk)
                o = intra[:, li] + inter
                outs.append(o * lax.rsqrt(jnp.mean(o * o, axis=-1, keepdims=True) + RMS_EPS))
        yo_scr[rows, :] = jnp.concatenate(outs, axis=-1) * gn * load(og_ref)

    if single_chunk:
        for g in range(n_seq):
            chunk_body(g, 0)
    else:
        def step(ci, carry):
            for g in range(n_seq):
                chunk_body(g, ci)
            return carry

        lax.fori_loop(0, t_rows // chunk, step, 0)

    yb_ref[...] = yo_scr[...].astype(BF16)

    if not single_chunk:
        @pl.when(t == n_t - 1)
        def _():
            for g in range(n_seq):
                for h in range(HG_HEADS):
                    so_ref[g, h] = st_scr[g, h].T


def _hgrn(uf, ub, s0, s0_layer, gn, l, prev, *, depth, total_rows, row0, n_batch, seq_len, n_seq, t_rows, chunk, tag):
    c = gn.shape[-1]
    dk = c // HG_HEADS
    n_t = seq_len // t_rows
    r = n_seq * t_rows
    assert seq_len % t_rows == 0 and n_batch % n_seq == 0 and row0 % r == 0 and t_rows % chunk == 0
    assert n_seq == 1 or n_t == 1
    assert t_rows != chunk or n_t == 1
    assert chunk % SUBLANES == 0 and c % MXU_DIM == 0 and MXU_DIM % dk == 0
    blk0 = row0 // r
    n_tiles = sum(chunk // SUBLANES - s // SUBLANES for s in range(chunk)) // 2
    a_specs, a_args, amap = _with_aliases(7, prev)
    body = functools.partial(_hgrn_body, n_seq=n_seq, t_rows=t_rows, n_t=n_t, chunk=chunk, n_alias=len(a_args))

    def col(cb):
        return pl.BlockSpec((r, c), lambda s, t: (blk0 + s * n_t + t, cb))

    return pl.pallas_call(
        body,
        grid=(n_batch // n_seq, n_t),
        in_specs=[
            col(UF_LOGF), col(UB_K), col(UB_Q), col(UB_V), col(UB_OG),
            pl.BlockSpec((None, n_seq, HG_HEADS, dk, dk), lambda s, t: (s0_layer, s, 0, 0, 0)),
            pl.BlockSpec((None, 1, c), lambda s, t: (l, 0, 0)),
        ] + a_specs,
        out_specs=[
            pl.BlockSpec((r, c), lambda s, t: (blk0 + s * n_t + t, 0)),
            pl.BlockSpec((None, n_seq, HG_HEADS, dk, dk), lambda s, t: (l, s, 0, 0, 0)),
        ],
        out_shape=[
            jax.ShapeDtypeStruct((total_rows, c), BF16),
            jax.ShapeDtypeStruct((depth, n_batch, HG_HEADS, dk, dk), F32),
        ],
        scratch_shapes=[
            pltpu.VMEM((n_seq, HG_HEADS, dk, dk), F32),
            pltpu.VMEM((n_seq, c // MXU_DIM, n_tiles * 16, MXU_DIM), BF16),
            pltpu.VMEM((r, c), F32),
        ],
        input_output_aliases=amap,
        compiler_params=_params(("arbitrary", "arbitrary")),
        name=f"hgrn_{tag}_l{l}",
    )(uf, ub, ub, ub, ub, s0, gn, *a_args)


def _merge_body(x_ref, ya_ref, yb_ref, ga_ref, gb_ref, wpa_ref, wpb_ref, wo_ref, g_ref, b_ref, o_ref, *, alpha):
    tm = o_ref.shape[0]
    packed = 2 * SUBLANES
    split = (tm // 2) // packed * packed
    for r in (slice(0, split), slice(split, tm)):
        pa = jnp.dot(ya_ref[r, :], wpa_ref[...], preferred_element_type=F32)
        pb = jnp.dot(yb_ref[r, :], wpb_ref[...], preferred_element_type=F32)
        merged = ga_ref[r, :].astype(F32) * pa + gb_ref[r, :].astype(F32) * pb
        m = jnp.dot(merged.astype(BF16), wo_ref[...], preferred_element_type=F32)
        o_ref[r, :] = _layer_norm(alpha * x_ref[r, :] + m, g_ref[...], b_ref[...])


def _merge(x, ya, yb, ub, wpa_b, wpb_b, wo_b, ln_g, ln_b, l, ln_idx, alpha):
    m, d = x.shape
    c = ya.shape[1]
    tm = TM_MERGE
    assert m % tm == 0
    body = functools.partial(_merge_body, alpha=alpha)
    once = pl.Buffered(1)
    return pl.pallas_call(
        body,
        grid=(m // tm,),
        in_specs=[
            pl.BlockSpec((tm, d), lambda i: (i, 0)),
            pl.BlockSpec((tm, c), lambda i: (i, 0)),
            pl.BlockSpec((tm, c), lambda i: (i, 0)),
            pl.BlockSpec((tm, d), lambda i: (i, UB_SGA)),
            pl.BlockSpec((tm, d), lambda i: (i, UB_SGB)),
            pl.BlockSpec((None, c, d), lambda i: (l, 0, 0), pipeline_mode=once),
            pl.BlockSpec((None, c, d), lambda i: (l, 0, 0), pipeline_mode=once),
            pl.BlockSpec((None, d, d), lambda i: (l, 0, 0), pipeline_mode=once),
            pl.BlockSpec((None, 1, d), lambda i: (ln_idx, 0, 0)),
            pl.BlockSpec((None, 1, d), lambda i: (ln_idx, 0, 0)),
        ],
        out_specs=pl.BlockSpec((tm, d), lambda i: (i, 0)),
        out_shape=jax.ShapeDtypeStruct((m, d), F32),
        compiler_params=_params(("parallel",)),
        name=f"merge_l{l}",
    )(x, ya, yb, ub, ub, wpa_b, wpb_b, wo_b, ln_g, ln_b)


def _assemble_body(xp_ref, xs_ref, o_ref, *, n_prompt_blocks):
    i = pl.program_id(0)

    @pl.when(i < n_prompt_blocks)
    def _():
        o_ref[...] = xp_ref[...]

    @pl.when(i >= n_prompt_blocks)
    def _():
        o_ref[...] = xs_ref[...]


def _meta_body(m_ref, prev_ref, o_ref):
    del prev_ref
    o_ref[...] = m_ref[...]


def _assemble(x_prompt, x_sample2, meta):
    batch, seq, d = x_prompt.shape
    n_meta = meta.shape[0]
    p_len = n_meta + seq
    s_rows = x_sample2.shape[0]
    tr = EXTRACT_ROWS
    per_seq = seq // tr
    n_pb = batch * per_seq
    assert seq % tr == 0 and s_rows % tr == 0 and p_len % SUBLANES == 0 and n_meta % SUBLANES == 0
    total = batch * p_len + s_rows

    def out_row(i):
        ip = jnp.minimum(i, n_pb - 1)
        prompt = (ip // per_seq) * (p_len // SUBLANES) + n_meta // SUBLANES + (ip % per_seq) * (tr // SUBLANES)
        sample = (batch * p_len) // SUBLANES + (i - n_pb) * (tr // SUBLANES)
        return jnp.where(i < n_pb, prompt, sample) * SUBLANES

    x = pl.pallas_call(
        functools.partial(_assemble_body, n_prompt_blocks=n_pb),
        grid=(n_pb + s_rows // tr,),
        in_specs=[
            pl.BlockSpec((None, tr, d), lambda i: (jnp.minimum(i, n_pb - 1) // per_seq,
                                                   jnp.minimum(i, n_pb - 1) % per_seq, 0)),
            pl.BlockSpec((tr, d), lambda i: (jnp.maximum(i - n_pb, 0), 0)),
        ],
        out_specs=pl.BlockSpec((pl.Element(tr), pl.Element(d)), lambda i: (out_row(i), 0)),
        out_shape=jax.ShapeDtypeStruct((total, d), x_prompt.dtype),
        compiler_params=_params(("arbitrary",)),
        name="assemble",
    )(x_prompt, x_sample2)
    return pl.pallas_call(
        _meta_body,
        grid=(batch,),
        in_specs=[pl.BlockSpec((n_meta, d), lambda b: (0, 0)), _ANY],
        out_specs=pl.BlockSpec((pl.Element(n_meta), pl.Element(d)), lambda b: (b * (p_len // SUBLANES) * SUBLANES, 0)),
        out_shape=jax.ShapeDtypeStruct((total, d), x_prompt.dtype),
        input_output_aliases={1: 0},
        compiler_params=_params(("arbitrary",)),
        name="assemble_meta",
    )(meta, x)


def _copy_body(x_ref, o_ref):
    o_ref[...] = x_ref[...]


def _extract_prompt(x, batch, p_len, n_meta):
    d = x.shape[1]
    seq = p_len - n_meta
    tr = EXTRACT_ROWS
    assert seq % tr == 0 and p_len % SUBLANES == 0 and n_meta % SUBLANES == 0 and tr % SUBLANES == 0
    return pl.pallas_call(
        _copy_body,
        grid=(batch, seq // tr),
        in_specs=[pl.BlockSpec(
            (pl.Element(tr), pl.Element(d)),
            lambda b, h: ((b * (p_len // SUBLANES) + n_meta // SUBLANES + h * (tr // SUBLANES)) * SUBLANES, 0))],
        out_specs=pl.BlockSpec((None, tr, d), lambda b, h: (b, h, 0)),
        out_shape=jax.ShapeDtypeStruct((batch, seq, d), x.dtype),
        compiler_params=_params(("parallel", "parallel")),
        name="extract_prompt",
    )(x)


def kernel(x_prompt, x_sample, state_rglru_h, state_rglru_conv, state_hgrn, meta_tokens, ln_g, ln_b, ffn_w_in, ffn_w_out, w_in, conv_w, conv_b, rg_wa, rg_ba, rg_wi, rg_bi, rg_lambda, hg_lb, hg_norm_g, w_pa, w_pb, w_o):
    batch, seq, d_model = x_prompt.shape
    dec_batch, dec_seq, _ = x_sample.shape
    depth = w_in.shape[0]
    n_meta = meta_tokens.shape[0]
    d_rnn = conv_w.shape[-1]
    dk = d_rnn // HG_HEADS
    d_ff = ffn_w_out.shape[-2]
    alpha = (2.0 * depth) ** 0.25
    dt = x_prompt.dtype

    p_len = n_meta + seq
    p_rows = batch * p_len
    total_rows = p_rows + dec_batch * dec_seq
    x = _assemble(x_prompt, x_sample.reshape(dec_batch * dec_seq, d_model), meta_tokens.astype(dt))

    w_in2 = ffn_w_in.reshape(depth * 2 * d_model, 2 * d_ff)
    w_out2 = ffn_w_out.reshape(depth * 2 * d_ff, d_model)
    wpa_b, wpb_b, wo_b = w_pa.astype(BF16), w_pb.astype(BF16), w_o.astype(BF16)
    wa_b, wi_b = rg_wa.astype(BF16), rg_wi.astype(BF16)
    ln_g3 = ln_g.reshape(depth * 3, 1, d_model)
    ln_b3 = ln_b.reshape(depth * 3, 1, d_model)
    vec = lambda a: a.reshape(depth, 1, a.shape[-1])
    conv_b3, ba3, bi3, lam3, gn3 = vec(conv_b), vec(rg_ba), vec(rg_bi), vec(rg_lambda), vec(hg_norm_g)

    zeros_h = jnp.zeros((batch, 1, d_rnn), dt)
    zeros_c = jnp.zeros((batch, CONV_W - 1, d_rnn), dt)
    zeros_s = jnp.zeros((1, batch, HG_HEADS, dk, dk), dt)

    common = dict(depth=depth, total_rows=total_rows)
    prompt = dict(row0=0, n_batch=batch, seq_len=p_len, n_seq=1, t_rows=PROMPT_TILE_ROWS, tag="p", **common)
    sample = dict(row0=p_rows, n_batch=dec_batch, seq_len=dec_seq, n_seq=SAMPLE_SEQS_PER_STEP, t_rows=dec_seq,
                  tag="s", **common)

    ph = pc = ps = sh = sc = ss = None
    for l in range(depth):
        x, xb = _ffn(x, w_in2, w_out2, ln_g3, ln_b3, 2 * l, 3 * l, alpha, emit_bf16=True)
        uf, ub = _proj(xb, w_in, hg_lb, l)
        rg = (conv_w, conv_b3, wa_b, ba3, wi_b, bi3, lam3, l)
        ya, ph, pc = _rglru(uf, ub, zeros_c, zeros_h, *rg, (None, ph, pc), **prompt)
        ya, sh, sc = _rglru(uf, ub, state_rglru_conv[l], state_rglru_h[l][:, None, :], *rg, (ya, sh, sc), **sample)
        yb, ps = _hgrn(uf, ub, zeros_s, 0, gn3, l, (None, ps), chunk=CHUNK, **prompt)
        yb, ss = _hgrn(uf, ub, state_hgrn, l, gn3, l, (yb, ss), chunk=dec_seq, **sample)
        x = _merge(x, ya, yb, ub, wpa_b, wpb_b, wo_b, ln_g3, ln_b3, l, 3 * l + 1, alpha)
        (x,) = _ffn(x, w_in2, w_out2, ln_g3, ln_b3, 2 * l + 1, 3 * l + 2, alpha)

    y_prompt = _extract_prompt(x, batch, p_len, n_meta)
    y_sample = x[p_rows:].reshape(dec_batch, dec_seq, d_model)
    return (y_prompt, y_sample, ph[:, :, 0], pc, ps, sh[:, :, 0], sc, ss)
```

```python
import functools

import jax
import jax.numpy as jnp
from jax import lax
from jax.experimental import pallas as pl
from jax.experimental.pallas import tpu as pltpu

F32 = jnp.float32
BF16 = jnp.bfloat16

CONV_W = 4
C_RG = 8.0
RG_BLOCKS = 8
HG_HEADS = 8
CHUNK = 16
LN_EPS = 1e-5
RMS_EPS = 1e-6

LANES = 128
SUBLANES = 8
MXU_DIM = 256
VMEM_LIMIT_BYTES = 62 * 1024 * 1024

TM_FFN = 928
TF_FFN = 256
TM_PROJ = 1160
TN_PROJ = 1024
TM_MERGE = 464
EXTRACT_ROWS = 1024
PROMPT_TILE_ROWS = 688
SAMPLE_SEQS_PER_STEP = 8

_NT = (((1,), (1,)), ((), ()))
_TN = (((0,), (0,)), ((), ()))
_MASKED = -1e30
LOG2_E = 1.4426950408889634


def _params(semantics):
    return pltpu.CompilerParams(dimension_semantics=semantics, vmem_limit_bytes=VMEM_LIMIT_BYTES)


def _layer_norm(y, g, b):
    mu = jnp.mean(y, axis=-1, keepdims=True)
    d = y - mu
    var = jnp.mean(d * d, axis=-1, keepdims=True)
    return d * lax.rsqrt(var + LN_EPS) * g + b


_ANY = pl.BlockSpec(memory_space=pl.ANY)


def _ffn_body(x_ref, wg_ref, wu_ref, wo_ref, g_ref, b_ref, o_ref, *rest, tf, n_f, f_dim, alpha):
    xb_ref = rest[-1]
    ob_ref = rest[0] if len(rest) == 2 else None
    j = pl.program_id(1)
    tm = o_ref.shape[0]
    overlap = n_f * tf - f_dim

    def contribution(xb, drop_cols):
        gate = jnp.dot(xb, wg_ref[...].astype(BF16), preferred_element_type=F32)
        up = jnp.dot(xb, wu_ref[...].astype(BF16), preferred_element_type=F32)
        h = (0.5 * gate) * jax.nn.sigmoid(gate) * up
        if drop_cols:
            col = lax.broadcasted_iota(jnp.int32, (1, tf), 1)
            h = jnp.where(col >= drop_cols, h, 0.0)
        return jnp.dot(h.astype(BF16), wo_ref[...].astype(BF16), preferred_element_type=F32)

    @pl.when(j == 0)
    def _():
        xb = x_ref[...].astype(BF16)
        xb_ref[...] = xb
        o_ref[...] = alpha * x_ref[...] + contribution(xb, 0)

    @pl.when((j > 0) & (j < n_f - 1))
    def _():
        o_ref[...] += contribution(xb_ref[...], 0)

    @pl.when(j == n_f - 1)
    def _():
        packed = 2 * SUBLANES
        split = (tm // 2) // packed * packed
        for r in (slice(0, split), slice(split, tm)):
            y = o_ref[r, :] + contribution(xb_ref[r, :], overlap)
            o_ref[r, :] = _layer_norm(y, g_ref[...], b_ref[...])
        if ob_ref is not None:
            ob_ref[...] = o_ref[...].astype(BF16)


def _ffn(x, w_in2, w_out2, ln_g, ln_b, wsel, ln_idx, alpha, emit_bf16=False):
    m, d = x.shape
    f_dim = w_in2.shape[1] // 2
    tf, tm = TF_FFN, TM_FFN
    n_f = -(-f_dim // tf)
    assert m % tm == 0 and tm % SUBLANES == 0 and f_dim % LANES == 0 and tf % LANES == 0 and f_dim >= tf
    assert (n_f * tf - f_dim) < tf and d % SUBLANES == 0 and f_dim % SUBLANES == 0
    last = (f_dim - tf) // LANES
    step = tf // LANES

    def chunk(j):
        return jnp.minimum(j * step, last)

    body = functools.partial(_ffn_body, tf=tf, n_f=n_f, f_dim=f_dim, alpha=alpha)
    return pl.pallas_call(
        body,
        grid=(m // tm, n_f),
        in_specs=[
            pl.BlockSpec((tm, d), lambda i, j: (i, 0)),
            pl.BlockSpec((pl.Element(d), pl.Element(tf)), lambda i, j: (wsel * d, chunk(j) * LANES)),
            pl.BlockSpec((pl.Element(d), pl.Element(tf)),
                         lambda i, j: (wsel * d, (f_dim // LANES + chunk(j)) * LANES)),
            pl.BlockSpec((pl.Element(tf), pl.Element(d)),
                         lambda i, j: ((wsel * (f_dim // SUBLANES) + chunk(j) * (LANES // SUBLANES)) * SUBLANES, 0)),
            pl.BlockSpec((None, 1, d), lambda i, j: (ln_idx, 0, 0)),
            pl.BlockSpec((None, 1, d), lambda i, j: (ln_idx, 0, 0)),
        ],
        out_specs=[pl.BlockSpec((tm, d), lambda i, j: (i, 0))] * (2 if emit_bf16 else 1),
        out_shape=[jax.ShapeDtypeStruct((m, d), F32)] + ([jax.ShapeDtypeStruct((m, d), BF16)] if emit_bf16 else []),
        scratch_shapes=[pltpu.VMEM((tm, d), BF16)],
        compiler_params=_params(("parallel", "arbitrary")),
        name=f"ffn_{wsel}",
    )(x, w_in2, w_in2, w_out2, ln_g, ln_b)


PROJ_ORDER = (6, 7, 8, 9, 3, 0, 1, 2, 4, 5)
UF_LOGF, UF_XR = 0, 1
UB_SGA, UB_SGB, UB_K, UB_GR, UB_Q, UB_V, UB_OG = 0, 1, 4, 5, 6, 7, 8


def _forget_lower_bound(hg_lb, layer):
    e = jnp.exp(hg_lb - jnp.max(hg_lb, axis=0, keepdims=True))
    sm = e / jnp.sum(e, axis=0, keepdims=True)
    cum = sm[0:1, :]
    for m in range(1, layer + 1):
        cum = cum + sm[m:m + 1, :]
    return cum - sm[0:1, :]


def _proj_body(xb_ref, w_ref, lb_ref, uf_ref, ub_ref, *, layer):
    j = pl.program_id(1)

    def acc():
        return jnp.dot(xb_ref[...], w_ref[...].astype(BF16), preferred_element_type=F32)

    col = [PROJ_ORDER.index(cb) for cb in range(len(PROJ_ORDER))]

    @pl.when(j <= col[9])
    def _():
        ub_ref[...] = jax.nn.sigmoid(acc()).astype(BF16)

    @pl.when(j == col[3])
    def _():
        lb = _forget_lower_bound(lb_ref[...], layer)
        f = lb + (1.0 - lb) * jax.nn.sigmoid(acc())
        uf_ref[...] = jnp.log(f)
        ub_ref[...] = (1.0 - f).astype(BF16)

    @pl.when(j == col[0])
    def _():
        uf_ref[...] = acc()

    @pl.when(j == col[1])
    def _():
        ub_ref[...] = jax.nn.gelu(acc()).astype(BF16)

    @pl.when((j == col[2]) | (j == col[4]))
    def _():
        ub_ref[...] = acc().astype(BF16)

    @pl.when(j == col[5])
    def _():
        a = acc()
        ub_ref[...] = (a * jax.nn.sigmoid(a)).astype(BF16)


def _proj(xb, w, hg_lb, l):
    m, d = xb.shape
    tm, tn = TM_PROJ, TN_PROJ
    n_blk = w.shape[-1] // tn
    assert m % tm == 0 and w.shape[-1] % tn == 0 and n_blk == len(PROJ_ORDER)
    assert PROJ_ORDER[:4] == (6, 7, 8, 9) and PROJ_ORDER[4:6] == (3, 0)

    def wcol(j):
        cb = jnp.int32(PROJ_ORDER[-1])
        for step in range(n_blk - 1):
            cb = jnp.where(j == step, PROJ_ORDER[step], cb)
        return cb

    body = functools.partial(_proj_body, layer=l)
    return pl.pallas_call(
        body,
        grid=(m // tm, n_blk),
        in_specs=[
            pl.BlockSpec((tm, d), lambda i, j: (i, 0)),
            pl.BlockSpec((None, d, tn), lambda i, j: (l, 0, wcol(j))),
            pl.BlockSpec(hg_lb.shape, lambda i, j: (0, 0)),
        ],
        out_specs=[
            pl.BlockSpec((tm, tn), lambda i, j: (i, jnp.where(j >= 5, 1, 0))),
            pl.BlockSpec((tm, tn), lambda i, j: (i, jnp.where(j >= 5, j - 1, j))),
        ],
        out_shape=[jax.ShapeDtypeStruct((m, 2 * tn), F32), jax.ShapeDtypeStruct((m, (n_blk - 1) * tn), BF16)],
        compiler_params=_params(("parallel", "arbitrary")),
        name=f"proj_l{l}",
    )(xb, w, hg_lb)


def _rglru_body(*refs, n_seq, t_rows, n_t, n_alias):
    xr_ref, gr_ref, c0_ref, h0_ref, cw_ref, cb_ref, wa_ref, ba_ref, wi_ref, bi_ref, lam_ref = refs[:11]
    ya_ref, hl_ref, cn_ref, xbuf, a_scr, b_scr, hcar = refs[11 + n_alias:]
    t = pl.program_id(1)
    c = a_scr.shape[1]
    hist = CONV_W - 1
    base = SUBLANES - hist

    @pl.when(t == 0)
    def _():
        for g in range(n_seq):
            xbuf[g, base:SUBLANES, :] = c0_ref[g]
            hcar[g] = h0_ref[g]

    for g in range(n_seq):
        xbuf[g, SUBLANES:SUBLANES + t_rows, :] = xr_ref[g * t_rows:(g + 1) * t_rows, :]

    log_a_scale = -C_RG * jax.nn.softplus(-lam_ref[...])
    bs = c // RG_BLOCKS
    rows_total = n_seq * t_rows
    grp = (rows_total // SUBLANES, SUBLANES, bs)

    def row_vec(v):
        return jnp.broadcast_to(v, (SUBLANES, bs))[None]

    def sigmoid(z):
        return 0.5 * jnp.tanh(0.5 * z) + 0.5

    for n in range(RG_BLOCKS):
        ls = slice(n * bs, (n + 1) * bs)
        taps = [row_vec(cw_ref[j:j + 1, ls]) for j in range(CONV_W)]
        for g in range(n_seq):
            xc = row_vec(cb_ref[:, ls]) + sum(
                xbuf[g, base + j:base + j + t_rows, ls].reshape(t_rows // SUBLANES, SUBLANES, bs) * taps[j]
                for j in range(CONV_W))
            a_scr[g * t_rows:(g + 1) * t_rows, ls] = xc.reshape(t_rows, bs)
        xc = a_scr[:, ls]
        xcb = xc.astype(BF16)
        xc = xc.reshape(grp)
        r = sigmoid(jnp.dot(xcb, wa_ref[n], preferred_element_type=F32).reshape(grp) + row_vec(ba_ref[:, ls]))
        i = sigmoid(jnp.dot(xcb, wi_ref[n], preferred_element_type=F32).reshape(grp) + row_vec(bi_ref[:, ls]))
        log_a = row_vec(log_a_scale[:, ls]) * r
        a = jnp.exp(log_a)
        m2 = jnp.tanh(-log_a) * (a * a + 1.0)
        mult = jnp.where(m2 > 0.0, m2 * lax.rsqrt(m2), 0.0)
        a_scr[:, ls] = a.reshape(rows_total, bs)
        b_scr[:, ls] = (mult * (i * xc)).reshape(rows_total, bs)

    for g in range(n_seq):
        xbuf[g, base:SUBLANES, :] = xbuf[g, base + t_rows:SUBLANES + t_rows, :]

    row = lax.broadcasted_iota(jnp.int32, (SUBLANES, c), 0)
    for g in range(n_seq):
        def group(j, h_in, g=g):
            r0 = pl.multiple_of(g * t_rows + j * SUBLANES, SUBLANES)
            a = a_scr[pl.ds(r0, SUBLANES), :]
            b = b_scr[pl.ds(r0, SUBLANES), :]
            for s in (1, 2, 4):
                keep = row >= s
                b = jnp.where(keep, a * pltpu.roll(b, s, 0) + b, b)
                a = jnp.where(keep, a * pltpu.roll(a, s, 0), a)
            h = a * h_in + b
            a_scr[pl.ds(r0, SUBLANES), :] = h
            return h[SUBLANES - 1:SUBLANES, :]

        hcar[g] = lax.fori_loop(0, t_rows // SUBLANES, group, hcar[g])

    for n in range(RG_BLOCKS):
        ls = slice(n * bs, (n + 1) * bs)
        ya_ref[:, ls] = (a_scr[:, ls] * gr_ref[:, ls].astype(F32)).astype(BF16)

    @pl.when(t == n_t - 1)
    def _():
        for g in range(n_seq):
            hl_ref[g] = hcar[g]
            cn_ref[g] = xbuf[g, base:SUBLANES, :]


def _with_aliases(n_in, prev):
    specs, args, amap = [], [], {}
    for out_idx, p in enumerate(prev):
        if p is not None:
            amap[n_in + len(args)] = out_idx
            specs.append(_ANY)
            args.append(p)
    return specs, args, amap


def _rglru(uf, ub, conv0, h0, conv_w, conv_b, wa_b, ba, wi_b, bi, lam, l, prev, *,
           depth, total_rows, row0, n_batch, seq_len, n_seq, t_rows, tag):
    c = conv_w.shape[-1]
    n_t = seq_len // t_rows
    r = n_seq * t_rows
    assert seq_len % t_rows == 0 and n_batch % n_seq == 0 and row0 % r == 0 and t_rows % SUBLANES == 0
    assert n_seq == 1 or n_t == 1
    blk0 = row0 // r
    hist = CONV_W - 1
    a_specs, a_args, amap = _with_aliases(11, prev)
    body = functools.partial(_rglru_body, n_seq=n_seq, t_rows=t_rows, n_t=n_t, n_alias=len(a_args))
    vec = pl.BlockSpec((None, 1, c), lambda s, t: (l, 0, 0))
    gate_w = pl.BlockSpec((None, RG_BLOCKS, c // RG_BLOCKS, c // RG_BLOCKS), lambda s, t: (l, 0, 0, 0))
    return pl.pallas_call(
        body,
        grid=(n_batch // n_seq, n_t),
        in_specs=[
            pl.BlockSpec((r, c), lambda s, t: (blk0 + s * n_t + t, UF_XR)),
            pl.BlockSpec((r, c), lambda s, t: (blk0 + s * n_t + t, UB_GR)),
            pl.BlockSpec((n_seq, hist, c), lambda s, t: (s, 0, 0)),
            pl.BlockSpec((n_seq, 1, c), lambda s, t: (s, 0, 0)),
            pl.BlockSpec((None, CONV_W, c), lambda s, t: (l, 0, 0)),
            vec, gate_w, vec, gate_w, vec, vec,
        ] + a_specs,
        out_specs=[
            pl.BlockSpec((r, c), lambda s, t: (blk0 + s * n_t + t, 0)),
            pl.BlockSpec((None, n_seq, 1, c), lambda s, t: (l, s, 0, 0)),
            pl.BlockSpec((None, n_seq, hist, c), lambda s, t: (l, s, 0, 0)),
        ],
        out_shape=[
            jax.ShapeDtypeStruct((total_rows, c), BF16),
            jax.ShapeDtypeStruct((depth, n_batch, 1, c), F32),
            jax.ShapeDtypeStruct((depth, n_batch, hist, c), F32),
        ],
        scratch_shapes=[
            pltpu.VMEM((n_seq, SUBLANES + t_rows, c), F32),
            pltpu.VMEM((r, c), F32),
            pltpu.VMEM((r, c), F32),
            pltpu.VMEM((n_seq, 1, c), F32),
        ],
        input_output_aliases=amap,
        compiler_params=_params(("arbitrary", "arbitrary")),
        name=f"rglru_{tag}_l{l}",
    )(uf, ub, conv0, h0, conv_w, conv_b, wa_b, ba, wi_b, bi, lam, *a_args)


def _hgrn_body(*refs, n_seq, t_rows, n_t, chunk, n_alias):
    lf_ref, k_ref, q_ref, v_ref, og_ref, s0_ref, gn_ref = refs[:7]
    yb_ref, so_ref, st_scr, p_scr, yo_scr = refs[7 + n_alias:]
    t = pl.program_id(1)
    c = q_ref.shape[1]
    dk = c // HG_HEADS
    n_lane_blk = c // MXU_DIM
    n_rb = chunk // SUBLANES
    n_units = sum(n_rb - s // SUBLANES for s in range(chunk))
    assert n_units % 2 == 0
    n_tiles = n_units // 2

    single_chunk = t_rows == chunk

    if not single_chunk:
        @pl.when(t == 0)
        def _():
            for g in range(n_seq):
                for h in range(HG_HEADS):
                    st_scr[g, h] = s0_ref[g, h].T

    gn = gn_ref[...]

    row_c = lax.broadcasted_iota(jnp.int32, (chunk, c), 0)
    row_8 = lax.broadcasted_iota(jnp.int32, (SUBLANES, MXU_DIM), 0)
    shifts = [s for s in (1, 2, 4, 8, 16) if s < chunk]
    ri = lax.broadcasted_iota(jnp.int32, (MXU_DIM, MXU_DIM), 0)
    ci_ = lax.broadcasted_iota(jnp.int32, (MXU_DIM, MXU_DIM), 1)
    head_ones = ((ri // dk) == (ci_ // dk)).astype(BF16)

    packed_rows = 2 * SUBLANES

    row_h = lax.broadcasted_iota(jnp.int32, (SUBLANES, dk), 0)
    ones_h = jnp.ones((SUBLANES, dk), BF16)

    def chunk_body(g, ci):
        r0 = g * t_rows if single_chunk else pl.multiple_of(g * t_rows + ci * chunk, chunk)
        rows = pl.ds(r0, chunk)

        def load(ref):
            if chunk % packed_rows == 0:
                return ref[rows, :].astype(F32)
            tile0 = (r0 // packed_rows) * packed_rows
            return ref[tile0:tile0 + packed_rows, :].astype(F32)[r0 - tile0:r0 - tile0 + chunk, :]

        b = lf_ref[rows, :] * LOG2_E
        for s in shifts:
            b = b + jnp.where(row_c >= s, pltpu.roll(b, s, 0), 0.0)
        k = load(k_ref)
        q = load(q_ref)
        v = load(v_ref)

        cs = b - jnp.log2(k)
        b_last = b[chunk - 1:chunk, :]
        qe = (q * jnp.exp2(b)).astype(BF16)
        kd = (k * jnp.exp2(b_last - b)).astype(BF16)
        e_last = jnp.exp2(b_last)
        vb = v.astype(BF16)
        units = [(s, tb) for s in range(chunk) for tb in range(s // SUBLANES, n_rb)]

        outs = []
        for j in range(n_lane_blk):
            lj = slice(j * MXU_DIM, (j + 1) * MXU_DIM)
            bb = [b[i * SUBLANES:(i + 1) * SUBLANES, lj] for i in range(n_rb)]
            qq = [q[i * SUBLANES:(i + 1) * SUBLANES, lj] for i in range(n_rb)]
            prods = []
            for s in range(chunk):
                cs_s = jnp.broadcast_to(cs[s:s + 1, lj], (SUBLANES, MXU_DIM))
                for tb in range(s // SUBLANES, n_rb):
                    d = bb[tb] - cs_s
                    if tb == s // SUBLANES:
                        d = jnp.where(row_8 >= (s % SUBLANES), d, _MASKED)
                    prods.append(qq[tb] * jnp.exp2(d))
                    if len(prods) == 2:
                        ti = units.index((s, tb)) // 2
                        p_scr[g, j, ti * 16:(ti + 1) * 16, :] = jnp.concatenate(prods, axis=0).astype(BF16)
                        prods = []
            a_cols = jnp.dot(p_scr[g, j], head_ones, preferred_element_type=F32)
            intra = [jnp.zeros((SUBLANES, MXU_DIM), F32) for _ in range(n_rb)]
            for s in range(chunk):
                v_s = jnp.broadcast_to(v[s:s + 1, lj], (SUBLANES, MXU_DIM))
                for tb in range(s // SUBLANES, n_rb):
                    un = units.index((s, tb))
                    intra[tb] = intra[tb] + a_cols[un * SUBLANES:(un + 1) * SUBLANES, :] * v_s
            intra = jnp.concatenate(intra, axis=0) if n_rb > 1 else intra[0]
            for h in range(j * (MXU_DIM // dk), (j + 1) * (MXU_DIM // dk)):
                ls = slice(h * dk, (h + 1) * dk)
                li = slice(h * dk - j * MXU_DIM, (h + 1) * dk - j * MXU_DIM)
                if single_chunk:
                    s_in = s0_ref[g, h]
                    inter = jnp.dot(qe[:, ls], s_in.astype(BF16), preferred_element_type=F32)
                    upd = lax.dot_general(kd[:, ls], vb[:, ls], _TN, preferred_element_type=F32)
                    e_hi = e_last[:, ls].astype(BF16)
                    rem = e_last[:, ls] - e_hi.astype(F32)
                    e_mid = rem.astype(BF16)
                    e_lo = (rem - e_mid.astype(F32)).astype(BF16)
                    pieces = jnp.where(row_h == 0, e_hi.astype(F32), jnp.where(
                        row_h == 1, e_mid.astype(F32), jnp.where(row_h == 2, e_lo.astype(F32), 0.0)))
                    decay = lax.dot_general(pieces.astype(BF16), ones_h, _TN, preferred_element_type=F32)
                    so_ref[g, h] = s_in * decay + upd
                else:
                    st = st_scr[g, h]
                    inter = lax.dot_general(qe[:, ls], st.astype(BF16), _NT, preferred_element_type=F32)
                    upd = lax.dot_general(vb[:, ls], kd[:, ls], _TN, preferred_element_type=F32)
                    decay = jnp.broadcast_to(e_last[:, ls], (SUBLANES, dk))
                    new = (st.reshape(dk // SUBLANES, SUBLANES, dk) * decay[None]
                           + upd.reshape(dk // SUBLANES, SUBLANES, dk))
                    st_scr[g, h] = new.reshape(dk, dk)
                o = intra[:, li] + inter
                outs.append(o * lax.rsqrt(jnp.mean(o * o, axis=-1, keepdims=True) + RMS_EPS))
        yo_scr[rows, :] = jnp.concatenate(outs, axis=-1) * gn * load(og_ref)

    if single_chunk:
        for g in range(n_seq):
            chunk_body(g, 0)
    else:
        def step(ci, carry):
            for g in range(n_seq):
                chunk_body(g, ci)
            return carry

        lax.fori_loop(0, t_rows // chunk, step, 0)

    yb_ref[...] = yo_scr[...].astype(BF16)

    if not single_chunk:
        @pl.when(t == n_t - 1)
        def _():
            for g in range(n_seq):
                for h in range(HG_HEADS):
                    so_ref[g, h] = st_scr[g, h].T


def _hgrn(uf, ub, s0, s0_layer, gn, l, prev, *, depth, total_rows, row0, n_batch, seq_len, n_seq, t_rows, chunk, tag):
    c = gn.shape[-1]
    dk = c // HG_HEADS
    n_t = seq_len // t_rows
    r = n_seq * t_rows
    assert seq_len % t_rows == 0 and n_batch % n_seq == 0 and row0 % r == 0 and t_rows % chunk == 0
    assert n_seq == 1 or n_t == 1
    assert t_rows != chunk or n_t == 1
    assert chunk % SUBLANES == 0 and c % MXU_DIM == 0 and MXU_DIM % dk == 0
    blk0 = row0 // r
    n_tiles = sum(chunk // SUBLANES - s // SUBLANES for s in range(chunk)) // 2
    a_specs, a_args, amap = _with_aliases(7, prev)
    body = functools.partial(_hgrn_body, n_seq=n_seq, t_rows=t_rows, n_t=n_t, chunk=chunk, n_alias=len(a_args))

    def col(cb):
        return pl.BlockSpec((r, c), lambda s, t: (blk0 + s * n_t + t, cb))

    return pl.pallas_call(
        body,
        grid=(n_batch // n_seq, n_t),
        in_specs=[
            col(UF_LOGF), col(UB_K), col(UB_Q), col(UB_V), col(UB_OG),
            pl.BlockSpec((None, n_seq, HG_HEADS, dk, dk), lambda s, t: (s0_layer, s, 0, 0, 0)),
            pl.BlockSpec((None, 1, c), lambda s, t: (l, 0, 0)),
        ] + a_specs,
        out_specs=[
            pl.BlockSpec((r, c), lambda s, t: (blk0 + s * n_t + t, 0)),
            pl.BlockSpec((None, n_seq, HG_HEADS, dk, dk), lambda s, t: (l, s, 0, 0, 0)),
        ],
        out_shape=[
            jax.ShapeDtypeStruct((total_rows, c), BF16),
            jax.ShapeDtypeStruct((depth, n_batch, HG_HEADS, dk, dk), F32),
        ],
        scratch_shapes=[
            pltpu.VMEM((n_seq, HG_HEADS, dk, dk), F32),
            pltpu.VMEM((n_seq, c // MXU_DIM, n_tiles * 16, MXU_DIM), BF16),
            pltpu.VMEM((r, c), F32),
        ],
        input_output_aliases=amap,
        compiler_params=_params(("arbitrary", "arbitrary")),
        name=f"hgrn_{tag}_l{l}",
    )(uf, ub, ub, ub, ub, s0, gn, *a_args)


def _merge_body(x_ref, ya_ref, yb_ref, ga_ref, gb_ref, wpa_ref, wpb_ref, wo_ref, g_ref, b_ref, o_ref, *, alpha):
    tm = o_ref.shape[0]
    packed = 2 * SUBLANES
    split = (tm // 2) // packed * packed
    for r in (slice(0, split), slice(split, tm)):
        pa = jnp.dot(ya_ref[r, :], wpa_ref[...], preferred_element_type=F32)
        pb = jnp.dot(yb_ref[r, :], wpb_ref[...], preferred_element_type=F32)
        merged = ga_ref[r, :].astype(F32) * pa + gb_ref[r, :].astype(F32) * pb
        m = jnp.dot(merged.astype(BF16), wo_ref[...], preferred_element_type=F32)
        o_ref[r, :] = _layer_norm(alpha * x_ref[r, :] + m, g_ref[...], b_ref[...])


def _merge(x, ya, yb, ub, wpa_b, wpb_b, wo_b, ln_g, ln_b, l, ln_idx, alpha):
    m, d = x.shape
    c = ya.shape[1]
    tm = TM_MERGE
    assert m % tm == 0
    body = functools.partial(_merge_body, alpha=alpha)
    once = pl.Buffered(1)
    return pl.pallas_call(
        body,
        grid=(m // tm,),
        in_specs=[
            pl.BlockSpec((tm, d), lambda i: (i, 0)),
            pl.BlockSpec((tm, c), lambda i: (i, 0)),
            pl.BlockSpec((tm, c), lambda i: (i, 0)),
            pl.BlockSpec((tm, d), lambda i: (i, UB_SGA)),
            pl.BlockSpec((tm, d), lambda i: (i, UB_SGB)),
            pl.BlockSpec((None, c, d), lambda i: (l, 0, 0), pipeline_mode=once),
            pl.BlockSpec((None, c, d), lambda i: (l, 0, 0), pipeline_mode=once),
            pl.BlockSpec((None, d, d), lambda i: (l, 0, 0), pipeline_mode=once),
            pl.BlockSpec((None, 1, d), lambda i: (ln_idx, 0, 0)),
            pl.BlockSpec((None, 1, d), lambda i: (ln_idx, 0, 0)),
        ],
        out_specs=pl.BlockSpec((tm, d), lambda i: (i, 0)),
        out_shape=jax.ShapeDtypeStruct((m, d), F32),
        compiler_params=_params(("parallel",)),
        name=f"merge_l{l}",
    )(x, ya, yb, ub, ub, wpa_b, wpb_b, wo_b, ln_g, ln_b)


def _assemble_body(xp_ref, xs_ref, o_ref, *, n_prompt_blocks):
    i = pl.program_id(0)

    @pl.when(i < n_prompt_blocks)
    def _():
        o_ref[...] = xp_ref[...]

    @pl.when(i >= n_prompt_blocks)
    def _():
        o_ref[...] = xs_ref[...]


def _meta_body(m_ref, prev_ref, o_ref):
    del prev_ref
    o_ref[...] = m_ref[...]


def _assemble(x_prompt, x_sample2, meta):
    batch, seq, d = x_prompt.shape
    n_meta = meta.shape[0]
    p_len = n_meta + seq
    s_rows = x_sample2.shape[0]
    tr = EXTRACT_ROWS
    per_seq = seq // tr
    n_pb = batch * per_seq
    assert seq % tr == 0 and s_rows % tr == 0 and p_len % SUBLANES == 0 and n_meta % SUBLANES == 0
    total = batch * p_len + s_rows

    def out_row(i):
        ip = jnp.minimum(i, n_pb - 1)
        prompt = (ip // per_seq) * (p_len // SUBLANES) + n_meta // SUBLANES + (ip % per_seq) * (tr // SUBLANES)
        sample = (batch * p_len) // SUBLANES + (i - n_pb) * (tr // SUBLANES)
        return jnp.where(i < n_pb, prompt, sample) * SUBLANES

    x = pl.pallas_call(
        functools.partial(_assemble_body, n_prompt_blocks=n_pb),
        grid=(n_pb + s_rows // tr,),
        in_specs=[
            pl.BlockSpec((None, tr, d), lambda i: (jnp.minimum(i, n_pb - 1) // per_seq,
                                                   jnp.minimum(i, n_pb - 1) % per_seq, 0)),
            pl.BlockSpec((tr, d), lambda i: (jnp.maximum(i - n_pb, 0), 0)),
        ],
        out_specs=pl.BlockSpec((pl.Element(tr), pl.Element(d)), lambda i: (out_row(i), 0)),
        out_shape=jax.ShapeDtypeStruct((total, d), x_prompt.dtype),
        compiler_params=_params(("arbitrary",)),
        name="assemble",
    )(x_prompt, x_sample2)
    return pl.pallas_call(
        _meta_body,
        grid=(batch,),
        in_specs=[pl.BlockSpec((n_meta, d), lambda b: (0, 0)), _ANY],
        out_specs=pl.BlockSpec((pl.Element(n_meta), pl.Element(d)), lambda b: (b * (p_len // SUBLANES) * SUBLANES, 0)),
        out_shape=jax.ShapeDtypeStruct((total, d), x_prompt.dtype),
        input_output_aliases={1: 0},
        compiler_params=_params(("arbitrary",)),
        name="assemble_meta",
    )(meta, x)


def _copy_body(x_ref, o_ref):
    o_ref[...] = x_ref[...]


def _extract_prompt(x, batch, p_len, n_meta):
    d = x.shape[1]
    seq = p_len - n_meta
    tr = EXTRACT_ROWS
    assert seq % tr == 0 and p_len % SUBLANES == 0 and n_meta % SUBLANES == 0 and tr % SUBLANES == 0
    return pl.pallas_call(
        _copy_body,
        grid=(batch, seq // tr),
        in_specs=[pl.BlockSpec(
            (pl.Element(tr), pl.Element(d)),
            lambda b, h: ((b * (p_len // SUBLANES) + n_meta // SUBLANES + h * (tr // SUBLANES)) * SUBLANES, 0))],
        out_specs=pl.BlockSpec((None, tr, d), lambda b, h: (b, h, 0)),
        out_shape=jax.ShapeDtypeStruct((batch, seq, d), x.dtype),
        compiler_params=_params(("parallel", "parallel")),
        name="extract_prompt",
    )(x)


def kernel(x_prompt, x_sample, state_rglru_h, state_rglru_conv, state_hgrn, meta_tokens, ln_g, ln_b, ffn_w_in, ffn_w_out, w_in, conv_w, conv_b, rg_wa, rg_ba, rg_wi, rg_bi, rg_lambda, hg_lb, hg_norm_g, w_pa, w_pb, w_o):
    batch, seq, d_model = x_prompt.shape
    dec_batch, dec_seq, _ = x_sample.shape
    depth = w_in.shape[0]
    n_meta = meta_tokens.shape[0]
    d_rnn = conv_w.shape[-1]
    dk = d_rnn // HG_HEADS
    d_ff = ffn_w_out.shape[-2]
    alpha = (2.0 * depth) ** 0.25
    dt = x_prompt.dtype

    p_len = n_meta + seq
    p_rows = batch * p_len
    total_rows = p_rows + dec_batch * dec_seq
    x = _assemble(x_prompt, x_sample.reshape(dec_batch * dec_seq, d_model), meta_tokens.astype(dt))

    w_in2 = ffn_w_in.reshape(depth * 2 * d_model, 2 * d_ff)
    w_out2 = ffn_w_out.reshape(depth * 2 * d_ff, d_model)
    wpa_b, wpb_b, wo_b = w_pa.astype(BF16), w_pb.astype(BF16), w_o.astype(BF16)
    wa_b, wi_b = rg_wa.astype(BF16), rg_wi.astype(BF16)
    ln_g3 = ln_g.reshape(depth * 3, 1, d_model)
    ln_b3 = ln_b.reshape(depth * 3, 1, d_model)
    vec = lambda a: a.reshape(depth, 1, a.shape[-1])
    conv_b3, ba3, bi3, lam3, gn3 = vec(conv_b), vec(rg_ba), vec(rg_bi), vec(rg_lambda), vec(hg_norm_g)

    zeros_h = jnp.zeros((batch, 1, d_rnn), dt)
    zeros_c = jnp.zeros((batch, CONV_W - 1, d_rnn), dt)
    zeros_s = jnp.zeros((1, batch, HG_HEADS, dk, dk), dt)

    common = dict(depth=depth, total_rows=total_rows)
    prompt = dict(row0=0, n_batch=batch, seq_len=p_len, n_seq=1, t_rows=PROMPT_TILE_ROWS, tag="p", **common)
    sample = dict(row0=p_rows, n_batch=dec_batch, seq_len=dec_seq, n_seq=SAMPLE_SEQS_PER_STEP, t_rows=dec_seq,
                  tag="s", **common)

    ph = pc = ps = sh = sc = ss = None
    for l in range(depth):
        x, xb = _ffn(x, w_in2, w_out2, ln_g3, ln_b3, 2 * l, 3 * l, alpha, emit_bf16=True)
        uf, ub = _proj(xb, w_in, hg_lb, l)
        rg = (conv_w, conv_b3, wa_b, ba3, wi_b, bi3, lam3, l)
        ya, ph, pc = _rglru(uf, ub, zeros_c, zeros_h, *rg, (None, ph, pc), **prompt)
        ya, sh, sc = _rglru(uf, ub, state_rglru_conv[l], state_rglru_h[l][:, None, :], *rg, (ya, sh, sc), **sample)
        yb, ps = _hgrn(uf, ub, zeros_s, 0, gn3, l, (None, ps), chunk=CHUNK, **prompt)
        yb, ss = _hgrn(uf, ub, state_hgrn, l, gn3, l, (yb, ss), chunk=dec_seq, **sample)
        x = _merge(x, ya, yb, ub, wpa_b, wpb_b, wo_b, ln_g3, ln_b3, l, 3 * l + 1, alpha)
        (x,) = _ffn(x, w_in2, w_out2, ln_g3, ln_b3, 2 * l + 1, 3 * l + 2, alpha)

    y_prompt = _extract_prompt(x, batch, p_len, n_meta)
    y_sample = x[p_rows:].reshape(dec_batch, dec_seq, d_model)
    return (y_prompt, y_sample, ph[:, :, 0], pc, ps, sh[:, :, 0], sc, ss)
```

```python
import functools

import jax
import jax.numpy as jnp
from jax import lax
from jax.experimental import pallas as pl
from jax.experimental.pallas import tpu as pltpu

F32 = jnp.float32
BF16 = jnp.bfloat16

CONV_W = 4
C_RG = 8.0
RG_BLOCKS = 8
HG_HEADS = 8
CHUNK = 16
LN_EPS = 1e-5
RMS_EPS = 1e-6

LANES = 128
SUBLANES = 8
MXU_DIM = 256
VMEM_LIMIT_BYTES = 62 * 1024 * 1024

TM_FFN = 928
TF_FFN = 256
TM_PROJ = 1160
TN_PROJ = 1024
TM_MERGE = 464
EXTRACT_ROWS = 1024
FFN_ROW_BLOCKS = 2
MERGE_ROW_BLOCKS = 2
PROMPT_TILE_ROWS = 688
SAMPLE_SEQS_PER_STEP = 8

_NT = (((1,), (1,)), ((), ()))
_TN = (((0,), (0,)), ((), ()))
_MASKED = -1e30
LOG2_E = 1.4426950408889634


def _params(semantics):
    return pltpu.CompilerParams(dimension_semantics=semantics, vmem_limit_bytes=VMEM_LIMIT_BYTES)


def _row_blocks(n_rows, n_blocks):
    packed = 2 * SUBLANES
    tiles = -(-n_rows // packed)
    cuts = [min(n_rows, (tiles * i // n_blocks) * packed) for i in range(n_blocks)] + [n_rows]
    return [slice(a, b) for a, b in zip(cuts[:-1], cuts[1:]) if b > a]


def _layer_norm(y, g, b):
    mu = jnp.mean(y, axis=-1, keepdims=True)
    d = y - mu
    var = jnp.mean(d * d, axis=-1, keepdims=True)
    return d * lax.rsqrt(var + LN_EPS) * g + b


_ANY = pl.BlockSpec(memory_space=pl.ANY)


def _ffn_body(x_ref, wg_ref, wu_ref, wo_ref, g_ref, b_ref, o_ref, *rest, tf, n_f, f_dim, alpha):
    xb_ref = rest[-1]
    ob_ref = rest[0] if len(rest) == 2 else None
    j = pl.program_id(1)
    tm = o_ref.shape[0]
    overlap = n_f * tf - f_dim

    def contribution(xb, drop_cols):
        gate = jnp.dot(xb, wg_ref[...].astype(BF16), preferred_element_type=F32)
        up = jnp.dot(xb, wu_ref[...].astype(BF16), preferred_element_type=F32)
        h = (0.5 * gate) * jax.nn.sigmoid(gate) * up
        if drop_cols:
            col = lax.broadcasted_iota(jnp.int32, (1, tf), 1)
            h = jnp.where(col >= drop_cols, h, 0.0)
        return jnp.dot(h.astype(BF16), wo_ref[...].astype(BF16), preferred_element_type=F32)

    @pl.when(j == 0)
    def _():
        xb = x_ref[...].astype(BF16)
        xb_ref[...] = xb
        o_ref[...] = alpha * x_ref[...] + contribution(xb, 0)

    @pl.when((j > 0) & (j < n_f - 1))
    def _():
        o_ref[...] += contribution(xb_ref[...], 0)

    @pl.when(j == n_f - 1)
    def _():
        for r in _row_blocks(tm, FFN_ROW_BLOCKS):
            y = o_ref[r, :] + contribution(xb_ref[r, :], overlap)
            o_ref[r, :] = _layer_norm(y, g_ref[...], b_ref[...])
        if ob_ref is not None:
            ob_ref[...] = o_ref[...].astype(BF16)


def _ffn(x, w_in2, w_out2, ln_g, ln_b, wsel, ln_idx, alpha, emit_bf16=False):
    m, d = x.shape
    f_dim = w_in2.shape[1] // 2
    tf, tm = TF_FFN, TM_FFN
    n_f = -(-f_dim // tf)
    assert m % tm == 0 and tm % SUBLANES == 0 and f_dim % LANES == 0 and tf % LANES == 0 and f_dim >= tf
    assert (n_f * tf - f_dim) < tf and d % SUBLANES == 0 and f_dim % SUBLANES == 0
    last = (f_dim - tf) // LANES
    step = tf // LANES

    def chunk(j):
        return jnp.minimum(j * step, last)

    body = functools.partial(_ffn_body, tf=tf, n_f=n_f, f_dim=f_dim, alpha=alpha)
    return pl.pallas_call(
        body,
        grid=(m // tm, n_f),
        in_specs=[
            pl.BlockSpec((tm, d), lambda i, j: (i, 0)),
            pl.BlockSpec((pl.Element(d), pl.Element(tf)), lambda i, j: (wsel * d, chunk(j) * LANES)),
            pl.BlockSpec((pl.Element(d), pl.Element(tf)),
                         lambda i, j: (wsel * d, (f_dim // LANES + chunk(j)) * LANES)),
            pl.BlockSpec((pl.Element(tf), pl.Element(d)),
                         lambda i, j: ((wsel * (f_dim // SUBLANES) + chunk(j) * (LANES // SUBLANES)) * SUBLANES, 0)),
            pl.BlockSpec((None, 1, d), lambda i, j: (ln_idx, 0, 0)),
            pl.BlockSpec((None, 1, d), lambda i, j: (ln_idx, 0, 0)),
        ],
        out_specs=[pl.BlockSpec((tm, d), lambda i, j: (i, 0))] * (2 if emit_bf16 else 1),
        out_shape=[jax.ShapeDtypeStruct((m, d), F32)] + ([jax.ShapeDtypeStruct((m, d), BF16)] if emit_bf16 else []),
        scratch_shapes=[pltpu.VMEM((tm, d), BF16)],
        compiler_params=_params(("parallel", "arbitrary")),
        name=f"ffn_{wsel}",
    )(x, w_in2, w_in2, w_out2, ln_g, ln_b)


PROJ_ORDER = (6, 7, 8, 9, 3, 0, 1, 2, 4, 5)
UF_LOGF, UF_XR = 0, 1
UB_SGA, UB_SGB, UB_K, UB_GR, UB_Q, UB_V, UB_OG = 0, 1, 4, 5, 6, 7, 8


def _forget_lower_bound(hg_lb, layer):
    e = jnp.exp(hg_lb - jnp.max(hg_lb, axis=0, keepdims=True))
    sm = e / jnp.sum(e, axis=0, keepdims=True)
    cum = sm[0:1, :]
    for m in range(1, layer + 1):
        cum = cum + sm[m:m + 1, :]
    return cum - sm[0:1, :]


def _proj_body(xb_ref, w_ref, lb_ref, uf_ref, ub_ref, *, layer):
    j = pl.program_id(1)

    def acc():
        return jnp.dot(xb_ref[...], w_ref[...].astype(BF16), preferred_element_type=F32)

    col = [PROJ_ORDER.index(cb) for cb in range(len(PROJ_ORDER))]

    @pl.when(j <= col[9])
    def _():
        ub_ref[...] = jax.nn.sigmoid(acc()).astype(BF16)

    @pl.when(j == col[3])
    def _():
        lb = _forget_lower_bound(lb_ref[...], layer)
        f = lb + (1.0 - lb) * jax.nn.sigmoid(acc())
        uf_ref[...] = jnp.log(f)
        ub_ref[...] = (1.0 - f).astype(BF16)

    @pl.when(j == col[0])
    def _():
        uf_ref[...] = acc()

    @pl.when(j == col[1])
    def _():
        ub_ref[...] = jax.nn.gelu(acc()).astype(BF16)

    @pl.when((j == col[2]) | (j == col[4]))
    def _():
        ub_ref[...] = acc().astype(BF16)

    @pl.when(j == col[5])
    def _():
        a = acc()
        ub_ref[...] = (a * jax.nn.sigmoid(a)).astype(BF16)


def _proj(xb, w, hg_lb, l):
    m, d = xb.shape
    tm, tn = TM_PROJ, TN_PROJ
    n_blk = w.shape[-1] // tn
    assert m % tm == 0 and w.shape[-1] % tn == 0 and n_blk == len(PROJ_ORDER)
    assert PROJ_ORDER[:4] == (6, 7, 8, 9) and PROJ_ORDER[4:6] == (3, 0)

    def wcol(j):
        cb = jnp.int32(PROJ_ORDER[-1])
        for step in range(n_blk - 1):
            cb = jnp.where(j == step, PROJ_ORDER[step], cb)
        return cb

    body = functools.partial(_proj_body, layer=l)
    return pl.pallas_call(
        body,
        grid=(m // tm, n_blk),
        in_specs=[
            pl.BlockSpec((tm, d), lambda i, j: (i, 0)),
            pl.BlockSpec((None, d, tn), lambda i, j: (l, 0, wcol(j))),
            pl.BlockSpec(hg_lb.shape, lambda i, j: (0, 0)),
        ],
        out_specs=[
            pl.BlockSpec((tm, tn), lambda i, j: (i, jnp.where(j >= 5, 1, 0))),
            pl.BlockSpec((tm, tn), lambda i, j: (i, jnp.where(j >= 5, j - 1, j))),
        ],
        out_shape=[jax.ShapeDtypeStruct((m, 2 * tn), F32), jax.ShapeDtypeStruct((m, (n_blk - 1) * tn), BF16)],
        compiler_params=_params(("parallel", "arbitrary")),
        name=f"proj_l{l}",
    )(xb, w, hg_lb)


def _rglru_body(*refs, n_seq, t_rows, n_t, n_alias):
    xr_ref, gr_ref, c0_ref, h0_ref, cw_ref, cb_ref, wa_ref, ba_ref, wi_ref, bi_ref, lam_ref = refs[:11]
    ya_ref, hl_ref, cn_ref, xbuf, a_scr, b_scr, hcar = refs[11 + n_alias:]
    t = pl.program_id(1)
    c = a_scr.shape[1]
    hist = CONV_W - 1
    base = SUBLANES - hist

    @pl.when(t == 0)
    def _():
        for g in range(n_seq):
            xbuf[g, base:SUBLANES, :] = c0_ref[g]
            hcar[g] = h0_ref[g]

    for g in range(n_seq):
        xbuf[g, SUBLANES:SUBLANES + t_rows, :] = xr_ref[g * t_rows:(g + 1) * t_rows, :]

    log_a_scale = -C_RG * jax.nn.softplus(-lam_ref[...])
    bs = c // RG_BLOCKS
    rows_total = n_seq * t_rows
    grp = (rows_total // SUBLANES, SUBLANES, bs)

    def row_vec(v):
        return jnp.broadcast_to(v, (SUBLANES, bs))[None]

    def sigmoid(z):
        return 0.5 * jnp.tanh(0.5 * z) + 0.5

    for n in range(RG_BLOCKS):
        ls = slice(n * bs, (n + 1) * bs)
        taps = [row_vec(cw_ref[j:j + 1, ls]) for j in range(CONV_W)]
        for g in range(n_seq):
            xc = row_vec(cb_ref[:, ls]) + sum(
                xbuf[g, base + j:base + j + t_rows, ls].reshape(t_rows // SUBLANES, SUBLANES, bs) * taps[j]
                for j in range(CONV_W))
            a_scr[g * t_rows:(g + 1) * t_rows, ls] = xc.reshape(t_rows, bs)
        xc = a_scr[:, ls]
        xcb = xc.astype(BF16)
        xc = xc.reshape(grp)
        r = sigmoid(jnp.dot(xcb, wa_ref[n], preferred_element_type=F32).reshape(grp) + row_vec(ba_ref[:, ls]))
        i = sigmoid(jnp.dot(xcb, wi_ref[n], preferred_element_type=F32).reshape(grp) + row_vec(bi_ref[:, ls]))
        log_a = row_vec(log_a_scale[:, ls]) * r
        a = jnp.exp(log_a)
        m2 = jnp.tanh(-log_a) * (a * a + 1.0)
        mult = jnp.where(m2 > 0.0, m2 * lax.rsqrt(m2), 0.0)
        a_scr[:, ls] = a.reshape(rows_total, bs)
        b_scr[:, ls] = (mult * (i * xc)).reshape(rows_total, bs)

    for g in range(n_seq):
        xbuf[g, base:SUBLANES, :] = xbuf[g, base + t_rows:SUBLANES + t_rows, :]

    row = lax.broadcasted_iota(jnp.int32, (SUBLANES, c), 0)
    for g in range(n_seq):
        def group(j, h_in, g=g):
            r0 = pl.multiple_of(g * t_rows + j * SUBLANES, SUBLANES)
            a = a_scr[pl.ds(r0, SUBLANES), :]
            b = b_scr[pl.ds(r0, SUBLANES), :]
            for s in (1, 2, 4):
                keep = row >= s
                b = jnp.where(keep, a * pltpu.roll(b, s, 0) + b, b)
                a = jnp.where(keep, a * pltpu.roll(a, s, 0), a)
            h = a * h_in + b
            a_scr[pl.ds(r0, SUBLANES), :] = h
            return h[SUBLANES - 1:SUBLANES, :]

        hcar[g] = lax.fori_loop(0, t_rows // SUBLANES, group, hcar[g])

    for n in range(RG_BLOCKS):
        ls = slice(n * bs, (n + 1) * bs)
        ya_ref[:, ls] = (a_scr[:, ls] * gr_ref[:, ls].astype(F32)).astype(BF16)

    @pl.when(t == n_t - 1)
    def _():
        for g in range(n_seq):
            hl_ref[g] = hcar[g]
            cn_ref[g] = xbuf[g, base:SUBLANES, :]


def _with_aliases(n_in, prev):
    specs, args, amap = [], [], {}
    for out_idx, p in enumerate(prev):
        if p is not None:
            amap[n_in + len(args)] = out_idx
            specs.append(_ANY)
            args.append(p)
    return specs, args, amap


def _rglru(uf, ub, conv0, h0, conv_w, conv_b, wa_b, ba, wi_b, bi, lam, l, prev, *,
           depth, total_rows, row0, n_batch, seq_len, n_seq, t_rows, tag):
    c = conv_w.shape[-1]
    n_t = seq_len // t_rows
    r = n_seq * t_rows
    assert seq_len % t_rows == 0 and n_batch % n_seq == 0 and row0 % r == 0 and t_rows % SUBLANES == 0
    assert n_seq == 1 or n_t == 1
    blk0 = row0 // r
    hist = CONV_W - 1
    a_specs, a_args, amap = _with_aliases(11, prev)
    body = functools.partial(_rglru_body, n_seq=n_seq, t_rows=t_rows, n_t=n_t, n_alias=len(a_args))
    vec = pl.BlockSpec((None, 1, c), lambda s, t: (l, 0, 0))
    gate_w = pl.BlockSpec((None, RG_BLOCKS, c // RG_BLOCKS, c // RG_BLOCKS), lambda s, t: (l, 0, 0, 0))
    return pl.pallas_call(
        body,
        grid=(n_batch // n_seq, n_t),
        in_specs=[
            pl.BlockSpec((r, c), lambda s, t: (blk0 + s * n_t + t, UF_XR)),
            pl.BlockSpec((r, c), lambda s, t: (blk0 + s * n_t + t, UB_GR)),
            pl.BlockSpec((n_seq, hist, c), lambda s, t: (s, 0, 0)),
            pl.BlockSpec((n_seq, 1, c), lambda s, t: (s, 0, 0)),
            pl.BlockSpec((None, CONV_W, c), lambda s, t: (l, 0, 0)),
            vec, gate_w, vec, gate_w, vec, vec,
        ] + a_specs,
        out_specs=[
            pl.BlockSpec((r, c), lambda s, t: (blk0 + s * n_t + t, 0)),
            pl.BlockSpec((None, n_seq, 1, c), lambda s, t: (l, s, 0, 0)),
            pl.BlockSpec((None, n_seq, hist, c), lambda s, t: (l, s, 0, 0)),
        ],
        out_shape=[
            jax.ShapeDtypeStruct((total_rows, c), BF16),
            jax.ShapeDtypeStruct((depth, n_batch, 1, c), F32),
            jax.ShapeDtypeStruct((depth, n_batch, hist, c), F32),
        ],
        scratch_shapes=[
            pltpu.VMEM((n_seq, SUBLANES + t_rows, c), F32),
            pltpu.VMEM((r, c), F32),
            pltpu.VMEM((r, c), F32),
            pltpu.VMEM((n_seq, 1, c), F32),
        ],
        input_output_aliases=amap,
        compiler_params=_params(("arbitrary", "arbitrary")),
        name=f"rglru_{tag}_l{l}",
    )(uf, ub, conv0, h0, conv_w, conv_b, wa_b, ba, wi_b, bi, lam, *a_args)


def _hgrn_body(*refs, n_seq, t_rows, n_t, chunk, n_alias):
    lf_ref, k_ref, q_ref, v_ref, og_ref, s0_ref, gn_ref = refs[:7]
    yb_ref, so_ref, st_scr, p_scr, yo_scr = refs[7 + n_alias:]
    t = pl.program_id(1)
    c = q_ref.shape[1]
    dk = c // HG_HEADS
    n_lane_blk = c // MXU_DIM
    n_rb = chunk // SUBLANES
    n_units = sum(n_rb - s // SUBLANES for s in range(chunk))
    assert n_units % 2 == 0
    n_tiles = n_units // 2

    single_chunk = t_rows == chunk

    if not single_chunk:
        @pl.when(t == 0)
        def _():
            for g in range(n_seq):
                for h in range(HG_HEADS):
                    st_scr[g, h] = s0_ref[g, h].T

    gn = gn_ref[...]

    row_c = lax.broadcasted_iota(jnp.int32, (chunk, c), 0)
    row_8 = lax.broadcasted_iota(jnp.int32, (SUBLANES, MXU_DIM), 0)
    shifts = [s for s in (1, 2, 4, 8, 16) if s < chunk]
    ri = lax.broadcasted_iota(jnp.int32, (MXU_DIM, MXU_DIM), 0)
    ci_ = lax.broadcasted_iota(jnp.int32, (MXU_DIM, MXU_DIM), 1)
    head_ones = ((ri // dk) == (ci_ // dk)).astype(BF16)

    packed_rows = 2 * SUBLANES

    row_h = lax.broadcasted_iota(jnp.int32, (SUBLANES, dk), 0)
    ones_h = jnp.ones((SUBLANES, dk), BF16)

    def chunk_body(g, ci):
        r0 = g * t_rows if single_chunk else pl.multiple_of(g * t_rows + ci * chunk, chunk)
        rows = pl.ds(r0, chunk)

        def load(ref):
            if chunk % packed_rows == 0:
                return ref[rows, :].astype(F32)
            tile0 = (r0 // packed_rows) * packed_rows
            return ref[tile0:tile0 + packed_rows, :].astype(F32)[r0 - tile0:r0 - tile0 + chunk, :]

        b = lf_ref[rows, :] * LOG2_E
        for s in shifts:
            b = b + jnp.where(row_c >= s, pltpu.roll(b, s, 0), 0.0)
        k = load(k_ref)
        q = load(q_ref)
        v = load(v_ref)

        cs = b - jnp.log2(k)
        b_last = b[chunk - 1:chunk, :]
        qe = (q * jnp.exp2(b)).astype(BF16)
        kd = (k * jnp.exp2(b_last - b)).astype(BF16)
        e_last = jnp.exp2(b_last)
        vb = v.astype(BF16)
        units = [(s, tb) for s in range(chunk) for tb in range(s // SUBLANES, n_rb)]

        outs = []
        for j in range(n_lane_blk):
            lj = slice(j * MXU_DIM, (j + 1) * MXU_DIM)
            bb = [b[i * SUBLANES:(i + 1) * SUBLANES, lj] for i in range(n_rb)]
            qq = [q[i * SUBLANES:(i + 1) * SUBLANES, lj] for i in range(n_rb)]
            prods = []
            for s in range(chunk):
                cs_s = jnp.broadcast_to(cs[s:s + 1, lj], (SUBLANES, MXU_DIM))
                for tb in range(s // SUBLANES, n_rb):
                    d = bb[tb] - cs_s
                    if tb == s // SUBLANES:
                        d = jnp.where(row_8 >= (s % SUBLANES), d, _MASKED)
                    prods.append(qq[tb] * jnp.exp2(d))
                    if len(prods) == 2:
                        ti = units.index((s, tb)) // 2
                        p_scr[g, j, ti * 16:(ti + 1) * 16, :] = jnp.concatenate(prods, axis=0).astype(BF16)
                        prods = []
            a_cols = jnp.dot(p_scr[g, j], head_ones, preferred_element_type=F32)
            intra = [jnp.zeros((SUBLANES, MXU_DIM), F32) for _ in range(n_rb)]
            for s in range(chunk):
                v_s = jnp.broadcast_to(v[s:s + 1, lj], (SUBLANES, MXU_DIM))
                for tb in range(s // SUBLANES, n_rb):
                    un = units.index((s, tb))
                    intra[tb] = intra[tb] + a_cols[un * SUBLANES:(un + 1) * SUBLANES, :] * v_s
            intra = jnp.concatenate(intra, axis=0) if n_rb > 1 else intra[0]
            for h in range(j * (MXU_DIM // dk), (j + 1) * (MXU_DIM // dk)):
                ls = slice(h * dk, (h + 1) * dk)
                li = slice(h * dk - j * MXU_DIM, (h + 1) * dk - j * MXU_DIM)
                if single_chunk:
                    s_in = s0_ref[g, h]
                    inter = jnp.dot(qe[:, ls], s_in.astype(BF16), preferred_element_type=F32)
                    upd = lax.dot_general(kd[:, ls], vb[:, ls], _TN, preferred_element_type=F32)
                    e_hi = e_last[:, ls].astype(BF16)
                    rem = e_last[:, ls] - e_hi.astype(F32)
                    e_mid = rem.astype(BF16)
                    e_lo = (rem - e_mid.astype(F32)).astype(BF16)
                    pieces = jnp.where(row_h == 0, e_hi.astype(F32), jnp.where(
                        row_h == 1, e_mid.astype(F32), jnp.where(row_h == 2, e_lo.astype(F32), 0.0)))
                    decay = lax.dot_general(pieces.astype(BF16), ones_h, _TN, preferred_element_type=F32)
                    so_ref[g, h] = s_in * decay + upd
                else:
                    st = st_scr[g, h]
                    inter = lax.dot_general(qe[:, ls], st.astype(BF16), _NT, preferred_element_type=F32)
                    upd = lax.dot_general(vb[:, ls], kd[:, ls], _TN, preferred_element_type=F32)
                    decay = jnp.broadcast_to(e_last[:, ls], (SUBLANES, dk))
                    new = (st.reshape(dk // SUBLANES, SUBLANES, dk) * decay[None]
                           + upd.reshape(dk // SUBLANES, SUBLANES, dk))
                    st_scr[g, h] = new.reshape(dk, dk)
                o = intra[:, li] + inter
                outs.append(o * lax.rsqrt(jnp.mean(o * o, axis=-1, keepdims=True) + RMS_EPS))
        yo_scr[rows, :] = jnp.concatenate(outs, axis=-1) * gn * load(og_ref)

    if single_chunk:
        for g in range(n_seq):
            chunk_body(g, 0)
    else:
        def step(ci, carry):
            for g in range(n_seq):
                chunk_body(g, ci)
            return carry

        lax.fori_loop(0, t_rows // chunk, step, 0)

    yb_ref[...] = yo_scr[...].astype(BF16)

    if not single_chunk:
        @pl.when(t == n_t - 1)
        def _():
            for g in range(n_seq):
                for h in range(HG_HEADS):
                    so_ref[g, h] = st_scr[g, h].T


def _hgrn(uf, ub, s0, s0_layer, gn, l, prev, *, depth, total_rows, row0, n_batch, seq_len, n_seq, t_rows, chunk, tag):
    c = gn.shape[-1]
    dk = c // HG_HEADS
    n_t = seq_len // t_rows
    r = n_seq * t_rows
    assert seq_len % t_rows == 0 and n_batch % n_seq == 0 and row0 % r == 0 and t_rows % chunk == 0
    assert n_seq == 1 or n_t == 1
    assert t_rows != chunk or n_t == 1
    assert chunk % SUBLANES == 0 and c % MXU_DIM == 0 and MXU_DIM % dk == 0
    blk0 = row0 // r
    n_tiles = sum(chunk // SUBLANES - s // SUBLANES for s in range(chunk)) // 2
    a_specs, a_args, amap = _with_aliases(7, prev)
    body = functools.partial(_hgrn_body, n_seq=n_seq, t_rows=t_rows, n_t=n_t, chunk=chunk, n_alias=len(a_args))

    def col(cb):
        return pl.BlockSpec((r, c), lambda s, t: (blk0 + s * n_t + t, cb))

    return pl.pallas_call(
        body,
        grid=(n_batch // n_seq, n_t),
        in_specs=[
            col(UF_LOGF), col(UB_K), col(UB_Q), col(UB_V), col(UB_OG),
            pl.BlockSpec((None, n_seq, HG_HEADS, dk, dk), lambda s, t: (s0_layer, s, 0, 0, 0)),
            pl.BlockSpec((None, 1, c), lambda s, t: (l, 0, 0)),
        ] + a_specs,
        out_specs=[
            pl.BlockSpec((r, c), lambda s, t: (blk0 + s * n_t + t, 0)),
            pl.BlockSpec((None, n_seq, HG_HEADS, dk, dk), lambda s, t: (l, s, 0, 0, 0)),
        ],
        out_shape=[
            jax.ShapeDtypeStruct((total_rows, c), BF16),
            jax.ShapeDtypeStruct((depth, n_batch, HG_HEADS, dk, dk), F32),
        ],
        scratch_shapes=[
            pltpu.VMEM((n_seq, HG_HEADS, dk, dk), F32),
            pltpu.VMEM((n_seq, c // MXU_DIM, n_tiles * 16, MXU_DIM), BF16),
            pltpu.VMEM((r, c), F32),
        ],
        input_output_aliases=amap,
        compiler_params=_params(("arbitrary", "arbitrary")),
        name=f"hgrn_{tag}_l{l}",
    )(uf, ub, ub, ub, ub, s0, gn, *a_args)


def _merge_body(x_ref, ya_ref, yb_ref, ga_ref, gb_ref, wpa_ref, wpb_ref, wo_ref, g_ref, b_ref, o_ref, *, alpha):
    tm = o_ref.shape[0]
    for r in _row_blocks(tm, MERGE_ROW_BLOCKS):
        pa = jnp.dot(ya_ref[r, :], wpa_ref[...], preferred_element_type=F32)
        pb = jnp.dot(yb_ref[r, :], wpb_ref[...], preferred_element_type=F32)
        merged = ga_ref[r, :].astype(F32) * pa + gb_ref[r, :].astype(F32) * pb
        m = jnp.dot(merged.astype(BF16), wo_ref[...], preferred_element_type=F32)
        o_ref[r, :] = _layer_norm(alpha * x_ref[r, :] + m, g_ref[...], b_ref[...])


def _merge(x, ya, yb, ub, wpa_b, wpb_b, wo_b, ln_g, ln_b, l, ln_idx, alpha):
    m, d = x.shape
    c = ya.shape[1]
    tm = TM_MERGE
    assert m % tm == 0
    body = functools.partial(_merge_body, alpha=alpha)
    once = pl.Buffered(1)
    return pl.pallas_call(
        body,
        grid=(m // tm,),
        in_specs=[
            pl.BlockSpec((tm, d), lambda i: (i, 0)),
            pl.BlockSpec((tm, c), lambda i: (i, 0)),
            pl.BlockSpec((tm, c), lambda i: (i, 0)),
            pl.BlockSpec((tm, d), lambda i: (i, UB_SGA)),
            pl.BlockSpec((tm, d), lambda i: (i, UB_SGB)),
            pl.BlockSpec((None, c, d), lambda i: (l, 0, 0), pipeline_mode=once),
            pl.BlockSpec((None, c, d), lambda i: (l, 0, 0), pipeline_mode=once),
            pl.BlockSpec((None, d, d), lambda i: (l, 0, 0), pipeline_mode=once),
            pl.BlockSpec((None, 1, d), lambda i: (ln_idx, 0, 0)),
            pl.BlockSpec((None, 1, d), lambda i: (ln_idx, 0, 0)),
        ],
        out_specs=pl.BlockSpec((tm, d), lambda i: (i, 0)),
        out_shape=jax.ShapeDtypeStruct((m, d), F32),
        compiler_params=_params(("parallel",)),
        name=f"merge_l{l}",
    )(x, ya, yb, ub, ub, wpa_b, wpb_b, wo_b, ln_g, ln_b)


def _assemble_body(xp_ref, xs_ref, o_ref, *, n_prompt_blocks):
    i = pl.program_id(0)

    @pl.when(i < n_prompt_blocks)
    def _():
        o_ref[...] = xp_ref[...]

    @pl.when(i >= n_prompt_blocks)
    def _():
        o_ref[...] = xs_ref[...]


def _meta_body(m_ref, prev_ref, o_ref):
    del prev_ref
    o_ref[...] = m_ref[...]


def _assemble(x_prompt, x_sample2, meta):
    batch, seq, d = x_prompt.shape
    n_meta = meta.shape[0]
    p_len = n_meta + seq
    s_rows = x_sample2.shape[0]
    tr = EXTRACT_ROWS
    per_seq = seq // tr
    n_pb = batch * per_seq
    assert seq % tr == 0 and s_rows % tr == 0 and p_len % SUBLANES == 0 and n_meta % SUBLANES == 0
    total = batch * p_len + s_rows

    def out_row(i):
        ip = jnp.minimum(i, n_pb - 1)
        prompt = (ip // per_seq) * (p_len // SUBLANES) + n_meta // SUBLANES + (ip % per_seq) * (tr // SUBLANES)
        sample = (batch * p_len) // SUBLANES + (i - n_pb) * (tr // SUBLANES)
        return jnp.where(i < n_pb, prompt, sample) * SUBLANES

    x = pl.pallas_call(
        functools.partial(_assemble_body, n_prompt_blocks=n_pb),
        grid=(n_pb + s_rows // tr,),
        in_specs=[
            pl.BlockSpec((None, tr, d), lambda i: (jnp.minimum(i, n_pb - 1) // per_seq,
                                                   jnp.minimum(i, n_pb - 1) % per_seq, 0)),
            pl.BlockSpec((tr, d), lambda i: (jnp.maximum(i - n_pb, 0), 0)),
        ],
        out_specs=pl.BlockSpec((pl.Element(tr), pl.Element(d)), lambda i: (out_row(i), 0)),
        out_shape=jax.ShapeDtypeStruct((total, d), x_prompt.dtype),
        compiler_params=_params(("arbitrary",)),
        name="assemble",
    )(x_prompt, x_sample2)
    return pl.pallas_call(
        _meta_body,
        grid=(batch,),
        in_specs=[pl.BlockSpec((n_meta, d), lambda b: (0, 0)), _ANY],
        out_specs=pl.BlockSpec((pl.Element(n_meta), pl.Element(d)), lambda b: (b * (p_len // SUBLANES) * SUBLANES, 0)),
        out_shape=jax.ShapeDtypeStruct((total, d), x_prompt.dtype),
        input_output_aliases={1: 0},
        compiler_params=_params(("arbitrary",)),
        name="assemble_meta",
    )(meta, x)


def _copy_body(x_ref, o_ref):
    o_ref[...] = x_ref[...]


def _extract_prompt(x, batch, p_len, n_meta):
    d = x.shape[1]
    seq = p_len - n_meta
    tr = EXTRACT_ROWS
    assert seq % tr == 0 and p_len % SUBLANES == 0 and n_meta % SUBLANES == 0 and tr % SUBLANES == 0
    return pl.pallas_call(
        _copy_body,
        grid=(batch, seq // tr),
        in_specs=[pl.BlockSpec(
            (pl.Element(tr), pl.Element(d)),
            lambda b, h: ((b * (p_len // SUBLANES) + n_meta // SUBLANES + h * (tr // SUBLANES)) * SUBLANES, 0))],
        out_specs=pl.BlockSpec((None, tr, d), lambda b, h: (b, h, 0)),
        out_shape=jax.ShapeDtypeStruct((batch, seq, d), x.dtype),
        compiler_params=_params(("parallel", "parallel")),
        name="extract_prompt",
    )(x)


def kernel(x_prompt, x_sample, state_rglru_h, state_rglru_conv, state_hgrn, meta_tokens, ln_g, ln_b, ffn_w_in, ffn_w_out, w_in, conv_w, conv_b, rg_wa, rg_ba, rg_wi, rg_bi, rg_lambda, hg_lb, hg_norm_g, w_pa, w_pb, w_o):
    batch, seq, d_model = x_prompt.shape
    dec_batch, dec_seq, _ = x_sample.shape
    depth = w_in.shape[0]
    n_meta = meta_tokens.shape[0]
    d_rnn = conv_w.shape[-1]
    dk = d_rnn // HG_HEADS
    d_ff = ffn_w_out.shape[-2]
    alpha = (2.0 * depth) ** 0.25
    dt = x_prompt.dtype

    p_len = n_meta + seq
    p_rows = batch * p_len
    total_rows = p_rows + dec_batch * dec_seq
    x = _assemble(x_prompt, x_sample.reshape(dec_batch * dec_seq, d_model), meta_tokens.astype(dt))

    w_in2 = ffn_w_in.reshape(depth * 2 * d_model, 2 * d_ff)
    w_out2 = ffn_w_out.reshape(depth * 2 * d_ff, d_model)
    wpa_b, wpb_b, wo_b = w_pa.astype(BF16), w_pb.astype(BF16), w_o.astype(BF16)
    wa_b, wi_b = rg_wa.astype(BF16), rg_wi.astype(BF16)
    ln_g3 = ln_g.reshape(depth * 3, 1, d_model)
    ln_b3 = ln_b.reshape(depth * 3, 1, d_model)
    vec = lambda a: a.reshape(depth, 1, a.shape[-1])
    conv_b3, ba3, bi3, lam3, gn3 = vec(conv_b), vec(rg_ba), vec(rg_bi), vec(rg_lambda), vec(hg_norm_g)

    zeros_h = jnp.zeros((batch, 1, d_rnn), dt)
    zeros_c = jnp.zeros((batch, CONV_W - 1, d_rnn), dt)
    zeros_s = jnp.zeros((1, batch, HG_HEADS, dk, dk), dt)

    common = dict(depth=depth, total_rows=total_rows)
    prompt = dict(row0=0, n_batch=batch, seq_len=p_len, n_seq=1, t_rows=PROMPT_TILE_ROWS, tag="p", **common)
    sample = dict(row0=p_rows, n_batch=dec_batch, seq_len=dec_seq, n_seq=SAMPLE_SEQS_PER_STEP, t_rows=dec_seq,
                  tag="s", **common)

    ph = pc = ps = sh = sc = ss = None
    for l in range(depth):
        x, xb = _ffn(x, w_in2, w_out2, ln_g3, ln_b3, 2 * l, 3 * l, alpha, emit_bf16=True)
        uf, ub = _proj(xb, w_in, hg_lb, l)
        rg = (conv_w, conv_b3, wa_b, ba3, wi_b, bi3, lam3, l)
        ya, ph, pc = _rglru(uf, ub, zeros_c, zeros_h, *rg, (None, ph, pc), **prompt)
        ya, sh, sc = _rglru(uf, ub, state_rglru_conv[l], state_rglru_h[l][:, None, :], *rg, (ya, sh, sc), **sample)
        yb, ps = _hgrn(uf, ub, zeros_s, 0, gn3, l, (None, ps), chunk=CHUNK, **prompt)
        yb, ss = _hgrn(uf, ub, state_hgrn, l, gn3, l, (yb, ss), chunk=dec_seq, **sample)
        x = _merge(x, ya, yb, ub, wpa_b, wpb_b, wo_b, ln_g3, ln_b3, l, 3 * l + 1, alpha)
        (x,) = _ffn(x, w_in2, w_out2, ln_g3, ln_b3, 2 * l + 1, 3 * l + 2, alpha)

    y_prompt = _extract_prompt(x, batch, p_len, n_meta)
    y_sample = x[p_rows:].reshape(dec_batch, dec_seq, d_model)
    return (y_prompt, y_sample, ph[:, :, 0], pc, ps, sh[:, :, 0], sc, ss)
```

```python
import functools

import jax
import jax.numpy as jnp
from jax import lax
from jax.experimental import pallas as pl
from jax.experimental.pallas import tpu as pltpu

F32 = jnp.float32
BF16 = jnp.bfloat16

CONV_W = 4
C_RG = 8.0
RG_BLOCKS = 8
HG_HEADS = 8
CHUNK = 16
LN_EPS = 1e-5
RMS_EPS = 1e-6

LANES = 128
SUBLANES = 8
MXU_DIM = 256
VMEM_LIMIT_BYTES = 62 * 1024 * 1024

TM_FFN = 928
TF_FFN = 256
TM_PROJ = 1160
TN_PROJ = 1024
TM_MERGE = 464
EXTRACT_ROWS = 1024
FFN_ROW_BLOCKS = 2
MERGE_ROW_BLOCKS = 2
PROMPT_TILE_ROWS = 688
SAMPLE_SEQS_PER_STEP = 8
HGRN_CHUNKS_PER_TRIP = 3

_NT = (((1,), (1,)), ((), ()))
_TN = (((0,), (0,)), ((), ()))
_MASKED = -1e30
LOG2_E = 1.4426950408889634


def _params(semantics):
    return pltpu.CompilerParams(dimension_semantics=semantics, vmem_limit_bytes=VMEM_LIMIT_BYTES)


def _row_blocks(n_rows, n_blocks):
    packed = 2 * SUBLANES
    tiles = -(-n_rows // packed)
    cuts = [min(n_rows, (tiles * i // n_blocks) * packed) for i in range(n_blocks)] + [n_rows]
    return [slice(a, b) for a, b in zip(cuts[:-1], cuts[1:]) if b > a]


def _layer_norm(y, g, b):
    mu = jnp.mean(y, axis=-1, keepdims=True)
    d = y - mu
    var = jnp.mean(d * d, axis=-1, keepdims=True)
    return d * lax.rsqrt(var + LN_EPS) * g + b


_ANY = pl.BlockSpec(memory_space=pl.ANY)


def _ffn_body(x_ref, wg_ref, wu_ref, wo_ref, g_ref, b_ref, o_ref, *rest, tf, n_f, f_dim, alpha):
    xb_ref = rest[-1]
    ob_ref = rest[0] if len(rest) == 2 else None
    j = pl.program_id(1)
    tm = o_ref.shape[0]
    overlap = n_f * tf - f_dim

    def contribution(xb, drop_cols):
        gate = jnp.dot(xb, wg_ref[...].astype(BF16), preferred_element_type=F32)
        up = jnp.dot(xb, wu_ref[...].astype(BF16), preferred_element_type=F32)
        h = (0.5 * gate) * jax.nn.sigmoid(gate) * up
        if drop_cols:
            col = lax.broadcasted_iota(jnp.int32, (1, tf), 1)
            h = jnp.where(col >= drop_cols, h, 0.0)
        return jnp.dot(h.astype(BF16), wo_ref[...].astype(BF16), preferred_element_type=F32)

    @pl.when(j == 0)
    def _():
        xb = x_ref[...].astype(BF16)
        xb_ref[...] = xb
        o_ref[...] = alpha * x_ref[...] + contribution(xb, 0)

    @pl.when((j > 0) & (j < n_f - 1))
    def _():
        o_ref[...] += contribution(xb_ref[...], 0)

    @pl.when(j == n_f - 1)
    def _():
        for r in _row_blocks(tm, FFN_ROW_BLOCKS):
            y = o_ref[r, :] + contribution(xb_ref[r, :], overlap)
            o_ref[r, :] = _layer_norm(y, g_ref[...], b_ref[...])
        if ob_ref is not None:
            ob_ref[...] = o_ref[...].astype(BF16)


def _ffn(x, w_in2, w_out2, ln_g, ln_b, wsel, ln_idx, alpha, emit_bf16=False):
    m, d = x.shape
    f_dim = w_in2.shape[1] // 2
    tf, tm = TF_FFN, TM_FFN
    n_f = -(-f_dim // tf)
    assert m % tm == 0 and tm % SUBLANES == 0 and f_dim % LANES == 0 and tf % LANES == 0 and f_dim >= tf
    assert (n_f * tf - f_dim) < tf and d % SUBLANES == 0 and f_dim % SUBLANES == 0
    last = (f_dim - tf) // LANES
    step = tf // LANES

    def chunk(j):
        return jnp.minimum(j * step, last)

    body = functools.partial(_ffn_body, tf=tf, n_f=n_f, f_dim=f_dim, alpha=alpha)
    return pl.pallas_call(
        body,
        grid=(m // tm, n_f),
        in_specs=[
            pl.BlockSpec((tm, d), lambda i, j: (i, 0)),
            pl.BlockSpec((pl.Element(d), pl.Element(tf)), lambda i, j: (wsel * d, chunk(j) * LANES)),
            pl.BlockSpec((pl.Element(d), pl.Element(tf)),
                         lambda i, j: (wsel * d, (f_dim // LANES + chunk(j)) * LANES)),
            pl.BlockSpec((pl.Element(tf), pl.Element(d)),
                         lambda i, j: ((wsel * (f_dim // SUBLANES) + chunk(j) * (LANES // SUBLANES)) * SUBLANES, 0)),
            pl.BlockSpec((None, 1, d), lambda i, j: (ln_idx, 0, 0)),
            pl.BlockSpec((None, 1, d), lambda i, j: (ln_idx, 0, 0)),
        ],
        out_specs=[pl.BlockSpec((tm, d), lambda i, j: (i, 0))] * (2 if emit_bf16 else 1),
        out_shape=[jax.ShapeDtypeStruct((m, d), F32)] + ([jax.ShapeDtypeStruct((m, d), BF16)] if emit_bf16 else []),
        scratch_shapes=[pltpu.VMEM((tm, d), BF16)],
        compiler_params=_params(("parallel", "arbitrary")),
        name=f"ffn_{wsel}",
    )(x, w_in2, w_in2, w_out2, ln_g, ln_b)


PROJ_ORDER = (6, 7, 8, 9, 3, 0, 1, 2, 4, 5)
UF_LOGF, UF_XR = 0, 1
UB_SGA, UB_SGB, UB_K, UB_GR, UB_Q, UB_V, UB_OG = 0, 1, 4, 5, 6, 7, 8


def _forget_lower_bound(hg_lb, layer):
    e = jnp.exp(hg_lb - jnp.max(hg_lb, axis=0, keepdims=True))
    sm = e / jnp.sum(e, axis=0, keepdims=True)
    cum = sm[0:1, :]
    for m in range(1, layer + 1):
        cum = cum + sm[m:m + 1, :]
    return cum - sm[0:1, :]


def _proj_body(xb_ref, w_ref, lb_ref, uf_ref, ub_ref, *, layer):
    j = pl.program_id(1)

    def acc():
        return jnp.dot(xb_ref[...], w_ref[...].astype(BF16), preferred_element_type=F32)

    col = [PROJ_ORDER.index(cb) for cb in range(len(PROJ_ORDER))]

    @pl.when(j <= col[9])
    def _():
        ub_ref[...] = jax.nn.sigmoid(acc()).astype(BF16)

    @pl.when(j == col[3])
    def _():
        lb = _forget_lower_bound(lb_ref[...], layer)
        f = lb + (1.0 - lb) * jax.nn.sigmoid(acc())
        uf_ref[...] = jnp.log(f)
        ub_ref[...] = (1.0 - f).astype(BF16)

    @pl.when(j == col[0])
    def _():
        uf_ref[...] = acc()

    @pl.when(j == col[1])
    def _():
        ub_ref[...] = jax.nn.gelu(acc()).astype(BF16)

    @pl.when((j == col[2]) | (j == col[4]))
    def _():
        ub_ref[...] = acc().astype(BF16)

    @pl.when(j == col[5])
    def _():
        a = acc()
        ub_ref[...] = (a * jax.nn.sigmoid(a)).astype(BF16)


def _proj(xb, w, hg_lb, l):
    m, d = xb.shape
    tm, tn = TM_PROJ, TN_PROJ
    n_blk = w.shape[-1] // tn
    assert m % tm == 0 and w.shape[-1] % tn == 0 and n_blk == len(PROJ_ORDER)
    assert PROJ_ORDER[:4] == (6, 7, 8, 9) and PROJ_ORDER[4:6] == (3, 0)

    def wcol(j):
        cb = jnp.int32(PROJ_ORDER[-1])
        for step in range(n_blk - 1):
            cb = jnp.where(j == step, PROJ_ORDER[step], cb)
        return cb

    body = functools.partial(_proj_body, layer=l)
    return pl.pallas_call(
        body,
        grid=(m // tm, n_blk),
        in_specs=[
            pl.BlockSpec((tm, d), lambda i, j: (i, 0)),
            pl.BlockSpec((None, d, tn), lambda i, j: (l, 0, wcol(j))),
            pl.BlockSpec(hg_lb.shape, lambda i, j: (0, 0)),
        ],
        out_specs=[
            pl.BlockSpec((tm, tn), lambda i, j: (i, jnp.where(j >= 5, 1, 0))),
            pl.BlockSpec((tm, tn), lambda i, j: (i, jnp.where(j >= 5, j - 1, j))),
        ],
        out_shape=[jax.ShapeDtypeStruct((m, 2 * tn), F32), jax.ShapeDtypeStruct((m, (n_blk - 1) * tn), BF16)],
        compiler_params=_params(("parallel", "arbitrary")),
        name=f"proj_l{l}",
    )(xb, w, hg_lb)


def _rglru_body(*refs, n_seq, t_rows, n_t, n_alias):
    xr_ref, gr_ref, c0_ref, h0_ref, cw_ref, cb_ref, wa_ref, ba_ref, wi_ref, bi_ref, lam_ref = refs[:11]
    ya_ref, hl_ref, cn_ref, xbuf, a_scr, b_scr, hcar = refs[11 + n_alias:]
    t = pl.program_id(1)
    c = a_scr.shape[1]
    hist = CONV_W - 1
    base = SUBLANES - hist

    @pl.when(t == 0)
    def _():
        for g in range(n_seq):
            xbuf[g, base:SUBLANES, :] = c0_ref[g]
            hcar[g] = h0_ref[g]

    for g in range(n_seq):
        xbuf[g, SUBLANES:SUBLANES + t_rows, :] = xr_ref[g * t_rows:(g + 1) * t_rows, :]

    log_a_scale = -C_RG * jax.nn.softplus(-lam_ref[...])
    bs = c // RG_BLOCKS
    rows_total = n_seq * t_rows
    grp = (rows_total // SUBLANES, SUBLANES, bs)

    def row_vec(v):
        return jnp.broadcast_to(v, (SUBLANES, bs))[None]

    def sigmoid(z):
        return 0.5 * jnp.tanh(0.5 * z) + 0.5

    for n in range(RG_BLOCKS):
        ls = slice(n * bs, (n + 1) * bs)
        taps = [row_vec(cw_ref[j:j + 1, ls]) for j in range(CONV_W)]
        for g in range(n_seq):
            xc = row_vec(cb_ref[:, ls]) + sum(
                xbuf[g, base + j:base + j + t_rows, ls].reshape(t_rows // SUBLANES, SUBLANES, bs) * taps[j]
                for j in range(CONV_W))
            a_scr[g * t_rows:(g + 1) * t_rows, ls] = xc.reshape(t_rows, bs)
        xc = a_scr[:, ls]
        xcb = xc.astype(BF16)
        xc = xc.reshape(grp)
        r = sigmoid(jnp.dot(xcb, wa_ref[n], preferred_element_type=F32).reshape(grp) + row_vec(ba_ref[:, ls]))
        i = sigmoid(jnp.dot(xcb, wi_ref[n], preferred_element_type=F32).reshape(grp) + row_vec(bi_ref[:, ls]))
        log_a = row_vec(log_a_scale[:, ls]) * r
        a = jnp.exp(log_a)
        m2 = jnp.tanh(-log_a) * (a * a + 1.0)
        mult = jnp.where(m2 > 0.0, m2 * lax.rsqrt(m2), 0.0)
        a_scr[:, ls] = a.reshape(rows_total, bs)
        b_scr[:, ls] = (mult * (i * xc)).reshape(rows_total, bs)

    for g in range(n_seq):
        xbuf[g, base:SUBLANES, :] = xbuf[g, base + t_rows:SUBLANES + t_rows, :]

    row = lax.broadcasted_iota(jnp.int32, (SUBLANES, c), 0)
    for g in range(n_seq):
        def group(j, h_in, g=g):
            r0 = pl.multiple_of(g * t_rows + j * SUBLANES, SUBLANES)
            a = a_scr[pl.ds(r0, SUBLANES), :]
            b = b_scr[pl.ds(r0, SUBLANES), :]
            for s in (1, 2, 4):
                keep = row >= s
                b = jnp.where(keep, a * pltpu.roll(b, s, 0) + b, b)
                a = jnp.where(keep, a * pltpu.roll(a, s, 0), a)
            h = a * h_in + b
            a_scr[pl.ds(r0, SUBLANES), :] = h
            return h[SUBLANES - 1:SUBLANES, :]

        hcar[g] = lax.fori_loop(0, t_rows // SUBLANES, group, hcar[g])

    for n in range(RG_BLOCKS):
        ls = slice(n * bs, (n + 1) * bs)
        ya_ref[:, ls] = (a_scr[:, ls] * gr_ref[:, ls].astype(F32)).astype(BF16)

    @pl.when(t == n_t - 1)
    def _():
        for g in range(n_seq):
            hl_ref[g] = hcar[g]
            cn_ref[g] = xbuf[g, base:SUBLANES, :]


def _with_aliases(n_in, prev):
    specs, args, amap = [], [], {}
    for out_idx, p in enumerate(prev):
        if p is not None:
            amap[n_in + len(args)] = out_idx
            specs.append(_ANY)
            args.append(p)
    return specs, args, amap


def _rglru(uf, ub, conv0, h0, conv_w, conv_b, wa_b, ba, wi_b, bi, lam, l, prev, *,
           depth, total_rows, row0, n_batch, seq_len, n_seq, t_rows, tag):
    c = conv_w.shape[-1]
    n_t = seq_len // t_rows
    r = n_seq * t_rows
    assert seq_len % t_rows == 0 and n_batch % n_seq == 0 and row0 % r == 0 and t_rows % SUBLANES == 0
    assert n_seq == 1 or n_t == 1
    blk0 = row0 // r
    hist = CONV_W - 1
    a_specs, a_args, amap = _with_aliases(11, prev)
    body = functools.partial(_rglru_body, n_seq=n_seq, t_rows=t_rows, n_t=n_t, n_alias=len(a_args))
    vec = pl.BlockSpec((None, 1, c), lambda s, t: (l, 0, 0))
    gate_w = pl.BlockSpec((None, RG_BLOCKS, c // RG_BLOCKS, c // RG_BLOCKS), lambda s, t: (l, 0, 0, 0))
    return pl.pallas_call(
        body,
        grid=(n_batch // n_seq, n_t),
        in_specs=[
            pl.BlockSpec((r, c), lambda s, t: (blk0 + s * n_t + t, UF_XR)),
            pl.BlockSpec((r, c), lambda s, t: (blk0 + s * n_t + t, UB_GR)),
            pl.BlockSpec((n_seq, hist, c), lambda s, t: (s, 0, 0)),
            pl.BlockSpec((n_seq, 1, c), lambda s, t: (s, 0, 0)),
            pl.BlockSpec((None, CONV_W, c), lambda s, t: (l, 0, 0)),
            vec, gate_w, vec, gate_w, vec, vec,
        ] + a_specs,
        out_specs=[
            pl.BlockSpec((r, c), lambda s, t: (blk0 + s * n_t + t, 0)),
            pl.BlockSpec((None, n_seq, 1, c), lambda s, t: (l, s, 0, 0)),
            pl.BlockSpec((None, n_seq, hist, c), lambda s, t: (l, s, 0, 0)),
        ],
        out_shape=[
            jax.ShapeDtypeStruct((total_rows, c), BF16),
            jax.ShapeDtypeStruct((depth, n_batch, 1, c), F32),
            jax.ShapeDtypeStruct((depth, n_batch, hist, c), F32),
        ],
        scratch_shapes=[
            pltpu.VMEM((n_seq, SUBLANES + t_rows, c), F32),
            pltpu.VMEM((r, c), F32),
            pltpu.VMEM((r, c), F32),
            pltpu.VMEM((n_seq, 1, c), F32),
        ],
        input_output_aliases=amap,
        compiler_params=_params(("arbitrary", "arbitrary")),
        name=f"rglru_{tag}_l{l}",
    )(uf, ub, conv0, h0, conv_w, conv_b, wa_b, ba, wi_b, bi, lam, *a_args)


def _hgrn_body(*refs, n_seq, t_rows, n_t, chunk, n_alias):
    lf_ref, k_ref, q_ref, v_ref, og_ref, s0_ref, gn_ref = refs[:7]
    yb_ref, so_ref, st_scr, p_scr, yo_scr = refs[7 + n_alias:]
    t = pl.program_id(1)
    c = q_ref.shape[1]
    dk = c // HG_HEADS
    n_lane_blk = c // MXU_DIM
    n_rb = chunk // SUBLANES
    n_units = sum(n_rb - s // SUBLANES for s in range(chunk))
    assert n_units % 2 == 0
    n_tiles = n_units // 2

    single_chunk = t_rows == chunk
    n_slots = 1 if single_chunk else HGRN_CHUNKS_PER_TRIP

    if not single_chunk:
        @pl.when(t == 0)
        def _():
            for g in range(n_seq):
                for h in range(HG_HEADS):
                    st_scr[g, h] = s0_ref[g, h].T

    gn = gn_ref[...]

    row_c = lax.broadcasted_iota(jnp.int32, (chunk, c), 0)
    row_8 = lax.broadcasted_iota(jnp.int32, (SUBLANES, MXU_DIM), 0)
    shifts = [s for s in (1, 2, 4, 8, 16) if s < chunk]
    ri = lax.broadcasted_iota(jnp.int32, (MXU_DIM, MXU_DIM), 0)
    ci_ = lax.broadcasted_iota(jnp.int32, (MXU_DIM, MXU_DIM), 1)
    head_ones = ((ri // dk) == (ci_ // dk)).astype(BF16)

    packed_rows = 2 * SUBLANES

    row_h = lax.broadcasted_iota(jnp.int32, (SUBLANES, dk), 0)
    ones_h = jnp.ones((SUBLANES, dk), BF16)

    def chunk_body(g, ci, slot=0):
        r0 = g * t_rows + ci * chunk
        if not isinstance(r0, int):
            r0 = pl.multiple_of(r0, chunk)
        rows = pl.ds(r0, chunk)
        slab = g * n_slots + slot

        def load(ref):
            if chunk % packed_rows == 0:
                return ref[rows, :].astype(F32)
            tile0 = (r0 // packed_rows) * packed_rows
            return ref[tile0:tile0 + packed_rows, :].astype(F32)[r0 - tile0:r0 - tile0 + chunk, :]

        b = lf_ref[rows, :] * LOG2_E
        for s in shifts:
            b = b + jnp.where(row_c >= s, pltpu.roll(b, s, 0), 0.0)
        k = load(k_ref)
        q = load(q_ref)
        v = load(v_ref)

        cs = b - jnp.log2(k)
        b_last = b[chunk - 1:chunk, :]
        qe = (q * jnp.exp2(b)).astype(BF16)
        kd = (k * jnp.exp2(b_last - b)).astype(BF16)
        e_last = jnp.exp2(b_last)
        vb = v.astype(BF16)
        units = [(s, tb) for s in range(chunk) for tb in range(s // SUBLANES, n_rb)]

        outs = []
        for j in range(n_lane_blk):
            lj = slice(j * MXU_DIM, (j + 1) * MXU_DIM)
            bb = [b[i * SUBLANES:(i + 1) * SUBLANES, lj] for i in range(n_rb)]
            qq = [q[i * SUBLANES:(i + 1) * SUBLANES, lj] for i in range(n_rb)]
            prods = []
            for s in range(chunk):
                cs_s = jnp.broadcast_to(cs[s:s + 1, lj], (SUBLANES, MXU_DIM))
                for tb in range(s // SUBLANES, n_rb):
                    d = bb[tb] - cs_s
                    if tb == s // SUBLANES:
                        d = jnp.where(row_8 >= (s % SUBLANES), d, _MASKED)
                    prods.append(qq[tb] * jnp.exp2(d))
                    if len(prods) == 2:
                        ti = units.index((s, tb)) // 2
                        p_scr[slab, j, ti * 16:(ti + 1) * 16, :] = jnp.concatenate(prods, axis=0).astype(BF16)
                        prods = []
            a_cols = jnp.dot(p_scr[slab, j], head_ones, preferred_element_type=F32)
            intra = [jnp.zeros((SUBLANES, MXU_DIM), F32) for _ in range(n_rb)]
            for s in range(chunk):
                v_s = jnp.broadcast_to(v[s:s + 1, lj], (SUBLANES, MXU_DIM))
                for tb in range(s // SUBLANES, n_rb):
                    un = units.index((s, tb))
                    intra[tb] = intra[tb] + a_cols[un * SUBLANES:(un + 1) * SUBLANES, :] * v_s
            intra = jnp.concatenate(intra, axis=0) if n_rb > 1 else intra[0]
            for h in range(j * (MXU_DIM // dk), (j + 1) * (MXU_DIM // dk)):
                ls = slice(h * dk, (h + 1) * dk)
                li = slice(h * dk - j * MXU_DIM, (h + 1) * dk - j * MXU_DIM)
                if single_chunk:
                    s_in = s0_ref[g, h]
                    inter = jnp.dot(qe[:, ls], s_in.astype(BF16), preferred_element_type=F32)
                    upd = lax.dot_general(kd[:, ls], vb[:, ls], _TN, preferred_element_type=F32)
                    e_hi = e_last[:, ls].astype(BF16)
                    rem = e_last[:, ls] - e_hi.astype(F32)
                    e_mid = rem.astype(BF16)
                    e_lo = (rem - e_mid.astype(F32)).astype(BF16)
                    pieces = jnp.where(row_h == 0, e_hi.astype(F32), jnp.where(
                        row_h == 1, e_mid.astype(F32), jnp.where(row_h == 2, e_lo.astype(F32), 0.0)))
                    decay = lax.dot_general(pieces.astype(BF16), ones_h, _TN, preferred_element_type=F32)
                    so_ref[g, h] = s_in * decay + upd
                else:
                    st = st_scr[g, h]
                    inter = lax.dot_general(qe[:, ls], st.astype(BF16), _NT, preferred_element_type=F32)
                    upd = lax.dot_general(vb[:, ls], kd[:, ls], _TN, preferred_element_type=F32)
                    decay = jnp.broadcast_to(e_last[:, ls], (SUBLANES, dk))
                    new = (st.reshape(dk // SUBLANES, SUBLANES, dk) * decay[None]
                           + upd.reshape(dk // SUBLANES, SUBLANES, dk))
                    st_scr[g, h] = new.reshape(dk, dk)
                o = intra[:, li] + inter
                outs.append(o * lax.rsqrt(jnp.mean(o * o, axis=-1, keepdims=True) + RMS_EPS))
        yo_scr[rows, :] = jnp.concatenate(outs, axis=-1) * gn * load(og_ref)

    if single_chunk:
        for g in range(n_seq):
            chunk_body(g, 0)
    else:
        n_chunks = t_rows // chunk

        def step(ci, carry):
            for g in range(n_seq):
                for slot in range(n_slots):
                    chunk_body(g, n_slots * ci + slot, slot)
            return carry

        lax.fori_loop(0, n_chunks // n_slots, step, 0)
        for g in range(n_seq):
            for slot in range(n_chunks % n_slots):
                chunk_body(g, n_chunks - n_chunks % n_slots + slot, slot)

    yb_ref[...] = yo_scr[...].astype(BF16)

    if not single_chunk:
        @pl.when(t == n_t - 1)
        def _():
            for g in range(n_seq):
                for h in range(HG_HEADS):
                    so_ref[g, h] = st_scr[g, h].T


def _hgrn(uf, ub, s0, s0_layer, gn, l, prev, *, depth, total_rows, row0, n_batch, seq_len, n_seq, t_rows, chunk, tag):
    c = gn.shape[-1]
    dk = c // HG_HEADS
    n_t = seq_len // t_rows
    r = n_seq * t_rows
    assert seq_len % t_rows == 0 and n_batch % n_seq == 0 and row0 % r == 0 and t_rows % chunk == 0
    assert n_seq == 1 or n_t == 1
    assert t_rows != chunk or n_t == 1
    assert chunk % SUBLANES == 0 and c % MXU_DIM == 0 and MXU_DIM % dk == 0
    blk0 = row0 // r
    n_tiles = sum(chunk // SUBLANES - s // SUBLANES for s in range(chunk)) // 2
    a_specs, a_args, amap = _with_aliases(7, prev)
    body = functools.partial(_hgrn_body, n_seq=n_seq, t_rows=t_rows, n_t=n_t, chunk=chunk, n_alias=len(a_args))

    def col(cb):
        return pl.BlockSpec((r, c), lambda s, t: (blk0 + s * n_t + t, cb))

    return pl.pallas_call(
        body,
        grid=(n_batch // n_seq, n_t),
        in_specs=[
            col(UF_LOGF), col(UB_K), col(UB_Q), col(UB_V), col(UB_OG),
            pl.BlockSpec((None, n_seq, HG_HEADS, dk, dk), lambda s, t: (s0_layer, s, 0, 0, 0)),
            pl.BlockSpec((None, 1, c), lambda s, t: (l, 0, 0)),
        ] + a_specs,
        out_specs=[
            pl.BlockSpec((r, c), lambda s, t: (blk0 + s * n_t + t, 0)),
            pl.BlockSpec((None, n_seq, HG_HEADS, dk, dk), lambda s, t: (l, s, 0, 0, 0)),
        ],
        out_shape=[
            jax.ShapeDtypeStruct((total_rows, c), BF16),
            jax.ShapeDtypeStruct((depth, n_batch, HG_HEADS, dk, dk), F32),
        ],
        scratch_shapes=[
            pltpu.VMEM((n_seq, HG_HEADS, dk, dk), F32),
            pltpu.VMEM((n_seq * (1 if t_rows == chunk else HGRN_CHUNKS_PER_TRIP), c // MXU_DIM, n_tiles * 16, MXU_DIM),
                       BF16),
            pltpu.VMEM((r, c), F32),
        ],
        input_output_aliases=amap,
        compiler_params=_params(("arbitrary", "arbitrary")),
        name=f"hgrn_{tag}_l{l}",
    )(uf, ub, ub, ub, ub, s0, gn, *a_args)


def _merge_body(x_ref, ya_ref, yb_ref, ga_ref, gb_ref, wpa_ref, wpb_ref, wo_ref, g_ref, b_ref, o_ref, *, alpha):
    tm = o_ref.shape[0]
    for r in _row_blocks(tm, MERGE_ROW_BLOCKS):
        pa = jnp.dot(ya_ref[r, :], wpa_ref[...], preferred_element_type=F32)
        pb = jnp.dot(yb_ref[r, :], wpb_ref[...], preferred_element_type=F32)
        merged = ga_ref[r, :].astype(F32) * pa + gb_ref[r, :].astype(F32) * pb
        m = jnp.dot(merged.astype(BF16), wo_ref[...], preferred_element_type=F32)
        o_ref[r, :] = _layer_norm(alpha * x_ref[r, :] + m, g_ref[...], b_ref[...])


def _merge(x, ya, yb, ub, wpa_b, wpb_b, wo_b, ln_g, ln_b, l, ln_idx, alpha):
    m, d = x.shape
    c = ya.shape[1]
    tm = TM_MERGE
    assert m % tm == 0
    body = functools.partial(_merge_body, alpha=alpha)
    once = pl.Buffered(1)
    return pl.pallas_call(
        body,
        grid=(m // tm,),
        in_specs=[
            pl.BlockSpec((tm, d), lambda i: (i, 0)),
            pl.BlockSpec((tm, c), lambda i: (i, 0)),
            pl.BlockSpec((tm, c), lambda i: (i, 0)),
            pl.BlockSpec((tm, d), lambda i: (i, UB_SGA)),
            pl.BlockSpec((tm, d), lambda i: (i, UB_SGB)),
            pl.BlockSpec((None, c, d), lambda i: (l, 0, 0), pipeline_mode=once),
            pl.BlockSpec((None, c, d), lambda i: (l, 0, 0), pipeline_mode=once),
            pl.BlockSpec((None, d, d), lambda i: (l, 0, 0), pipeline_mode=once),
            pl.BlockSpec((None, 1, d), lambda i: (ln_idx, 0, 0)),
            pl.BlockSpec((None, 1, d), lambda i: (ln_idx, 0, 0)),
        ],
        out_specs=pl.BlockSpec((tm, d), lambda i: (i, 0)),
        out_shape=jax.ShapeDtypeStruct((m, d), F32),
        compiler_params=_params(("parallel",)),
        name=f"merge_l{l}",
    )(x, ya, yb, ub, ub, wpa_b, wpb_b, wo_b, ln_g, ln_b)


def _assemble_body(xp_ref, xs_ref, o_ref, *, n_prompt_blocks):
    i = pl.program_id(0)

    @pl.when(i < n_prompt_blocks)
    def _():
        o_ref[...] = xp_ref[...]

    @pl.when(i >= n_prompt_blocks)
    def _():
        o_ref[...] = xs_ref[...]


def _meta_body(m_ref, prev_ref, o_ref):
    del prev_ref
    o_ref[...] = m_ref[...]


def _assemble(x_prompt, x_sample2, meta):
    batch, seq, d = x_prompt.shape
    n_meta = meta.shape[0]
    p_len = n_meta + seq
    s_rows = x_sample2.shape[0]
    tr = EXTRACT_ROWS
    per_seq = seq // tr
    n_pb = batch * per_seq
    assert seq % tr == 0 and s_rows % tr == 0 and p_len % SUBLANES == 0 and n_meta % SUBLANES == 0
    total = batch * p_len + s_rows

    def out_row(i):
        ip = jnp.minimum(i, n_pb - 1)
        prompt = (ip // per_seq) * (p_len // SUBLANES) + n_meta // SUBLANES + (ip % per_seq) * (tr // SUBLANES)
        sample = (batch * p_len) // SUBLANES + (i - n_pb) * (tr // SUBLANES)
        return jnp.where(i < n_pb, prompt, sample) * SUBLANES

    x = pl.pallas_call(
        functools.partial(_assemble_body, n_prompt_blocks=n_pb),
        grid=(n_pb + s_rows // tr,),
        in_specs=[
            pl.BlockSpec((None, tr, d), lambda i: (jnp.minimum(i, n_pb - 1) // per_seq,
                                                   jnp.minimum(i, n_pb - 1) % per_seq, 0)),
            pl.BlockSpec((tr, d), lambda i: (jnp.maximum(i - n_pb, 0), 0)),
        ],
        out_specs=pl.BlockSpec((pl.Element(tr), pl.Element(d)), lambda i: (out_row(i), 0)),
        out_shape=jax.ShapeDtypeStruct((total, d), x_prompt.dtype),
        compiler_params=_params(("arbitrary",)),
        name="assemble",
    )(x_prompt, x_sample2)
    return pl.pallas_call(
        _meta_body,
        grid=(batch,),
        in_specs=[pl.BlockSpec((n_meta, d), lambda b: (0, 0)), _ANY],
        out_specs=pl.BlockSpec((pl.Element(n_meta), pl.Element(d)), lambda b: (b * (p_len // SUBLANES) * SUBLANES, 0)),
        out_shape=jax.ShapeDtypeStruct((total, d), x_prompt.dtype),
        input_output_aliases={1: 0},
        compiler_params=_params(("arbitrary",)),
        name="assemble_meta",
    )(meta, x)


def _copy_body(x_ref, o_ref):
    o_ref[...] = x_ref[...]


def _extract_prompt(x, batch, p_len, n_meta):
    d = x.shape[1]
    seq = p_len - n_meta
    tr = EXTRACT_ROWS
    assert seq % tr == 0 and p_len % SUBLANES == 0 and n_meta % SUBLANES == 0 and tr % SUBLANES == 0
    return pl.pallas_call(
        _copy_body,
        grid=(batch, seq // tr),
        in_specs=[pl.BlockSpec(
            (pl.Element(tr), pl.Element(d)),
            lambda b, h: ((b * (p_len // SUBLANES) + n_meta // SUBLANES + h * (tr // SUBLANES)) * SUBLANES, 0))],
        out_specs=pl.BlockSpec((None, tr, d), lambda b, h: (b, h, 0)),
        out_shape=jax.ShapeDtypeStruct((batch, seq, d), x.dtype),
        compiler_params=_params(("parallel", "parallel")),
        name="extract_prompt",
    )(x)


def kernel(x_prompt, x_sample, state_rglru_h, state_rglru_conv, state_hgrn, meta_tokens, ln_g, ln_b, ffn_w_in, ffn_w_out, w_in, conv_w, conv_b, rg_wa, rg_ba, rg_wi, rg_bi, rg_lambda, hg_lb, hg_norm_g, w_pa, w_pb, w_o):
    batch, seq, d_model = x_prompt.shape
    dec_batch, dec_seq, _ = x_sample.shape
    depth = w_in.shape[0]
    n_meta = meta_tokens.shape[0]
    d_rnn = conv_w.shape[-1]
    dk = d_rnn // HG_HEADS
    d_ff = ffn_w_out.shape[-2]
    alpha = (2.0 * depth) ** 0.25
    dt = x_prompt.dtype

    p_len = n_meta + seq
    p_rows = batch * p_len
    total_rows = p_rows + dec_batch * dec_seq
    x = _assemble(x_prompt, x_sample.reshape(dec_batch * dec_seq, d_model), meta_tokens.astype(dt))

    w_in2 = ffn_w_in.reshape(depth * 2 * d_model, 2 * d_ff)
    w_out2 = ffn_w_out.reshape(depth * 2 * d_ff, d_model)
    wpa_b, wpb_b, wo_b = w_pa.astype(BF16), w_pb.astype(BF16), w_o.astype(BF16)
    wa_b, wi_b = rg_wa.astype(BF16), rg_wi.astype(BF16)
    ln_g3 = ln_g.reshape(depth * 3, 1, d_model)
    ln_b3 = ln_b.reshape(depth * 3, 1, d_model)
    vec = lambda a: a.reshape(depth, 1, a.shape[-1])
    conv_b3, ba3, bi3, lam3, gn3 = vec(conv_b), vec(rg_ba), vec(rg_bi), vec(rg_lambda), vec(hg_norm_g)

    zeros_h = jnp.zeros((batch, 1, d_rnn), dt)
    zeros_c = jnp.zeros((batch, CONV_W - 1, d_rnn), dt)
    zeros_s = jnp.zeros((1, batch, HG_HEADS, dk, dk), dt)

    common = dict(depth=depth, total_rows=total_rows)
    prompt = dict(row0=0, n_batch=batch, seq_len=p_len, n_seq=1, t_rows=PROMPT_TILE_ROWS, tag="p", **common)
    sample = dict(row0=p_rows, n_batch=dec_batch, seq_len=dec_seq, n_seq=SAMPLE_SEQS_PER_STEP, t_rows=dec_seq,
                  tag="s", **common)

    ph = pc = ps = sh = sc = ss = None
    for l in range(depth):
        x, xb = _ffn(x, w_in2, w_out2, ln_g3, ln_b3, 2 * l, 3 * l, alpha, emit_bf16=True)
        uf, ub = _proj(xb, w_in, hg_lb, l)
        rg = (conv_w, conv_b3, wa_b, ba3, wi_b, bi3, lam3, l)
        ya, ph, pc = _rglru(uf, ub, zeros_c, zeros_h, *rg, (None, ph, pc), **prompt)
        ya, sh, sc = _rglru(uf, ub, state_rglru_conv[l], state_rglru_h[l][:, None, :], *rg, (ya, sh, sc), **sample)
        yb, ps = _hgrn(uf, ub, zeros_s, 0, gn3, l, (None, ps), chunk=CHUNK, **prompt)
        yb, ss = _hgrn(uf, ub, state_hgrn, l, gn3, l, (yb, ss), chunk=dec_seq, **sample)
        x = _merge(x, ya, yb, ub, wpa_b, wpb_b, wo_b, ln_g3, ln_b3, l, 3 * l + 1, alpha)
        (x,) = _ffn(x, w_in2, w_out2, ln_g3, ln_b3, 2 * l + 1, 3 * l + 2, alpha)

    y_prompt = _extract_prompt(x, batch, p_len, n_meta)
    y_sample = x[p_rows:].reshape(dec_batch, dec_seq, d_model)
    return (y_prompt, y_sample, ph[:, :, 0], pc, ps, sh[:, :, 0], sc, ss)
```

```python
import functools

import jax
import jax.numpy as jnp
from jax import lax
from jax.experimental import pallas as pl
from jax.experimental.pallas import tpu as pltpu

F32 = jnp.float32
BF16 = jnp.bfloat16

CONV_W = 4
C_RG = 8.0
RG_BLOCKS = 8
HG_HEADS = 8
CHUNK = 16
LN_EPS = 1e-5
RMS_EPS = 1e-6

LANES = 128
SUBLANES = 8
MXU_DIM = 256
VMEM_LIMIT_BYTES = 62 * 1024 * 1024

TM_FFN = 928
TF_FFN = 256
TM_PROJ = 1160
TN_PROJ = 1024
TM_MERGE = 464
EXTRACT_ROWS = 1024
FFN_ROW_BLOCKS = 2
MERGE_ROW_BLOCKS = 2
PROMPT_TILE_ROWS = 688
SAMPLE_SEQS_PER_STEP = 8
HGRN_CHUNKS_PER_TRIP = 4

_NT = (((1,), (1,)), ((), ()))
_TN = (((0,), (0,)), ((), ()))
_MASKED = -1e30
LOG2_E = 1.4426950408889634


def _params(semantics):
    return pltpu.CompilerParams(dimension_semantics=semantics, vmem_limit_bytes=VMEM_LIMIT_BYTES)


def _row_blocks(n_rows, n_blocks):
    packed = 2 * SUBLANES
    tiles = -(-n_rows // packed)
    cuts = [min(n_rows, (tiles * i // n_blocks) * packed) for i in range(n_blocks)] + [n_rows]
    return [slice(a, b) for a, b in zip(cuts[:-1], cuts[1:]) if b > a]


def _layer_norm(y, g, b):
    mu = jnp.mean(y, axis=-1, keepdims=True)
    d = y - mu
    var = jnp.mean(d * d, axis=-1, keepdims=True)
    return d * lax.rsqrt(var + LN_EPS) * g + b


_ANY = pl.BlockSpec(memory_space=pl.ANY)


def _ffn_body(x_ref, wg_ref, wu_ref, wo_ref, g_ref, b_ref, o_ref, *rest, tf, n_f, f_dim, alpha):
    xb_ref = rest[-1]
    ob_ref = rest[0] if len(rest) == 2 else None
    j = pl.program_id(1)
    tm = o_ref.shape[0]
    overlap = n_f * tf - f_dim

    def contribution(xb, drop_cols):
        gate = jnp.dot(xb, wg_ref[...].astype(BF16), preferred_element_type=F32)
        up = jnp.dot(xb, wu_ref[...].astype(BF16), preferred_element_type=F32)
        h = (0.5 * gate) * jax.nn.sigmoid(gate) * up
        if drop_cols:
            col = lax.broadcasted_iota(jnp.int32, (1, tf), 1)
            h = jnp.where(col >= drop_cols, h, 0.0)
        return jnp.dot(h.astype(BF16), wo_ref[...].astype(BF16), preferred_element_type=F32)

    @pl.when(j == 0)
    def _():
        xb = x_ref[...].astype(BF16)
        xb_ref[...] = xb
        o_ref[...] = alpha * x_ref[...] + contribution(xb, 0)

    @pl.when((j > 0) & (j < n_f - 1))
    def _():
        o_ref[...] += contribution(xb_ref[...], 0)

    @pl.when(j == n_f - 1)
    def _():
        for r in _row_blocks(tm, FFN_ROW_BLOCKS):
            y = _layer_norm(o_ref[r, :] + contribution(xb_ref[r, :], overlap), g_ref[...], b_ref[...])
            o_ref[r, :] = y
            if ob_ref is not None:
                ob_ref[r, :] = y.astype(BF16)


def _ffn(x, w_in2, w_out2, ln_g, ln_b, wsel, ln_idx, alpha, emit_bf16=False):
    m, d = x.shape
    f_dim = w_in2.shape[1] // 2
    tf, tm = TF_FFN, TM_FFN
    n_f = -(-f_dim // tf)
    assert m % tm == 0 and tm % SUBLANES == 0 and f_dim % LANES == 0 and tf % LANES == 0 and f_dim >= tf
    assert (n_f * tf - f_dim) < tf and d % SUBLANES == 0 and f_dim % SUBLANES == 0
    last = (f_dim - tf) // LANES
    step = tf // LANES

    def chunk(j):
        return jnp.minimum(j * step, last)

    body = functools.partial(_ffn_body, tf=tf, n_f=n_f, f_dim=f_dim, alpha=alpha)
    return pl.pallas_call(
        body,
        grid=(m // tm, n_f),
        in_specs=[
            pl.BlockSpec((tm, d), lambda i, j: (i, 0)),
            pl.BlockSpec((pl.Element(d), pl.Element(tf)), lambda i, j: (wsel * d, chunk(j) * LANES)),
            pl.BlockSpec((pl.Element(d), pl.Element(tf)),
                         lambda i, j: (wsel * d, (f_dim // LANES + chunk(j)) * LANES)),
            pl.BlockSpec((pl.Element(tf), pl.Element(d)),
                         lambda i, j: ((wsel * (f_dim // SUBLANES) + chunk(j) * (LANES // SUBLANES)) * SUBLANES, 0)),
            pl.BlockSpec((None, 1, d), lambda i, j: (ln_idx, 0, 0)),
            pl.BlockSpec((None, 1, d), lambda i, j: (ln_idx, 0, 0)),
        ],
        out_specs=[pl.BlockSpec((tm, d), lambda i, j: (i, 0))] * (2 if emit_bf16 else 1),
        out_shape=[jax.ShapeDtypeStruct((m, d), F32)] + ([jax.ShapeDtypeStruct((m, d), BF16)] if emit_bf16 else []),
        scratch_shapes=[pltpu.VMEM((tm, d), BF16)],
        compiler_params=_params(("parallel", "arbitrary")),
        name=f"ffn_{wsel}",
    )(x, w_in2, w_in2, w_out2, ln_g, ln_b)


PROJ_ORDER = (6, 7, 8, 9, 3, 0, 1, 2, 4, 5)
UF_LOGF, UF_XR = 0, 1
UB_SGA, UB_SGB, UB_K, UB_GR, UB_Q, UB_V, UB_OG = 0, 1, 4, 5, 6, 7, 8


def _forget_lower_bound(hg_lb, layer):
    e = jnp.exp(hg_lb - jnp.max(hg_lb, axis=0, keepdims=True))
    sm = e / jnp.sum(e, axis=0, keepdims=True)
    cum = sm[0:1, :]
    for m in range(1, layer + 1):
        cum = cum + sm[m:m + 1, :]
    return cum - sm[0:1, :]


def _proj_body(xb_ref, w_ref, lb_ref, uf_ref, ub_ref, *, layer):
    j = pl.program_id(1)

    def acc():
        return jnp.dot(xb_ref[...], w_ref[...].astype(BF16), preferred_element_type=F32)

    col = [PROJ_ORDER.index(cb) for cb in range(len(PROJ_ORDER))]

    @pl.when(j <= col[9])
    def _():
        ub_ref[...] = jax.nn.sigmoid(acc()).astype(BF16)

    @pl.when(j == col[3])
    def _():
        lb = _forget_lower_bound(lb_ref[...], layer)
        f = lb + (1.0 - lb) * jax.nn.sigmoid(acc())
        uf_ref[...] = jnp.log(f)
        ub_ref[...] = (1.0 - f).astype(BF16)

    @pl.when(j == col[0])
    def _():
        uf_ref[...] = acc()

    @pl.when(j == col[1])
    def _():
        ub_ref[...] = jax.nn.gelu(acc()).astype(BF16)

    @pl.when((j == col[2]) | (j == col[4]))
    def _():
        ub_ref[...] = acc().astype(BF16)

    @pl.when(j == col[5])
    def _():
        a = acc()
        ub_ref[...] = (a * jax.nn.sigmoid(a)).astype(BF16)


def _proj(xb, w, hg_lb, l):
    m, d = xb.shape
    tm, tn = TM_PROJ, TN_PROJ
    n_blk = w.shape[-1] // tn
    assert m % tm == 0 and w.shape[-1] % tn == 0 and n_blk == len(PROJ_ORDER)
    assert PROJ_ORDER[:4] == (6, 7, 8, 9) and PROJ_ORDER[4:6] == (3, 0)

    def wcol(j):
        cb = jnp.int32(PROJ_ORDER[-1])
        for step in range(n_blk - 1):
            cb = jnp.where(j == step, PROJ_ORDER[step], cb)
        return cb

    body = functools.partial(_proj_body, layer=l)
    return pl.pallas_call(
        body,
        grid=(m // tm, n_blk),
        in_specs=[
            pl.BlockSpec((tm, d), lambda i, j: (i, 0)),
            pl.BlockSpec((None, d, tn), lambda i, j: (l, 0, wcol(j))),
            pl.BlockSpec(hg_lb.shape, lambda i, j: (0, 0)),
        ],
        out_specs=[
            pl.BlockSpec((tm, tn), lambda i, j: (i, jnp.where(j >= 5, 1, 0))),
            pl.BlockSpec((tm, tn), lambda i, j: (i, jnp.where(j >= 5, j - 1, j))),
        ],
        out_shape=[jax.ShapeDtypeStruct((m, 2 * tn), F32), jax.ShapeDtypeStruct((m, (n_blk - 1) * tn), BF16)],
        compiler_params=_params(("parallel", "arbitrary")),
        name=f"proj_l{l}",
    )(xb, w, hg_lb)


def _rglru_body(*refs, n_seq, t_rows, n_t, n_alias):
    xr_ref, gr_ref, c0_ref, h0_ref, cw_ref, cb_ref, wa_ref, ba_ref, wi_ref, bi_ref, lam_ref = refs[:11]
    ya_ref, hl_ref, cn_ref, xbuf, a_scr, b_scr, hcar = refs[11 + n_alias:]
    t = pl.program_id(1)
    c = a_scr.shape[1]
    hist = CONV_W - 1
    base = SUBLANES - hist

    @pl.when(t == 0)
    def _():
        for g in range(n_seq):
            xbuf[g, base:SUBLANES, :] = c0_ref[g]
            hcar[g] = h0_ref[g]

    for g in range(n_seq):
        xbuf[g, SUBLANES:SUBLANES + t_rows, :] = xr_ref[g * t_rows:(g + 1) * t_rows, :]

    log_a_scale = -C_RG * jax.nn.softplus(-lam_ref[...])
    bs = c // RG_BLOCKS
    rows_total = n_seq * t_rows
    grp = (rows_total // SUBLANES, SUBLANES, bs)

    def row_vec(v):
        return jnp.broadcast_to(v, (SUBLANES, bs))[None]

    def sigmoid(z):
        return 0.5 * jnp.tanh(0.5 * z) + 0.5

    for n in range(RG_BLOCKS):
        ls = slice(n * bs, (n + 1) * bs)
        taps = [row_vec(cw_ref[j:j + 1, ls]) for j in range(CONV_W)]
        for g in range(n_seq):
            xc = row_vec(cb_ref[:, ls]) + sum(
                xbuf[g, base + j:base + j + t_rows, ls].reshape(t_rows // SUBLANES, SUBLANES, bs) * taps[j]
                for j in range(CONV_W))
            a_scr[g * t_rows:(g + 1) * t_rows, ls] = xc.reshape(t_rows, bs)
        xc = a_scr[:, ls]
        xcb = xc.astype(BF16)
        xc = xc.reshape(grp)
        r = sigmoid(jnp.dot(xcb, wa_ref[n], preferred_element_type=F32).reshape(grp) + row_vec(ba_ref[:, ls]))
        i = sigmoid(jnp.dot(xcb, wi_ref[n], preferred_element_type=F32).reshape(grp) + row_vec(bi_ref[:, ls]))
        log_a = row_vec(log_a_scale[:, ls]) * r
        a = jnp.exp(log_a)
        m2 = jnp.tanh(-log_a) * (a * a + 1.0)
        mult = jnp.where(m2 > 0.0, m2 * lax.rsqrt(m2), 0.0)
        a_scr[:, ls] = a.reshape(rows_total, bs)
        b_scr[:, ls] = (mult * (i * xc)).reshape(rows_total, bs)

    for g in range(n_seq):
        xbuf[g, base:SUBLANES, :] = xbuf[g, base + t_rows:SUBLANES + t_rows, :]

    row = lax.broadcasted_iota(jnp.int32, (SUBLANES, c), 0)
    for g in range(n_seq):
        def group(j, h_in, g=g):
            r0 = pl.multiple_of(g * t_rows + j * SUBLANES, SUBLANES)
            a = a_scr[pl.ds(r0, SUBLANES), :]
            b = b_scr[pl.ds(r0, SUBLANES), :]
            for s in (1, 2, 4):
                keep = row >= s
                b = jnp.where(keep, a * pltpu.roll(b, s, 0) + b, b)
                a = jnp.where(keep, a * pltpu.roll(a, s, 0), a)
            h = a * h_in + b
            a_scr[pl.ds(r0, SUBLANES), :] = h
            return h[SUBLANES - 1:SUBLANES, :]

        hcar[g] = lax.fori_loop(0, t_rows // SUBLANES, group, hcar[g])

    for n in range(RG_BLOCKS):
        ls = slice(n * bs, (n + 1) * bs)
        ya_ref[:, ls] = (a_scr[:, ls] * gr_ref[:, ls].astype(F32)).astype(BF16)

    @pl.when(t == n_t - 1)
    def _():
        for g in range(n_seq):
            hl_ref[g] = hcar[g]
            cn_ref[g] = xbuf[g, base:SUBLANES, :]


def _with_aliases(n_in, prev):
    specs, args, amap = [], [], {}
    for out_idx, p in enumerate(prev):
        if p is not None:
            amap[n_in + len(args)] = out_idx
            specs.append(_ANY)
            args.append(p)
    return specs, args, amap


def _rglru(uf, ub, conv0, h0, conv_w, conv_b, wa_b, ba, wi_b, bi, lam, l, prev, *,
           depth, total_rows, row0, n_batch, seq_len, n_seq, t_rows, tag):
    c = conv_w.shape[-1]
    n_t = seq_len // t_rows
    r = n_seq * t_rows
    assert seq_len % t_rows == 0 and n_batch % n_seq == 0 and row0 % r == 0 and t_rows % SUBLANES == 0
    assert n_seq == 1 or n_t == 1
    blk0 = row0 // r
    hist = CONV_W - 1
    a_specs, a_args, amap = _with_aliases(11, prev)
    body = functools.partial(_rglru_body, n_seq=n_seq, t_rows=t_rows, n_t=n_t, n_alias=len(a_args))
    vec = pl.BlockSpec((None, 1, c), lambda s, t: (l, 0, 0))
    gate_w = pl.BlockSpec((None, RG_BLOCKS, c // RG_BLOCKS, c // RG_BLOCKS), lambda s, t: (l, 0, 0, 0))
    return pl.pallas_call(
        body,
        grid=(n_batch // n_seq, n_t),
        in_specs=[
            pl.BlockSpec((r, c), lambda s, t: (blk0 + s * n_t + t, UF_XR)),
            pl.BlockSpec((r, c), lambda s, t: (blk0 + s * n_t + t, UB_GR)),
            pl.BlockSpec((n_seq, hist, c), lambda s, t: (s, 0, 0)),
            pl.BlockSpec((n_seq, 1, c), lambda s, t: (s, 0, 0)),
            pl.BlockSpec((None, CONV_W, c), lambda s, t: (l, 0, 0)),
            vec, gate_w, vec, gate_w, vec, vec,
        ] + a_specs,
        out_specs=[
            pl.BlockSpec((r, c), lambda s, t: (blk0 + s * n_t + t, 0)),
            pl.BlockSpec((None, n_seq, 1, c), lambda s, t: (l, s, 0, 0)),
            pl.BlockSpec((None, n_seq, hist, c), lambda s, t: (l, s, 0, 0)),
        ],
        out_shape=[
            jax.ShapeDtypeStruct((total_rows, c), BF16),
            jax.ShapeDtypeStruct((depth, n_batch, 1, c), F32),
            jax.ShapeDtypeStruct((depth, n_batch, hist, c), F32),
        ],
        scratch_shapes=[
            pltpu.VMEM((n_seq, SUBLANES + t_rows, c), F32),
            pltpu.VMEM((r, c), F32),
            pltpu.VMEM((r, c), F32),
            pltpu.VMEM((n_seq, 1, c), F32),
        ],
        input_output_aliases=amap,
        compiler_params=_params(("arbitrary", "arbitrary")),
        name=f"rglru_{tag}_l{l}",
    )(uf, ub, conv0, h0, conv_w, conv_b, wa_b, ba, wi_b, bi, lam, *a_args)


def _hgrn_body(*refs, n_seq, t_rows, n_t, chunk, n_alias):
    lf_ref, k_ref, q_ref, v_ref, og_ref, s0_ref, gn_ref = refs[:7]
    yb_ref, so_ref, st_scr, p_scr, yo_scr = refs[7 + n_alias:]
    t = pl.program_id(1)
    c = q_ref.shape[1]
    dk = c // HG_HEADS
    n_lane_blk = c // MXU_DIM
    n_rb = chunk // SUBLANES
    n_units = sum(n_rb - s // SUBLANES for s in range(chunk))
    assert n_units % 2 == 0
    n_tiles = n_units // 2

    single_chunk = t_rows == chunk
    n_slots = 1 if single_chunk else HGRN_CHUNKS_PER_TRIP

    if not single_chunk:
        @pl.when(t == 0)
        def _():
            for g in range(n_seq):
                for h in range(HG_HEADS):
                    st_scr[g, h] = s0_ref[g, h].T

    gn = gn_ref[...]

    row_c = lax.broadcasted_iota(jnp.int32, (chunk, c), 0)
    row_8 = lax.broadcasted_iota(jnp.int32, (SUBLANES, MXU_DIM), 0)
    shifts = [s for s in (1, 2, 4, 8, 16) if s < chunk]
    ri = lax.broadcasted_iota(jnp.int32, (MXU_DIM, MXU_DIM), 0)
    ci_ = lax.broadcasted_iota(jnp.int32, (MXU_DIM, MXU_DIM), 1)
    head_ones = ((ri // dk) == (ci_ // dk)).astype(BF16)

    packed_rows = 2 * SUBLANES

    row_h = lax.broadcasted_iota(jnp.int32, (SUBLANES, dk), 0)
    ones_h = jnp.ones((SUBLANES, dk), BF16)

    def chunk_body(g, ci, slot=0):
        r0 = g * t_rows + ci * chunk
        if not isinstance(r0, int):
            r0 = pl.multiple_of(r0, chunk)
        rows = pl.ds(r0, chunk)
        slab = g * n_slots + slot

        def load(ref):
            if chunk % packed_rows == 0:
                return ref[rows, :].astype(F32)
            tile0 = (r0 // packed_rows) * packed_rows
            return ref[tile0:tile0 + packed_rows, :].astype(F32)[r0 - tile0:r0 - tile0 + chunk, :]

        b = lf_ref[rows, :] * LOG2_E
        for s in shifts:
            b = b + jnp.where(row_c >= s, pltpu.roll(b, s, 0), 0.0)
        k = load(k_ref)
        q = load(q_ref)
        v = load(v_ref)

        cs = b - jnp.log2(k)
        b_last = b[chunk - 1:chunk, :]
        qe = (q * jnp.exp2(b)).astype(BF16)
        kd = (k * jnp.exp2(b_last - b)).astype(BF16)
        e_last = jnp.exp2(b_last)
        vb = v.astype(BF16)
        units = [(s, tb) for s in range(chunk) for tb in range(s // SUBLANES, n_rb)]

        outs = []
        for j in range(n_lane_blk):
            lj = slice(j * MXU_DIM, (j + 1) * MXU_DIM)
            bb = [b[i * SUBLANES:(i + 1) * SUBLANES, lj] for i in range(n_rb)]
            qq = [q[i * SUBLANES:(i + 1) * SUBLANES, lj] for i in range(n_rb)]
            prods = []
            q_tiles = {}
            for s in range(chunk):
                cs_s = jnp.broadcast_to(cs[s:s + 1, lj], (SUBLANES, MXU_DIM))
                for tb in range(s // SUBLANES, n_rb):
                    d = bb[tb] - cs_s
                    if tb == s // SUBLANES:
                        d = jnp.where(row_8 >= (s % SUBLANES), d, _MASKED)
                    prods.append((tb, jnp.exp2(d)))
                    if len(prods) == 2:
                        ti = units.index((s, tb)) // 2
                        key = (prods[0][0], prods[1][0])
                        if key not in q_tiles:
                            q_tiles[key] = jnp.concatenate([qq[key[0]], qq[key[1]]], axis=0).astype(BF16)
                        decay = jnp.concatenate([prods[0][1], prods[1][1]], axis=0).astype(BF16)
                        p_scr[slab, j, ti * 16:(ti + 1) * 16, :] = decay * q_tiles[key]
                        prods = []
            a_cols = jnp.dot(p_scr[slab, j], head_ones, preferred_element_type=F32)
            intra = [jnp.zeros((SUBLANES, MXU_DIM), F32) for _ in range(n_rb)]
            for s in range(chunk):
                v_s = jnp.broadcast_to(v[s:s + 1, lj], (SUBLANES, MXU_DIM))
                for tb in range(s // SUBLANES, n_rb):
                    un = units.index((s, tb))
                    intra[tb] = intra[tb] + a_cols[un * SUBLANES:(un + 1) * SUBLANES, :] * v_s
            intra = jnp.concatenate(intra, axis=0) if n_rb > 1 else intra[0]
            for h in range(j * (MXU_DIM // dk), (j + 1) * (MXU_DIM // dk)):
                ls = slice(h * dk, (h + 1) * dk)
                li = slice(h * dk - j * MXU_DIM, (h + 1) * dk - j * MXU_DIM)
                if single_chunk:
                    s_in = s0_ref[g, h]
                    inter = jnp.dot(qe[:, ls], s_in.astype(BF16), preferred_element_type=F32)
                    upd = lax.dot_general(kd[:, ls], vb[:, ls], _TN, preferred_element_type=F32)
                    e_hi = e_last[:, ls].astype(BF16)
                    rem = e_last[:, ls] - e_hi.astype(F32)
                    e_mid = rem.astype(BF16)
                    e_lo = (rem - e_mid.astype(F32)).astype(BF16)
                    pieces = jnp.where(row_h == 0, e_hi.astype(F32), jnp.where(
                        row_h == 1, e_mid.astype(F32), jnp.where(row_h == 2, e_lo.astype(F32), 0.0)))
                    decay = lax.dot_general(pieces.astype(BF16), ones_h, _TN, preferred_element_type=F32)
                    so_ref[g, h] = s_in * decay + upd
                else:
                    st = st_scr[g, h]
                    inter = lax.dot_general(qe[:, ls], st.astype(BF16), _NT, preferred_element_type=F32)
                    upd = lax.dot_general(vb[:, ls], kd[:, ls], _TN, preferred_element_type=F32)
                    decay = jnp.broadcast_to(e_last[:, ls], (SUBLANES, dk))
                    new = (st.reshape(dk // SUBLANES, SUBLANES, dk) * decay[None]
                           + upd.reshape(dk // SUBLANES, SUBLANES, dk))
                    st_scr[g, h] = new.reshape(dk, dk)
                o = intra[:, li] + inter
                outs.append(o * lax.rsqrt(jnp.mean(o * o, axis=-1, keepdims=True) + RMS_EPS))
        yo_scr[rows, :] = jnp.concatenate(outs, axis=-1) * gn * load(og_ref)

    if single_chunk:
        for g in range(n_seq):
            chunk_body(g, 0)
    else:
        n_chunks = t_rows // chunk

        def step(ci, carry):
            for g in range(n_seq):
                for slot in range(n_slots):
                    chunk_body(g, n_slots * ci + slot, slot)
            return carry

        lax.fori_loop(0, n_chunks // n_slots, step, 0)
        for g in range(n_seq):
            for slot in range(n_chunks % n_slots):
                chunk_body(g, n_chunks - n_chunks % n_slots + slot, slot)

    yb_ref[...] = yo_scr[...].astype(BF16)

    if not single_chunk:
        @pl.when(t == n_t - 1)
        def _():
            for g in range(n_seq):
                for h in range(HG_HEADS):
                    so_ref[g, h] = st_scr[g, h].T


def _hgrn(uf, ub, s0, s0_layer, gn, l, prev, *, depth, total_rows, row0, n_batch, seq_len, n_seq, t_rows, chunk, tag):
    c = gn.shape[-1]
    dk = c // HG_HEADS
    n_t = seq_len // t_rows
    r = n_seq * t_rows
    assert seq_len % t_rows == 0 and n_batch % n_seq == 0 and row0 % r == 0 and t_rows % chunk == 0
    assert n_seq == 1 or n_t == 1
    assert t_rows != chunk or n_t == 1
    assert chunk % SUBLANES == 0 and c % MXU_DIM == 0 and MXU_DIM % dk == 0
    blk0 = row0 // r
    n_tiles = sum(chunk // SUBLANES - s // SUBLANES for s in range(chunk)) // 2
    a_specs, a_args, amap = _with_aliases(7, prev)
    body = functools.partial(_hgrn_body, n_seq=n_seq, t_rows=t_rows, n_t=n_t, chunk=chunk, n_alias=len(a_args))

    def col(cb):
        return pl.BlockSpec((r, c), lambda s, t: (blk0 + s * n_t + t, cb))

    return pl.pallas_call(
        body,
        grid=(n_batch // n_seq, n_t),
        in_specs=[
            col(UF_LOGF), col(UB_K), col(UB_Q), col(UB_V), col(UB_OG),
            pl.BlockSpec((None, n_seq, HG_HEADS, dk, dk), lambda s, t: (s0_layer, s, 0, 0, 0)),
            pl.BlockSpec((None, 1, c), lambda s, t: (l, 0, 0)),
        ] + a_specs,
        out_specs=[
            pl.BlockSpec((r, c), lambda s, t: (blk0 + s * n_t + t, 0)),
            pl.BlockSpec((None, n_seq, HG_HEADS, dk, dk), lambda s, t: (l, s, 0, 0, 0)),
        ],
        out_shape=[
            jax.ShapeDtypeStruct((total_rows, c), BF16),
            jax.ShapeDtypeStruct((depth, n_batch, HG_HEADS, dk, dk), F32),
        ],
        scratch_shapes=[
            pltpu.VMEM((n_seq, HG_HEADS, dk, dk), F32),
            pltpu.VMEM((n_seq * (1 if t_rows == chunk else HGRN_CHUNKS_PER_TRIP), c // MXU_DIM, n_tiles * 16, MXU_DIM),
                       BF16),
            pltpu.VMEM((r, c), F32),
        ],
        input_output_aliases=amap,
        compiler_params=_params(("arbitrary", "arbitrary")),
        name=f"hgrn_{tag}_l{l}",
    )(uf, ub, ub, ub, ub, s0, gn, *a_args)


def _merge_body(x_ref, ya_ref, yb_ref, ga_ref, gb_ref, wpa_ref, wpb_ref, wo_ref, g_ref, b_ref, o_ref, *, alpha):
    tm = o_ref.shape[0]
    for r in _row_blocks(tm, MERGE_ROW_BLOCKS):
        pa = jnp.dot(ya_ref[r, :], wpa_ref[...], preferred_element_type=F32)
        pb = jnp.dot(yb_ref[r, :], wpb_ref[...], preferred_element_type=F32)
        merged = ga_ref[r, :].astype(F32) * pa + gb_ref[r, :].astype(F32) * pb
        m = jnp.dot(merged.astype(BF16), wo_ref[...], preferred_element_type=F32)
        o_ref[r, :] = _layer_norm(alpha * x_ref[r, :] + m, g_ref[...], b_ref[...])


def _merge(x, ya, yb, ub, wpa_b, wpb_b, wo_b, ln_g, ln_b, l, ln_idx, alpha):
    m, d = x.shape
    c = ya.shape[1]
    tm = TM_MERGE
    assert m % tm == 0
    body = functools.partial(_merge_body, alpha=alpha)
    once = pl.Buffered(1)
    return pl.pallas_call(
        body,
        grid=(m // tm,),
        in_specs=[
            pl.BlockSpec((tm, d), lambda i: (i, 0)),
            pl.BlockSpec((tm, c), lambda i: (i, 0)),
            pl.BlockSpec((tm, c), lambda i: (i, 0)),
            pl.BlockSpec((tm, d), lambda i: (i, UB_SGA)),
            pl.BlockSpec((tm, d), lambda i: (i, UB_SGB)),
            pl.BlockSpec((None, c, d), lambda i: (l, 0, 0), pipeline_mode=once),
            pl.BlockSpec((None, c, d), lambda i: (l, 0, 0), pipeline_mode=once),
            pl.BlockSpec((None, d, d), lambda i: (l, 0, 0), pipeline_mode=once),
            pl.BlockSpec((None, 1, d), lambda i: (ln_idx, 0, 0)),
            pl.BlockSpec((None, 1, d), lambda i: (ln_idx, 0, 0)),
        ],
        out_specs=pl.BlockSpec((tm, d), lambda i: (i, 0)),
        out_shape=jax.ShapeDtypeStruct((m, d), F32),
        compiler_params=_params(("parallel",)),
        name=f"merge_l{l}",
    )(x, ya, yb, ub, ub, wpa_b, wpb_b, wo_b, ln_g, ln_b)


def _assemble_body(xp_ref, xs_ref, o_ref, *, n_prompt_blocks):
    i = pl.program_id(0)

    @pl.when(i < n_prompt_blocks)
    def _():
        o_ref[...] = xp_ref[...]

    @pl.when(i >= n_prompt_blocks)
    def _():
        o_ref[...] = xs_ref[...]


def _meta_body(m_ref, prev_ref, o_ref):
    del prev_ref
    o_ref[...] = m_ref[...]


def _assemble(x_prompt, x_sample2, meta):
    batch, seq, d = x_prompt.shape
    n_meta = meta.shape[0]
    p_len = n_meta + seq
    s_rows = x_sample2.shape[0]
    tr = EXTRACT_ROWS
    per_seq = seq // tr
    n_pb = batch * per_seq
    assert seq % tr == 0 and s_rows % tr == 0 and p_len % SUBLANES == 0 and n_meta % SUBLANES == 0
    total = batch * p_len + s_rows

    def out_row(i):
        ip = jnp.minimum(i, n_pb - 1)
        prompt = (ip // per_seq) * (p_len // SUBLANES) + n_meta // SUBLANES + (ip % per_seq) * (tr // SUBLANES)
        sample = (batch * p_len) // SUBLANES + (i - n_pb) * (tr // SUBLANES)
        return jnp.where(i < n_pb, prompt, sample) * SUBLANES

    x = pl.pallas_call(
        functools.partial(_assemble_body, n_prompt_blocks=n_pb),
        grid=(n_pb + s_rows // tr,),
        in_specs=[
            pl.BlockSpec((None, tr, d), lambda i: (jnp.minimum(i, n_pb - 1) // per_seq,
                                                   jnp.minimum(i, n_pb - 1) % per_seq, 0)),
            pl.BlockSpec((tr, d), lambda i: (jnp.maximum(i - n_pb, 0), 0)),
        ],
        out_specs=pl.BlockSpec((pl.Element(tr), pl.Element(d)), lambda i: (out_row(i), 0)),
        out_shape=jax.ShapeDtypeStruct((total, d), x_prompt.dtype),
        compiler_params=_params(("arbitrary",)),
        name="assemble",
    )(x_prompt, x_sample2)
    return pl.pallas_call(
        _meta_body,
        grid=(batch,),
        in_specs=[pl.BlockSpec((n_meta, d), lambda b: (0, 0)), _ANY],
        out_specs=pl.BlockSpec((pl.Element(n_meta), pl.Element(d)), lambda b: (b * (p_len // SUBLANES) * SUBLANES, 0)),
        out_shape=jax.ShapeDtypeStruct((total, d), x_prompt.dtype),
        input_output_aliases={1: 0},
        compiler_params=_params(("arbitrary",)),
        name="assemble_meta",
    )(meta, x)


def _copy_body(x_ref, o_ref):
    o_ref[...] = x_ref[...]


def _extract_prompt(x, batch, p_len, n_meta):
    d = x.shape[1]
    seq = p_len - n_meta
    tr = EXTRACT_ROWS
    assert seq % tr == 0 and p_len % SUBLANES == 0 and n_meta % SUBLANES == 0 and tr % SUBLANES == 0
    return pl.pallas_call(
        _copy_body,
        grid=(batch, seq // tr),
        in_specs=[pl.BlockSpec(
            (pl.Element(tr), pl.Element(d)),
            lambda b, h: ((b * (p_len // SUBLANES) + n_meta // SUBLANES + h * (tr // SUBLANES)) * SUBLANES, 0))],
        out_specs=pl.BlockSpec((None, tr, d), lambda b, h: (b, h, 0)),
        out_shape=jax.ShapeDtypeStruct((batch, seq, d), x.dtype),
        compiler_params=_params(("parallel", "parallel")),
        name="extract_prompt",
    )(x)


def kernel(x_prompt, x_sample, state_rglru_h, state_rglru_conv, state_hgrn, meta_tokens, ln_g, ln_b, ffn_w_in, ffn_w_out, w_in, conv_w, conv_b, rg_wa, rg_ba, rg_wi, rg_bi, rg_lambda, hg_lb, hg_norm_g, w_pa, w_pb, w_o):
    batch, seq, d_model = x_prompt.shape
    dec_batch, dec_seq, _ = x_sample.shape
    depth = w_in.shape[0]
    n_meta = meta_tokens.shape[0]
    d_rnn = conv_w.shape[-1]
    dk = d_rnn // HG_HEADS
    d_ff = ffn_w_out.shape[-2]
    alpha = (2.0 * depth) ** 0.25
    dt = x_prompt.dtype

    p_len = n_meta + seq
    p_rows = batch * p_len
    total_rows = p_rows + dec_batch * dec_seq
    x = _assemble(x_prompt, x_sample.reshape(dec_batch * dec_seq, d_model), meta_tokens.astype(dt))

    w_in2 = ffn_w_in.reshape(depth * 2 * d_model, 2 * d_ff)
    w_out2 = ffn_w_out.reshape(depth * 2 * d_ff, d_model)
    wpa_b, wpb_b, wo_b = w_pa.astype(BF16), w_pb.astype(BF16), w_o.astype(BF16)
    wa_b, wi_b = rg_wa.astype(BF16), rg_wi.astype(BF16)
    ln_g3 = ln_g.reshape(depth * 3, 1, d_model)
    ln_b3 = ln_b.reshape(depth * 3, 1, d_model)
    vec = lambda a: a.reshape(depth, 1, a.shape[-1])
    conv_b3, ba3, bi3, lam3, gn3 = vec(conv_b), vec(rg_ba), vec(rg_bi), vec(rg_lambda), vec(hg_norm_g)

    zeros_h = jnp.zeros((batch, 1, d_rnn), dt)
    zeros_c = jnp.zeros((batch, CONV_W - 1, d_rnn), dt)
    zeros_s = jnp.zeros((1, batch, HG_HEADS, dk, dk), dt)

    common = dict(depth=depth, total_rows=total_rows)
    prompt = dict(row0=0, n_batch=batch, seq_len=p_len, n_seq=1, t_rows=PROMPT_TILE_ROWS, tag="p", **common)
    sample = dict(row0=p_rows, n_batch=dec_batch, seq_len=dec_seq, n_seq=SAMPLE_SEQS_PER_STEP, t_rows=dec_seq,
                  tag="s", **common)

    ph = pc = ps = sh = sc = ss = None
    for l in range(depth):
        x, xb = _ffn(x, w_in2, w_out2, ln_g3, ln_b3, 2 * l, 3 * l, alpha, emit_bf16=True)
        uf, ub = _proj(xb, w_in, hg_lb, l)
        rg = (conv_w, conv_b3, wa_b, ba3, wi_b, bi3, lam3, l)
        ya, ph, pc = _rglru(uf, ub, zeros_c, zeros_h, *rg, (None, ph, pc), **prompt)
        ya, sh, sc = _rglru(uf, ub, state_rglru_conv[l], state_rglru_h[l][:, None, :], *rg, (ya, sh, sc), **sample)
        yb, ps = _hgrn(uf, ub, zeros_s, 0, gn3, l, (None, ps), chunk=CHUNK, **prompt)
        yb, ss = _hgrn(uf, ub, state_hgrn, l, gn3, l, (yb, ss), chunk=dec_seq, **sample)
        x = _merge(x, ya, yb, ub, wpa_b, wpb_b, wo_b, ln_g3, ln_b3, l, 3 * l + 1, alpha)
        (x,) = _ffn(x, w_in2, w_out2, ln_g3, ln_b3, 2 * l + 1, 3 * l + 2, alpha)

    y_prompt = _extract_prompt(x, batch, p_len, n_meta)
    y_sample = x[p_rows:].reshape(dec_batch, dec_seq, d_model)
    return (y_prompt, y_sample, ph[:, :, 0], pc, ps, sh[:, :, 0], sc, ss)
```

```python
import functools

import jax
import jax.numpy as jnp
from jax import lax
from jax.experimental import pallas as pl
from jax.experimental.pallas import tpu as pltpu

F32 = jnp.float32
BF16 = jnp.bfloat16

CONV_W = 4
C_RG = 8.0
RG_BLOCKS = 8
HG_HEADS = 8
CHUNK = 16
LN_EPS = 1e-5
RMS_EPS = 1e-6

LANES = 128
SUBLANES = 8
MXU_DIM = 256
VMEM_LIMIT_BYTES = 62 * 1024 * 1024

TM_FFN = 928
TF_FFN = 256
TM_PROJ = 1160
TN_PROJ = 1024
TM_MERGE = 464
EXTRACT_ROWS = 1024
FFN_ROW_BLOCKS = 2
MERGE_ROW_BLOCKS = 2
PROMPT_TILE_ROWS = 688
SAMPLE_SEQS_PER_STEP = 8
HGRN_CHUNKS_PER_TRIP = 3

_NT = (((1,), (1,)), ((), ()))
_TN = (((0,), (0,)), ((), ()))
_MASKED = -1e30
LOG2_E = 1.4426950408889634


def _params(semantics):
    return pltpu.CompilerParams(dimension_semantics=semantics, vmem_limit_bytes=VMEM_LIMIT_BYTES)


def _row_blocks(n_rows, n_blocks):
    packed = 2 * SUBLANES
    tiles = -(-n_rows // packed)
    cuts = [min(n_rows, (tiles * i // n_blocks) * packed) for i in range(n_blocks)] + [n_rows]
    return [slice(a, b) for a, b in zip(cuts[:-1], cuts[1:]) if b > a]


def _layer_norm(y, g, b):
    mu = jnp.mean(y, axis=-1, keepdims=True)
    d = y - mu
    var = jnp.mean(d * d, axis=-1, keepdims=True)
    return d * lax.rsqrt(var + LN_EPS) * g + b


_ANY = pl.BlockSpec(memory_space=pl.ANY)


def _ffn_body(x_ref, wg_ref, wu_ref, wo_ref, g_ref, b_ref, o_ref, *rest, tf, n_f, f_dim, alpha):
    xb_ref = rest[-1]
    ob_ref = rest[0] if len(rest) == 2 else None
    j = pl.program_id(1)
    tm = o_ref.shape[0]
    overlap = n_f * tf - f_dim

    def contribution(xb, drop_cols):
        gate = jnp.dot(xb, wg_ref[...].astype(BF16), preferred_element_type=F32)
        up = jnp.dot(xb, wu_ref[...].astype(BF16), preferred_element_type=F32)
        h = (0.5 * gate) * jax.nn.sigmoid(gate) * up
        if drop_cols:
            col = lax.broadcasted_iota(jnp.int32, (1, tf), 1)
            h = jnp.where(col >= drop_cols, h, 0.0)
        return jnp.dot(h.astype(BF16), wo_ref[...].astype(BF16), preferred_element_type=F32)

    @pl.when(j == 0)
    def _():
        xb = x_ref[...].astype(BF16)
        xb_ref[...] = xb
        o_ref[...] = alpha * x_ref[...] + contribution(xb, 0)

    @pl.when((j > 0) & (j < n_f - 1))
    def _():
        o_ref[...] += contribution(xb_ref[...], 0)

    @pl.when(j == n_f - 1)
    def _():
        for r in _row_blocks(tm, FFN_ROW_BLOCKS):
            y = o_ref[r, :] + contribution(xb_ref[r, :], overlap)
            o_ref[r, :] = _layer_norm(y, g_ref[...], b_ref[...])
        if ob_ref is not None:
            ob_ref[...] = o_ref[...].astype(BF16)


def _ffn(x, w_in2, w_out2, ln_g, ln_b, wsel, ln_idx, alpha, emit_bf16=False):
    m, d = x.shape
    f_dim = w_in2.shape[1] // 2
    tf, tm = TF_FFN, TM_FFN
    n_f = -(-f_dim // tf)
    assert m % tm == 0 and tm % SUBLANES == 0 and f_dim % LANES == 0 and tf % LANES == 0 and f_dim >= tf
    assert (n_f * tf - f_dim) < tf and d % SUBLANES == 0 and f_dim % SUBLANES == 0
    last = (f_dim - tf) // LANES
    step = tf // LANES

    def chunk(j):
        return jnp.minimum(j * step, last)

    body = functools.partial(_ffn_body, tf=tf, n_f=n_f, f_dim=f_dim, alpha=alpha)
    return pl.pallas_call(
        body,
        grid=(m // tm, n_f),
        in_specs=[
            pl.BlockSpec((tm, d), lambda i, j: (i, 0)),
            pl.BlockSpec((pl.Element(d), pl.Element(tf)), lambda i, j: (wsel * d, chunk(j) * LANES)),
            pl.BlockSpec((pl.Element(d), pl.Element(tf)),
                         lambda i, j: (wsel * d, (f_dim // LANES + chunk(j)) * LANES)),
            pl.BlockSpec((pl.Element(tf), pl.Element(d)),
                         lambda i, j: ((wsel * (f_dim // SUBLANES) + chunk(j) * (LANES // SUBLANES)) * SUBLANES, 0)),
            pl.BlockSpec((None, 1, d), lambda i, j: (ln_idx, 0, 0)),
            pl.BlockSpec((None, 1, d), lambda i, j: (ln_idx, 0, 0)),
        ],
        out_specs=[pl.BlockSpec((tm, d), lambda i, j: (i, 0))] * (2 if emit_bf16 else 1),
        out_shape=[jax.ShapeDtypeStruct((m, d), F32)] + ([jax.ShapeDtypeStruct((m, d), BF16)] if emit_bf16 else []),
        scratch_shapes=[pltpu.VMEM((tm, d), BF16)],
        compiler_params=_params(("parallel", "arbitrary")),
        name=f"ffn_{wsel}",
    )(x, w_in2, w_in2, w_out2, ln_g, ln_b)


PROJ_ORDER = (6, 7, 8, 9, 3, 0, 1, 2, 4, 5)
UF_LOGF, UF_XR = 0, 1
UB_SGA, UB_SGB, UB_K, UB_GR, UB_Q, UB_V, UB_OG = 0, 1, 4, 5, 6, 7, 8


def _forget_lower_bound(hg_lb, layer):
    e = jnp.exp(hg_lb - jnp.max(hg_lb, axis=0, keepdims=True))
    sm = e / jnp.sum(e, axis=0, keepdims=True)
    cum = sm[0:1, :]
    for m in range(1, layer + 1):
        cum = cum + sm[m:m + 1, :]
    return cum - sm[0:1, :]


def _proj_body(xb_ref, w_ref, lb_ref, uf_ref, ub_ref, wb_ref, *, layer):
    j = pl.program_id(0)

    @pl.when(pl.program_id(1) == 0)
    def _():
        wb_ref[...] = w_ref[...].astype(BF16)

    def acc():
        return jnp.dot(xb_ref[...], wb_ref[...], preferred_element_type=F32)

    col = [PROJ_ORDER.index(cb) for cb in range(len(PROJ_ORDER))]

    @pl.when(j <= col[9])
    def _():
        ub_ref[...] = jax.nn.sigmoid(acc()).astype(BF16)

    @pl.when(j == col[3])
    def _():
        lb = _forget_lower_bound(lb_ref[...], layer)
        f = lb + (1.0 - lb) * jax.nn.sigmoid(acc())
        uf_ref[...] = jnp.log(f)
        ub_ref[...] = (1.0 - f).astype(BF16)

    @pl.when(j == col[0])
    def _():
        uf_ref[...] = acc()

    @pl.when(j == col[1])
    def _():
        ub_ref[...] = jax.nn.gelu(acc()).astype(BF16)

    @pl.when((j == col[2]) | (j == col[4]))
    def _():
        ub_ref[...] = acc().astype(BF16)

    @pl.when(j == col[5])
    def _():
        a = acc()
        ub_ref[...] = (a * jax.nn.sigmoid(a)).astype(BF16)


def _proj(xb, w, hg_lb, l):
    m, d = xb.shape
    tm, tn = TM_PROJ, TN_PROJ
    n_blk = w.shape[-1] // tn
    assert m % tm == 0 and w.shape[-1] % tn == 0 and n_blk == len(PROJ_ORDER)
    assert PROJ_ORDER[:4] == (6, 7, 8, 9) and PROJ_ORDER[4:6] == (3, 0)

    def wcol(j):
        cb = jnp.int32(PROJ_ORDER[-1])
        for step in range(n_blk - 1):
            cb = jnp.where(j == step, PROJ_ORDER[step], cb)
        return cb

    body = functools.partial(_proj_body, layer=l)
    n_i = m // tm
    last = n_i - 1

    def uf_idx(j, i):
        row = jnp.where(j < 4, 0, jnp.where(j > 5, last, i))
        return row, jnp.where(j >= 5, 1, 0)

    def ub_idx(j, i):
        return jnp.where(j == 5, last, i), jnp.where(j >= 5, j - 1, j)

    return pl.pallas_call(
        body,
        grid=(n_blk, n_i),
        in_specs=[
            pl.BlockSpec((tm, d), lambda j, i: (i, 0)),
            pl.BlockSpec((None, d, tn), lambda j, i: (l, 0, wcol(j))),
            pl.BlockSpec(hg_lb.shape, lambda j, i: (0, 0)),
        ],
        out_specs=[pl.BlockSpec((tm, tn), uf_idx), pl.BlockSpec((tm, tn), ub_idx)],
        out_shape=[jax.ShapeDtypeStruct((m, 2 * tn), F32), jax.ShapeDtypeStruct((m, (n_blk - 1) * tn), BF16)],
        scratch_shapes=[pltpu.VMEM((d, tn), BF16)],
        compiler_params=_params(("arbitrary", "arbitrary")),
        name=f"proj_l{l}",
    )(xb, w, hg_lb)


def _rglru_body(*refs, n_seq, t_rows, n_t, n_alias):
    xr_ref, gr_ref, c0_ref, h0_ref, cw_ref, cb_ref, wa_ref, ba_ref, wi_ref, bi_ref, lam_ref = refs[:11]
    ya_ref, hl_ref, cn_ref, xbuf, a_scr, b_scr, hcar = refs[11 + n_alias:]
    t = pl.program_id(1)
    c = a_scr.shape[1]
    hist = CONV_W - 1
    base = SUBLANES - hist

    @pl.when(t == 0)
    def _():
        for g in range(n_seq):
            xbuf[g, base:SUBLANES, :] = c0_ref[g]
            hcar[g] = h0_ref[g]

    for g in range(n_seq):
        xbuf[g, SUBLANES:SUBLANES + t_rows, :] = xr_ref[g * t_rows:(g + 1) * t_rows, :]

    log_a_scale = -C_RG * jax.nn.softplus(-lam_ref[...])
    bs = c // RG_BLOCKS
    rows_total = n_seq * t_rows
    grp = (rows_total // SUBLANES, SUBLANES, bs)

    def row_vec(v):
        return jnp.broadcast_to(v, (SUBLANES, bs))[None]

    def sigmoid(z):
        return 0.5 * jnp.tanh(0.5 * z) + 0.5

    for n in range(RG_BLOCKS):
        ls = slice(n * bs, (n + 1) * bs)
        taps = [row_vec(cw_ref[j:j + 1, ls]) for j in range(CONV_W)]
        for g in range(n_seq):
            xc = row_vec(cb_ref[:, ls]) + sum(
                xbuf[g, base + j:base + j + t_rows, ls].reshape(t_rows // SUBLANES, SUBLANES, bs) * taps[j]
                for j in range(CONV_W))
            a_scr[g * t_rows:(g + 1) * t_rows, ls] = xc.reshape(t_rows, bs)
        xc = a_scr[:, ls]
        xcb = xc.astype(BF16)
        xc = xc.reshape(grp)
        r = sigmoid(jnp.dot(xcb, wa_ref[n], preferred_element_type=F32).reshape(grp) + row_vec(ba_ref[:, ls]))
        i = sigmoid(jnp.dot(xcb, wi_ref[n], preferred_element_type=F32).reshape(grp) + row_vec(bi_ref[:, ls]))
        log_a = row_vec(log_a_scale[:, ls]) * r
        a = jnp.exp(log_a)
        m2 = jnp.tanh(-log_a) * (a * a + 1.0)
        mult = jnp.where(m2 > 0.0, m2 * lax.rsqrt(m2), 0.0)
        a_scr[:, ls] = a.reshape(rows_total, bs)
        b_scr[:, ls] = (mult * (i * xc)).reshape(rows_total, bs)

    for g in range(n_seq):
        xbuf[g, base:SUBLANES, :] = xbuf[g, base + t_rows:SUBLANES + t_rows, :]

    row = lax.broadcasted_iota(jnp.int32, (SUBLANES, c), 0)
    for g in range(n_seq):
        def group(j, h_in, g=g):
            r0 = pl.multiple_of(g * t_rows + j * SUBLANES, SUBLANES)
            a = a_scr[pl.ds(r0, SUBLANES), :]
            b = b_scr[pl.ds(r0, SUBLANES), :]
            for s in (1, 2, 4):
                keep = row >= s
                b = jnp.where(keep, a * pltpu.roll(b, s, 0) + b, b)
                a = jnp.where(keep, a * pltpu.roll(a, s, 0), a)
            h = a * h_in + b
            a_scr[pl.ds(r0, SUBLANES), :] = h
            return h[SUBLANES - 1:SUBLANES, :]

        hcar[g] = lax.fori_loop(0, t_rows // SUBLANES, group, hcar[g])

    for n in range(RG_BLOCKS):
        ls = slice(n * bs, (n + 1) * bs)
        ya_ref[:, ls] = (a_scr[:, ls] * gr_ref[:, ls].astype(F32)).astype(BF16)

    @pl.when(t == n_t - 1)
    def _():
        for g in range(n_seq):
            hl_ref[g] = hcar[g]
            cn_ref[g] = xbuf[g, base:SUBLANES, :]


def _with_aliases(n_in, prev):
    specs, args, amap = [], [], {}
    for out_idx, p in enumerate(prev):
        if p is not None:
            amap[n_in + len(args)] = out_idx
            specs.append(_ANY)
            args.append(p)
    return specs, args, amap


def _rglru(uf, ub, conv0, h0, conv_w, conv_b, wa_b, ba, wi_b, bi, lam, l, prev, *,
           depth, total_rows, row0, n_batch, seq_len, n_seq, t_rows, tag):
    c = conv_w.shape[-1]
    n_t = seq_len // t_rows
    r = n_seq * t_rows
    assert seq_len % t_rows == 0 and n_batch % n_seq == 0 and row0 % r == 0 and t_rows % SUBLANES == 0
    assert n_seq == 1 or n_t == 1
    blk0 = row0 // r
    hist = CONV_W - 1
    a_specs, a_args, amap = _with_aliases(11, prev)
    body = functools.partial(_rglru_body, n_seq=n_seq, t_rows=t_rows, n_t=n_t, n_alias=len(a_args))
    vec = pl.BlockSpec((None, 1, c), lambda s, t: (l, 0, 0))
    gate_w = pl.BlockSpec((None, RG_BLOCKS, c // RG_BLOCKS, c // RG_BLOCKS), lambda s, t: (l, 0, 0, 0))
    return pl.pallas_call(
        body,
        grid=(n_batch // n_seq, n_t),
        in_specs=[
            pl.BlockSpec((r, c), lambda s, t: (blk0 + s * n_t + t, UF_XR)),
            pl.BlockSpec((r, c), lambda s, t: (blk0 + s * n_t + t, UB_GR)),
            pl.BlockSpec((n_seq, hist, c), lambda s, t: (s, 0, 0)),
            pl.BlockSpec((n_seq, 1, c), lambda s, t: (s, 0, 0)),
            pl.BlockSpec((None, CONV_W, c), lambda s, t: (l, 0, 0)),
            vec, gate_w, vec, gate_w, vec, vec,
        ] + a_specs,
        out_specs=[
            pl.BlockSpec((r, c), lambda s, t: (blk0 + s * n_t + t, 0)),
            pl.BlockSpec((None, n_seq, 1, c), lambda s, t: (l, s, 0, 0)),
            pl.BlockSpec((None, n_seq, hist, c), lambda s, t: (l, s, 0, 0)),
        ],
        out_shape=[
            jax.ShapeDtypeStruct((total_rows, c), BF16),
            jax.ShapeDtypeStruct((depth, n_batch, 1, c), F32),
            jax.ShapeDtypeStruct((depth, n_batch, hist, c), F32),
        ],
        scratch_shapes=[
            pltpu.VMEM((n_seq, SUBLANES + t_rows, c), F32),
            pltpu.VMEM((r, c), F32),
            pltpu.VMEM((r, c), F32),
            pltpu.VMEM((n_seq, 1, c), F32),
        ],
        input_output_aliases=amap,
        compiler_params=_params(("arbitrary", "arbitrary")),
        name=f"rglru_{tag}_l{l}",
    )(uf, ub, conv0, h0, conv_w, conv_b, wa_b, ba, wi_b, bi, lam, *a_args)


def _hgrn_body(*refs, n_seq, t_rows, n_t, chunk, n_alias):
    lf_ref, k_ref, q_ref, v_ref, og_ref, s0_ref, gn_ref = refs[:7]
    yb_ref, so_ref, st_scr, p_scr, yo_scr = refs[7 + n_alias:]
    t = pl.program_id(1)
    c = q_ref.shape[1]
    dk = c // HG_HEADS
    n_lane_blk = c // MXU_DIM
    n_rb = chunk // SUBLANES
    n_units = sum(n_rb - s // SUBLANES for s in range(chunk))
    assert n_units % 2 == 0
    n_tiles = n_units // 2

    single_chunk = t_rows == chunk
    n_slots = 1 if single_chunk else HGRN_CHUNKS_PER_TRIP

    if not single_chunk:
        @pl.when(t == 0)
        def _():
            for g in range(n_seq):
                for h in range(HG_HEADS):
                    st_scr[g, h] = s0_ref[g, h].T

    gn = gn_ref[...]

    row_c = lax.broadcasted_iota(jnp.int32, (chunk, c), 0)
    row_8 = lax.broadcasted_iota(jnp.int32, (SUBLANES, MXU_DIM), 0)
    shifts = [s for s in (1, 2, 4, 8, 16) if s < chunk]
    ri = lax.broadcasted_iota(jnp.int32, (MXU_DIM, MXU_DIM), 0)
    ci_ = lax.broadcasted_iota(jnp.int32, (MXU_DIM, MXU_DIM), 1)
    head_ones = ((ri // dk) == (ci_ // dk)).astype(BF16)

    packed_rows = 2 * SUBLANES

    row_h = lax.broadcasted_iota(jnp.int32, (SUBLANES, dk), 0)
    ones_h = jnp.ones((SUBLANES, dk), BF16)

    def chunk_body(g, ci, slot=0):
        r0 = g * t_rows + ci * chunk
        if not isinstance(r0, int):
            r0 = pl.multiple_of(r0, chunk)
        rows = pl.ds(r0, chunk)
        slab = g * n_slots + slot

        def load(ref):
            if chunk % packed_rows == 0:
                return ref[rows, :].astype(F32)
            tile0 = (r0 // packed_rows) * packed_rows
            return ref[tile0:tile0 + packed_rows, :].astype(F32)[r0 - tile0:r0 - tile0 + chunk, :]

        b = lf_ref[rows, :] * LOG2_E
        for s in shifts:
            b = b + jnp.where(row_c >= s, pltpu.roll(b, s, 0), 0.0)
        k = load(k_ref)
        q = load(q_ref)
        v = load(v_ref)

        cs = b - jnp.log2(k)
        b_last = b[chunk - 1:chunk, :]
        qe = (q * jnp.exp2(b)).astype(BF16)
        kd = (k * jnp.exp2(b_last - b)).astype(BF16)
        e_last = jnp.exp2(b_last)
        vb = v.astype(BF16)
        units = [(s, tb) for s in range(chunk) for tb in range(s // SUBLANES, n_rb)]

        outs = []
        for j in range(n_lane_blk):
            lj = slice(j * MXU_DIM, (j + 1) * MXU_DIM)
            bb = [b[i * SUBLANES:(i + 1) * SUBLANES, lj] for i in range(n_rb)]
            qq = [q[i * SUBLANES:(i + 1) * SUBLANES, lj] for i in range(n_rb)]
            prods = []
            for s in range(chunk):
                cs_s = jnp.broadcast_to(cs[s:s + 1, lj], (SUBLANES, MXU_DIM))
                for tb in range(s // SUBLANES, n_rb):
                    d = bb[tb] - cs_s
                    if tb == s // SUBLANES:
                        d = jnp.where(row_8 >= (s % SUBLANES), d, _MASKED)
                    prods.append(qq[tb] * jnp.exp2(d))
                    if len(prods) == 2:
                        ti = units.index((s, tb)) // 2
                        p_scr[slab, j, ti * 16:(ti + 1) * 16, :] = jnp.concatenate(prods, axis=0).astype(BF16)
                        prods = []
            a_cols = jnp.dot(p_scr[slab, j], head_ones, preferred_element_type=F32)
            intra = [jnp.zeros((SUBLANES, MXU_DIM), F32) for _ in range(n_rb)]
            for s in range(chunk):
                v_s = jnp.broadcast_to(v[s:s + 1, lj], (SUBLANES, MXU_DIM))
                for tb in range(s // SUBLANES, n_rb):
                    un = units.index((s, tb))
                    intra[tb] = intra[tb] + a_cols[un * SUBLANES:(un + 1) * SUBLANES, :] * v_s
            intra = jnp.concatenate(intra, axis=0) if n_rb > 1 else intra[0]
            for h in range(j * (MXU_DIM // dk), (j + 1) * (MXU_DIM // dk)):
                ls = slice(h * dk, (h + 1) * dk)
                li = slice(h * dk - j * MXU_DIM, (h + 1) * dk - j * MXU_DIM)
                if single_chunk:
                    s_in = s0_ref[g, h]
                    inter = jnp.dot(qe[:, ls], s_in.astype(BF16), preferred_element_type=F32)
                    upd = lax.dot_general(kd[:, ls], vb[:, ls], _TN, preferred_element_type=F32)
                    e_hi = e_last[:, ls].astype(BF16)
                    rem = e_last[:, ls] - e_hi.astype(F32)
                    e_mid = rem.astype(BF16)
                    e_lo = (rem - e_mid.astype(F32)).astype(BF16)
                    pieces = jnp.where(row_h == 0, e_hi.astype(F32), jnp.where(
                        row_h == 1, e_mid.astype(F32), jnp.where(row_h == 2, e_lo.astype(F32), 0.0)))
                    decay = lax.dot_general(pieces.astype(BF16), ones_h, _TN, preferred_element_type=F32)
                    so_ref[g, h] = s_in * decay + upd
                else:
                    st = st_scr[g, h]
                    inter = lax.dot_general(qe[:, ls], st.astype(BF16), _NT, preferred_element_type=F32)
                    upd = lax.dot_general(vb[:, ls], kd[:, ls], _TN, preferred_element_type=F32)
                    decay = jnp.broadcast_to(e_last[:, ls], (SUBLANES, dk))
                    new = (st.reshape(dk // SUBLANES, SUBLANES, dk) * decay[None]
                           + upd.reshape(dk // SUBLANES, SUBLANES, dk))
                    st_scr[g, h] = new.reshape(dk, dk)
                o = intra[:, li] + inter
                outs.append(o * lax.rsqrt(jnp.mean(o * o, axis=-1, keepdims=True) + RMS_EPS))
        yo_scr[rows, :] = jnp.concatenate(outs, axis=-1) * gn * load(og_ref)

    if single_chunk:
        for g in range(n_seq):
            chunk_body(g, 0)
    else:
        n_chunks = t_rows // chunk

        def step(ci, carry):
            for g in range(n_seq):
                for slot in range(n_slots):
                    chunk_body(g, n_slots * ci + slot, slot)
            return carry

        lax.fori_loop(0, n_chunks // n_slots, step, 0)
        for g in range(n_seq):
            for slot in range(n_chunks % n_slots):
                chunk_body(g, n_chunks - n_chunks % n_slots + slot, slot)

    yb_ref[...] = yo_scr[...].astype(BF16)

    if not single_chunk:
        @pl.when(t == n_t - 1)
        def _():
            for g in range(n_seq):
                for h in range(HG_HEADS):
                    so_ref[g, h] = st_scr[g, h].T


def _hgrn(uf, ub, s0, s0_layer, gn, l, prev, *, depth, total_rows, row0, n_batch, seq_len, n_seq, t_rows, chunk, tag):
    c = gn.shape[-1]
    dk = c // HG_HEADS
    n_t = seq_len // t_rows
    r = n_seq * t_rows
    assert seq_len % t_rows == 0 and n_batch % n_seq == 0 and row0 % r == 0 and t_rows % chunk == 0
    assert n_seq == 1 or n_t == 1
    assert t_rows != chunk or n_t == 1
    assert chunk % SUBLANES == 0 and c % MXU_DIM == 0 and MXU_DIM % dk == 0
    blk0 = row0 // r
    n_tiles = sum(chunk // SUBLANES - s // SUBLANES for s in range(chunk)) // 2
    a_specs, a_args, amap = _with_aliases(7, prev)
    body = functools.partial(_hgrn_body, n_seq=n_seq, t_rows=t_rows, n_t=n_t, chunk=chunk, n_alias=len(a_args))

    def col(cb):
        return pl.BlockSpec((r, c), lambda s, t: (blk0 + s * n_t + t, cb))

    return pl.pallas_call(
        body,
        grid=(n_batch // n_seq, n_t),
        in_specs=[
            col(UF_LOGF), col(UB_K), col(UB_Q), col(UB_V), col(UB_OG),
            pl.BlockSpec((None, n_seq, HG_HEADS, dk, dk), lambda s, t: (s0_layer, s, 0, 0, 0)),
            pl.BlockSpec((None, 1, c), lambda s, t: (l, 0, 0)),
        ] + a_specs,
        out_specs=[
            pl.BlockSpec((r, c), lambda s, t: (blk0 + s * n_t + t, 0)),
            pl.BlockSpec((None, n_seq, HG_HEADS, dk, dk), lambda s, t: (l, s, 0, 0, 0)),
        ],
        out_shape=[
            jax.ShapeDtypeStruct((total_rows, c), BF16),
            jax.ShapeDtypeStruct((depth, n_batch, HG_HEADS, dk, dk), F32),
        ],
        scratch_shapes=[
            pltpu.VMEM((n_seq, HG_HEADS, dk, dk), F32),
            pltpu.VMEM((n_seq * (1 if t_rows == chunk else HGRN_CHUNKS_PER_TRIP), c // MXU_DIM, n_tiles * 16, MXU_DIM),
                       BF16),
            pltpu.VMEM((r, c), F32),
        ],
        input_output_aliases=amap,
        compiler_params=_params(("arbitrary", "arbitrary")),
        name=f"hgrn_{tag}_l{l}",
    )(uf, ub, ub, ub, ub, s0, gn, *a_args)


def _merge_body(x_ref, ya_ref, yb_ref, ga_ref, gb_ref, wpa_ref, wpb_ref, wo_ref, g_ref, b_ref, o_ref, *, alpha):
    tm = o_ref.shape[0]
    for r in _row_blocks(tm, MERGE_ROW_BLOCKS):
        pa = jnp.dot(ya_ref[r, :], wpa_ref[...], preferred_element_type=F32)
        pb = jnp.dot(yb_ref[r, :], wpb_ref[...], preferred_element_type=F32)
        merged = ga_ref[r, :].astype(F32) * pa + gb_ref[r, :].astype(F32) * pb
        m = jnp.dot(merged.astype(BF16), wo_ref[...], preferred_element_type=F32)
        o_ref[r, :] = _layer_norm(alpha * x_ref[r, :] + m, g_ref[...], b_ref[...])


def _merge(x, ya, yb, ub, wpa_b, wpb_b, wo_b, ln_g, ln_b, l, ln_idx, alpha):
    m, d = x.shape
    c = ya.shape[1]
    tm = TM_MERGE
    assert m % tm == 0
    body = functools.partial(_merge_body, alpha=alpha)
    once = pl.Buffered(1)
    return pl.pallas_call(
        body,
        grid=(m // tm,),
        in_specs=[
            pl.BlockSpec((tm, d), lambda i: (i, 0)),
            pl.BlockSpec((tm, c), lambda i: (i, 0)),
            pl.BlockSpec((tm, c), lambda i: (i, 0)),
            pl.BlockSpec((tm, d), lambda i: (i, UB_SGA)),
            pl.BlockSpec((tm, d), lambda i: (i, UB_SGB)),
            pl.BlockSpec((None, c, d), lambda i: (l, 0, 0), pipeline_mode=once),
            pl.BlockSpec((None, c, d), lambda i: (l, 0, 0), pipeline_mode=once),
            pl.BlockSpec((None, d, d), lambda i: (l, 0, 0), pipeline_mode=once),
            pl.BlockSpec((None, 1, d), lambda i: (ln_idx, 0, 0)),
            pl.BlockSpec((None, 1, d), lambda i: (ln_idx, 0, 0)),
        ],
        out_specs=pl.BlockSpec((tm, d), lambda i: (i, 0)),
        out_shape=jax.ShapeDtypeStruct((m, d), F32),
        compiler_params=_params(("parallel",)),
        name=f"merge_l{l}",
    )(x, ya, yb, ub, ub, wpa_b, wpb_b, wo_b, ln_g, ln_b)


def _assemble_body(xp_ref, xs_ref, o_ref, *, n_prompt_blocks):
    i = pl.program_id(0)

    @pl.when(i < n_prompt_blocks)
    def _():
        o_ref[...] = xp_ref[...]

    @pl.when(i >= n_prompt_blocks)
    def _():
        o_ref[...] = xs_ref[...]


def _meta_body(m_ref, prev_ref, o_ref):
    del prev_ref
    o_ref[...] = m_ref[...]


def _assemble(x_prompt, x_sample2, meta):
    batch, seq, d = x_prompt.shape
    n_meta = meta.shape[0]
    p_len = n_meta + seq
    s_rows = x_sample2.shape[0]
    tr = EXTRACT_ROWS
    per_seq = seq // tr
    n_pb = batch * per_seq
    assert seq % tr == 0 and s_rows % tr == 0 and p_len % SUBLANES == 0 and n_meta % SUBLANES == 0
    total = batch * p_len + s_rows

    def out_row(i):
        ip = jnp.minimum(i, n_pb - 1)
        prompt = (ip // per_seq) * (p_len // SUBLANES) + n_meta // SUBLANES + (ip % per_seq) * (tr // SUBLANES)
        sample = (batch * p_len) // SUBLANES + (i - n_pb) * (tr // SUBLANES)
        return jnp.where(i < n_pb, prompt, sample) * SUBLANES

    x = pl.pallas_call(
        functools.partial(_assemble_body, n_prompt_blocks=n_pb),
        grid=(n_pb + s_rows // tr,),
        in_specs=[
            pl.BlockSpec((None, tr, d), lambda i: (jnp.minimum(i, n_pb - 1) // per_seq,
                                                   jnp.minimum(i, n_pb - 1) % per_seq, 0)),
            pl.BlockSpec((tr, d), lambda i: (jnp.maximum(i - n_pb, 0), 0)),
        ],
        out_specs=pl.BlockSpec((pl.Element(tr), pl.Element(d)), lambda i: (out_row(i), 0)),
        out_shape=jax.ShapeDtypeStruct((total, d), x_prompt.dtype),
        compiler_params=_params(("arbitrary",)),
        name="assemble",
    )(x_prompt, x_sample2)
    return pl.pallas_call(
        _meta_body,
        grid=(batch,),
        in_specs=[pl.BlockSpec((n_meta, d), lambda b: (0, 0)), _ANY],
        out_specs=pl.BlockSpec((pl.Element(n_meta), pl.Element(d)), lambda b: (b * (p_len // SUBLANES) * SUBLANES, 0)),
        out_shape=jax.ShapeDtypeStruct((total, d), x_prompt.dtype),
        input_output_aliases={1: 0},
        compiler_params=_params(("arbitrary",)),
        name="assemble_meta",
    )(meta, x)


def _copy_body(x_ref, o_ref):
    o_ref[...] = x_ref[...]


def _extract_prompt(x, batch, p_len, n_meta):
    d = x.shape[1]
    seq = p_len - n_meta
    tr = EXTRACT_ROWS
    assert seq % tr == 0 and p_len % SUBLANES == 0 and n_meta % SUBLANES == 0 and tr % SUBLANES == 0
    return pl.pallas_call(
        _copy_body,
        grid=(batch, seq // tr),
        in_specs=[pl.BlockSpec(
            (pl.Element(tr), pl.Element(d)),
            lambda b, h: ((b * (p_len // SUBLANES) + n_meta // SUBLANES + h * (tr // SUBLANES)) * SUBLANES, 0))],
        out_specs=pl.BlockSpec((None, tr, d), lambda b, h: (b, h, 0)),
        out_shape=jax.ShapeDtypeStruct((batch, seq, d), x.dtype),
        compiler_params=_params(("parallel", "parallel")),
        name="extract_prompt",
    )(x)


def kernel(x_prompt, x_sample, state_rglru_h, state_rglru_conv, state_hgrn, meta_tokens, ln_g, ln_b, ffn_w_in, ffn_w_out, w_in, conv_w, conv_b, rg_wa, rg_ba, rg_wi, rg_bi, rg_lambda, hg_lb, hg_norm_g, w_pa, w_pb, w_o):
    batch, seq, d_model = x_prompt.shape
    dec_batch, dec_seq, _ = x_sample.shape
    depth = w_in.shape[0]
    n_meta = meta_tokens.shape[0]
    d_rnn = conv_w.shape[-1]
    dk = d_rnn // HG_HEADS
    d_ff = ffn_w_out.shape[-2]
    alpha = (2.0 * depth) ** 0.25
    dt = x_prompt.dtype

    p_len = n_meta + seq
    p_rows = batch * p_len
    total_rows = p_rows + dec_batch * dec_seq
    x = _assemble(x_prompt, x_sample.reshape(dec_batch * dec_seq, d_model), meta_tokens.astype(dt))

    w_in2 = ffn_w_in.reshape(depth * 2 * d_model, 2 * d_ff)
    w_out2 = ffn_w_out.reshape(depth * 2 * d_ff, d_model)
    wpa_b, wpb_b, wo_b = w_pa.astype(BF16), w_pb.astype(BF16), w_o.astype(BF16)
    wa_b, wi_b = rg_wa.astype(BF16), rg_wi.astype(BF16)
    ln_g3 = ln_g.reshape(depth * 3, 1, d_model)
    ln_b3 = ln_b.reshape(depth * 3, 1, d_model)
    vec = lambda a: a.reshape(depth, 1, a.shape[-1])
    conv_b3, ba3, bi3, lam3, gn3 = vec(conv_b), vec(rg_ba), vec(rg_bi), vec(rg_lambda), vec(hg_norm_g)

    zeros_h = jnp.zeros((batch, 1, d_rnn), dt)
    zeros_c = jnp.zeros((batch, CONV_W - 1, d_rnn), dt)
    zeros_s = jnp.zeros((1, batch, HG_HEADS, dk, dk), dt)

    common = dict(depth=depth, total_rows=total_rows)
    prompt = dict(row0=0, n_batch=batch, seq_len=p_len, n_seq=1, t_rows=PROMPT_TILE_ROWS, tag="p", **common)
    sample = dict(row0=p_rows, n_batch=dec_batch, seq_len=dec_seq, n_seq=SAMPLE_SEQS_PER_STEP, t_rows=dec_seq,
                  tag="s", **common)

    ph = pc = ps = sh = sc = ss = None
    for l in range(depth):
        x, xb = _ffn(x, w_in2, w_out2, ln_g3, ln_b3, 2 * l, 3 * l, alpha, emit_bf16=True)
        uf, ub = _proj(xb, w_in, hg_lb, l)
        rg = (conv_w, conv_b3, wa_b, ba3, wi_b, bi3, lam3, l)
        ya, ph, pc = _rglru(uf, ub, zeros_c, zeros_h, *rg, (None, ph, pc), **prompt)
        ya, sh, sc = _rglru(uf, ub, state_rglru_conv[l], state_rglru_h[l][:, None, :], *rg, (ya, sh, sc), **sample)
        yb, ps = _hgrn(uf, ub, zeros_s, 0, gn3, l, (None, ps), chunk=CHUNK, **prompt)
        yb, ss = _hgrn(uf, ub, state_hgrn, l, gn3, l, (yb, ss), chunk=dec_seq, **sample)
        x = _merge(x, ya, yb, ub, wpa_b, wpb_b, wo_b, ln_g3, ln_b3, l, 3 * l + 1, alpha)
        (x,) = _ffn(x, w_in2, w_out2, ln_g3, ln_b3, 2 * l + 1, 3 * l + 2, alpha)

    y_prompt = _extract_prompt(x, batch, p_len, n_meta)
    y_sample = x[p_rows:].reshape(dec_batch, dec_seq, d_model)
    return (y_prompt, y_sample, ph[:, :, 0], pc, ps, sh[:, :, 0], sc, ss)
```
